```python
import jax, jax.numpy as jnp
from jax import lax
import numpy as np

D_MODEL = 1024
BATCH = 4
SEQ = 4096
DEPTH = 2
DEC_BATCH = 32
DEC_SEQ = 1
PAST_LEN = 8192
PAGE_SIZE = 128

N_HEADS = 16
HEAD_DIM = D_MODEL // N_HEADS
D_RNN = D_MODEL
N_LRU_BLOCKS = 8
LRU_BLOCK = D_RNN // N_LRU_BLOCKS
CONV_W = 4
LRU_C = 8.0
D_FF = 3 * D_MODEL
N_EXPERTS = 8
TOP_K = 2
D_FF_EXPERT = 7 * D_MODEL // 2
D_PLE = 256
Q_BLOCK = 128
RMS_EPS = 1e-6
SB_BIAS_INIT = -9.0
N_REC = (DEPTH + 1) // 2
N_ATT = DEPTH // 2

kernel_name = 'hawk_stickbreaking_hybrid_step'


def rmsnorm(x, gain):
    xf = x.astype(jnp.float32)
    y = xf * lax.rsqrt(jnp.mean(xf * xf, axis=-1, keepdims=True) + RMS_EPS)
    return (y * gain.astype(jnp.float32)).astype(x.dtype)


def swiglu(x, w_gu, w_down):
    g, u = jnp.split(x @ w_gu, 2, axis=-1)
    return (jax.nn.silu(g) * u) @ w_down


def moe_swiglu(x, w_router, b_router, w_gu, w_down):
    logits = jnp.einsum('btd,de->bte', x, w_router, preferred_element_type=jnp.float32) + b_router.astype(jnp.float32)
    top_val, top_idx = lax.top_k(logits, TOP_K)
    top_w = jax.nn.softmax(top_val, axis=-1)
    combine = jnp.sum(jax.nn.one_hot(top_idx, N_EXPERTS, dtype=jnp.float32) * top_w[..., None], axis=-2).astype(x.dtype)
    out = jnp.zeros_like(x)
    for e in range(N_EXPERTS):
        out = out + combine[..., e:e + 1] * swiglu(x, w_gu[e], w_down[e])
    return out


def causal_conv(xr, conv_state, conv_w, conv_b):
    t = xr.shape[1]
    xpad = jnp.concatenate([conv_state.astype(xr.dtype), xr], axis=1)
    out = conv_b
    for tap in range(CONV_W):
        out = out + xpad[:, tap:tap + t] * conv_w[tap]
    return out, xpad[:, xpad.shape[1] - (CONV_W - 1):]


def _lin_combine(e1, e2):
    a1, b1 = e1
    a2, b2 = e2
    return a1 * a2, a2 * b1 + b2


def rg_lru(x, h0, w_a, b_a, w_x, b_x, lam):
    b, t, _ = x.shape
    xb = x.reshape(b, t, N_LRU_BLOCKS, LRU_BLOCK)
    r = jax.nn.sigmoid(jnp.einsum('btnc,ncd->btnd', xb, w_a, preferred_element_type=jnp.float32).reshape(b, t, D_RNN) + b_a.astype(jnp.float32))
    ig = jax.nn.sigmoid(jnp.einsum('btnc,ncd->btnd', xb, w_x, preferred_element_type=jnp.float32).reshape(b, t, D_RNN) + b_x.astype(jnp.float32))
    log_a = -LRU_C * r * jax.nn.softplus(-lam.astype(jnp.float32))
    a = jnp.exp(log_a)
    gated_x = jnp.sqrt(-jnp.expm1(2.0 * log_a)) * ig * x.astype(jnp.float32)
    a_cum, h_from_zero = lax.associative_scan(_lin_combine, (a, gated_x), axis=1)
    h = h_from_zero + a_cum * h0.astype(jnp.float32)[:, None, :]
    return h.astype(x.dtype), h[:, -1].astype(x.dtype)


def recurrent_block(x, conv_state, h0, w_in, conv_w, conv_b, w_a, b_a, w_x, b_x, lam, w_out):
    gate, xr = jnp.split(x @ w_in, 2, axis=-1)
    gate = jax.nn.gelu(gate)
    xc, new_conv = causal_conv(xr, conv_state, conv_w, conv_b)
    h, h_last = rg_lru(xc, h0, w_a, b_a, w_x, b_x, lam)
    return (h * gate) @ w_out, new_conv, h_last


def _stick_breaking_block(q, k, v, q_pos, k_pos, sb_bias):
    z = jnp.einsum('bqhd,bkhd->bhqk', q, k, preferred_element_type=jnp.float32) * (HEAD_DIM ** -0.5)
    z = z + sb_bias.astype(jnp.float32)[None, :, None, None]
    causal = k_pos[None, :] < q_pos[:, None]
    log_surv = jnp.where(causal, jax.nn.log_sigmoid(-z), 0.0)
    later = lax.cumsum(log_surv, axis=3, reverse=True) - log_surv
    log_w = jnp.where(causal, jax.nn.log_sigmoid(z) + later, -jnp.inf)
    w = jnp.exp(log_w)
    return jnp.einsum('bhqk,bkhd->bqhd', w.astype(v.dtype), v)


def stick_breaking(q, k, v, q_pos, k_pos, sb_bias):
    b, t = q.shape[0], q.shape[1]
    blk = Q_BLOCK if t % Q_BLOCK == 0 else t
    nb = t // blk
    qb = q.reshape(b, nb, blk, N_HEADS, HEAD_DIM).transpose(1, 0, 2, 3, 4)
    pb = q_pos.reshape(nb, blk)
    out = lax.map(lambda qp: _stick_breaking_block(qp[0], k, v, qp[1], k_pos, sb_bias), (qb, pb))
    return out.transpose(1, 0, 2, 3, 4).reshape(b, t, N_HEADS, HEAD_DIM)


def sb_attention_block(x, k_past, v_past, past_len, w_qkv, w_o, sb_bias):
    b, t, _ = x.shape
    qkv = (x @ w_qkv).reshape(b, t, 3, N_HEADS, HEAD_DIM)
    q, k, v = qkv[:, :, 0], qkv[:, :, 1], qkv[:, :, 2]
    k_all = jnp.concatenate([k_past.astype(k.dtype), k], axis=1)
    v_all = jnp.concatenate([v_past.astype(v.dtype), v], axis=1)
    q_pos = past_len + jnp.arange(t, dtype=jnp.int32)
    k_pos = jnp.arange(past_len + t, dtype=jnp.int32)
    o = stick_breaking(q, k_all, v_all, q_pos, k_pos, sb_bias)
    return o.reshape(b, t, D_MODEL) @ w_o, k, v


def per_layer_embed(h, p, gain, w_proj, w_gate):
    e = p.astype(h.dtype) @ w_proj
    g = jax.nn.sigmoid(rmsnorm(h, gain) @ w_gate)
    return h + g * e


def setup_inputs(seed: int = 0) -> dict:
    key = jax.random.key(seed)
    ks = jax.random.split(key, 40)
    f32 = jnp.float32

    def nrm(k, shape, scale):
        return jax.random.normal(k, shape, f32) * scale

    n_pages = PAST_LEN // PAGE_SIZE
    n_pool = (5 * DEC_BATCH * n_pages) // 4
    page_table = jax.random.permutation(ks[0], n_pool)[:DEC_BATCH * n_pages].reshape(DEC_BATCH, n_pages).astype(jnp.int32)
    u = jax.random.uniform(ks[1], (N_REC, D_RNN), f32, minval=0.9, maxval=0.999)
    a_base = u ** (1.0 / LRU_C)
    rec_lambda = jnp.log(a_base) - jnp.log1p(-a_base)
    return {
        'x_prompt': nrm(ks[2], (BATCH, SEQ, D_MODEL), 1.0),
        'x_sample': nrm(ks[3], (DEC_BATCH, DEC_SEQ, D_MODEL), 1.0),
        'state_conv': nrm(ks[4], (N_REC, DEC_BATCH, CONV_W - 1, D_RNN), 1.0),
        'state_lru': nrm(ks[5], (N_REC, DEC_BATCH, D_RNN), 0.5),
        'cache_k': nrm(ks[6], (N_ATT, n_pool, PAGE_SIZE, N_HEADS, HEAD_DIM), 1.0),
        'cache_v': nrm(ks[7], (N_ATT, n_pool, PAGE_SIZE, N_HEADS, HEAD_DIM), 1.0),
        'page_table': page_table,
        'p_prompt': nrm(ks[8], (DEPTH, BATCH, SEQ, D_PLE), 1.0),
        'p_sample': nrm(ks[9], (DEPTH, DEC_BATCH, DEC_SEQ, D_PLE), 1.0),
        'norm_mix_pre': 1.0 + nrm(ks[10], (DEPTH, D_MODEL), 0.05),
        'norm_mix_post': 1.0 + nrm(ks[11], (DEPTH, D_MODEL), 0.05),
        'norm_ffn_pre': 1.0 + nrm(ks[12], (DEPTH, D_MODEL), 0.05),
        'norm_ffn_post': 1.0 + nrm(ks[13], (DEPTH, D_MODEL), 0.05),
        'norm_ple': 1.0 + nrm(ks[14], (DEPTH, D_MODEL), 0.05),
        'rec_w_in': nrm(ks[15], (N_REC, D_MODEL, 2 * D_RNN), D_MODEL ** -0.5),
        'rec_conv_w': nrm(ks[16], (N_REC, CONV_W, D_RNN), CONV_W ** -0.5),
        'rec_conv_b': nrm(ks[17], (N_REC, D_RNN), 0.1),
        'rec_w_a': nrm(ks[18], (N_REC, N_LRU_BLOCKS, LRU_BLOCK, LRU_BLOCK), LRU_BLOCK ** -0.5),
        'rec_b_a': nrm(ks[19], (N_REC, D_RNN), 0.1),
        'rec_w_x': nrm(ks[20], (N_REC, N_LRU_BLOCKS, LRU_BLOCK, LRU_BLOCK), LRU_BLOCK ** -0.5),
        'rec_b_x': nrm(ks[21], (N_REC, D_RNN), 0.1),
        'rec_lambda': rec_lambda,
        'rec_w_out': nrm(ks[22], (N_REC, D_RNN, D_MODEL), D_RNN ** -0.5),
        'att_w_qkv': nrm(ks[23], (N_ATT, D_MODEL, 3 * D_MODEL), D_MODEL ** -0.5),
        'att_w_o': nrm(ks[24], (N_ATT, D_MODEL, D_MODEL), D_MODEL ** -0.5),
        'att_sb_bias': SB_BIAS_INIT + nrm(ks[33], (N_ATT, N_HEADS), 0.5),
        'ffn_w_gu': nrm(ks[25], (N_REC, D_MODEL, 2 * D_FF), D_MODEL ** -0.5),
        'ffn_w_down': nrm(ks[26], (N_REC, D_FF, D_MODEL), D_FF ** -0.5),
        'moe_w_router': nrm(ks[27], (N_ATT, D_MODEL, N_EXPERTS), D_MODEL ** -0.5),
        'moe_b_router': nrm(ks[28], (N_ATT, N_EXPERTS), 0.01),
        'moe_w_gu': nrm(ks[29], (N_ATT, N_EXPERTS, D_MODEL, 2 * D_FF_EXPERT), D_MODEL ** -0.5),
        'moe_w_down': nrm(ks[30], (N_ATT, N_EXPERTS, D_FF_EXPERT, D_MODEL), D_FF_EXPERT ** -0.5),
        'ple_w_proj': nrm(ks[31], (DEPTH, D_PLE, D_MODEL), D_PLE ** -0.5),
        'ple_w_gate': nrm(ks[32], (DEPTH, D_MODEL, D_MODEL), D_MODEL ** -0.5),
    }


def reference(x_prompt, x_sample, state_conv, state_lru, cache_k, cache_v, page_table, p_prompt, p_sample,
              norm_mix_pre, norm_mix_post, norm_ffn_pre, norm_ffn_post, norm_ple,
              rec_w_in, rec_conv_w, rec_conv_b, rec_w_a, rec_b_a, rec_w_x, rec_b_x, rec_lambda, rec_w_out,
              att_w_qkv, att_w_o, att_sb_bias, ffn_w_gu, ffn_w_down,
              moe_w_router, moe_b_router, moe_w_gu, moe_w_down,
              ple_w_proj, ple_w_gate):
    n_pages = PAST_LEN // PAGE_SIZE
    hp, hs = x_prompt, x_sample
    conv_p, lru_p, k_p, v_p = [], [], [], []
    conv_s, lru_s, k_s, v_s = [], [], [], []
    for i in range(DEPTH):
        j = i // 2
        xp = rmsnorm(hp, norm_mix_pre[i])
        xs = rmsnorm(hs, norm_mix_pre[i])
        if i % 2 == 0:
            rec = (rec_w_in[j], rec_conv_w[j], rec_conv_b[j], rec_w_a[j], rec_b_a[j],
                   rec_w_x[j], rec_b_x[j], rec_lambda[j], rec_w_out[j])
            conv0 = jnp.zeros((BATCH, CONV_W - 1, D_RNN), hp.dtype)
            h0 = jnp.zeros((BATCH, D_RNN), hp.dtype)
            mp, c_new, h_new = recurrent_block(xp, conv0, h0, *rec)
            conv_p.append(c_new)
            lru_p.append(h_new)
            ms, c_new, h_new = recurrent_block(xs, state_conv[j], state_lru[j], *rec)
            conv_s.append(c_new)
            lru_s.append(h_new)
        else:
            empty = jnp.zeros((BATCH, 0, N_HEADS, HEAD_DIM), hp.dtype)
            mp, k_new, v_new = sb_attention_block(xp, empty, empty, 0, att_w_qkv[j], att_w_o[j], att_sb_bias[j])
            k_p.append(k_new)
            v_p.append(v_new)
            k_past = cache_k[j][page_table].reshape(DEC_BATCH, n_pages * PAGE_SIZE, N_HEADS, HEAD_DIM)
            v_past = cache_v[j][page_table].reshape(DEC_BATCH, n_pages * PAGE_SIZE, N_HEADS, HEAD_DIM)
            ms, k_new, v_new = sb_attention_block(xs, k_past, v_past, PAST_LEN, att_w_qkv[j], att_w_o[j], att_sb_bias[j])
            k_s.append(k_new)
            v_s.append(v_new)
        hp = hp + rmsnorm(mp, norm_mix_post[i])
        hs = hs + rmsnorm(ms, norm_mix_post[i])
        zp = rmsnorm(hp, norm_ffn_pre[i])
        zs = rmsnorm(hs, norm_ffn_pre[i])
        if i % 2 == 0:
            fp = swiglu(zp, ffn_w_gu[j], ffn_w_down[j])
            fs = swiglu(zs, ffn_w_gu[j], ffn_w_down[j])
        else:
            fp = moe_swiglu(zp, moe_w_router[j], moe_b_router[j], moe_w_gu[j], moe_w_down[j])
            fs = moe_swiglu(zs, moe_w_router[j], moe_b_router[j], moe_w_gu[j], moe_w_down[j])
        hp = hp + rmsnorm(fp, norm_ffn_post[i])
        hs = hs + rmsnorm(fs, norm_ffn_post[i])
        hp = per_layer_embed(hp, p_prompt[i], norm_ple[i], ple_w_proj[i], ple_w_gate[i])
        hs = per_layer_embed(hs, p_sample[i], norm_ple[i], ple_w_proj[i], ple_w_gate[i])
    return (hp, hs, jnp.stack(conv_p), jnp.stack(lru_p), jnp.stack(k_p), jnp.stack(v_p),
            jnp.stack(conv_s), jnp.stack(lru_s), jnp.stack(k_s), jnp.stack(v_s))
```

```python
import functools
import math

import jax
import jax.numpy as jnp
from jax import lax
from jax.experimental import pallas as pl
from jax.experimental.pallas import tpu as pltpu

F32 = jnp.float32
BF16 = jnp.bfloat16

RMS_EPS = 1e-6
LRU_C = 8.0
N_HEADS = 16
N_LRU_BLOCKS = 8
CONV_W = 4
N_EXPERTS = 8
LANES = 128
NEG_BIG = -1e30
VMEM_LIMIT = 56 * 1024 * 1024


def _cparams(*sem):
    return pltpu.CompilerParams(dimension_semantics=sem, vmem_limit_bytes=VMEM_LIMIT)


def _resident(shape):
    return pl.BlockSpec(shape, lambda *_: (0,) * len(shape), pipeline_mode=pl.Buffered(1))


def _dot(a, b):
    return jnp.dot(a, b, preferred_element_type=F32)


def _dot_nt(a, b):
    return lax.dot_general(a, b, (((1,), (1,)), ((), ())), preferred_element_type=F32)


def _rms(x, g):
    return x * lax.rsqrt(jnp.mean(x * x, axis=-1, keepdims=True) + RMS_EPS) * g


def _softplus(x):
    return jnp.maximum(x, 0.0) + jnp.log1p(jnp.exp(-jnp.abs(x)))


def _gelu(x):
    c = math.sqrt(2.0 / math.pi)
    return 0.5 * x * (1.0 + jnp.tanh(c * (x + 0.044715 * (x * x * x))))


def _silu(x):
    return x * jax.nn.sigmoid(x)


def _lru_gates(xc, wa_ref, ba, wx_ref, bx, lam):
    blk = xc.shape[1] // N_LRU_BLOCKS
    ra, rx = [], []
    for n in range(N_LRU_BLOCKS):
        xb = xc[:, n * blk:(n + 1) * blk].astype(BF16)
        ra.append(_dot(xb, wa_ref[n]))
        rx.append(_dot(xb, wx_ref[n]))
    r = jax.nn.sigmoid(jnp.concatenate(ra, axis=1) + ba)
    ig = jax.nn.sigmoid(jnp.concatenate(rx, axis=1) + bx)
    log_a = -LRU_C * r * _softplus(-lam)
    a = jnp.exp(log_a)
    gx = jnp.sqrt(-jnp.tanh(log_a) * (a * a + 1.0)) * ig * xc
    return a, gx


def _rec_prompt_kernel(x_ref, gpre_ref, win_ref, cw_ref, cb_ref, wa_ref, ba_ref, wx_ref, bx_ref, lam_ref,
                       wout_ref, gpost_ref, h_ref, conv_ref, hlast_ref, tail_ref, hc_ref):
    tt, d = x_ref.shape

    @pl.when(pl.program_id(1) == 0)
    def _():
        tail_ref[...] = jnp.zeros_like(tail_ref)
        hc_ref[...] = jnp.zeros_like(hc_ref)

    x = x_ref[...]
    xn = _rms(x, gpre_ref[...]).astype(BF16)
    gate = _gelu(_dot(xn, win_ref[:, :d]))
    xr = _dot(xn, win_ref[:, d:])
    xpad = jnp.concatenate([tail_ref[...], xr], axis=0)
    xc = cb_ref[...] + cw_ref[3:4, :] * xr
    for back in range(1, CONV_W):
        xc = xc + cw_ref[CONV_W - 1 - back:CONV_W - back, :] * xpad[8 - back:8 - back + tt]
    tail_ref[...] = xr[tt - 8:]
    a, b = _lru_gates(xc, wa_ref, ba_ref[...], wx_ref, bx_ref[...], lam_ref[...])
    row = lax.broadcasted_iota(jnp.int32, (tt, 1), 0)
    s = 1
    while s < tt:
        keep = row >= s
        a_sh = jnp.where(keep, pltpu.roll(a, s, 0), 1.0)
        b_sh = jnp.where(keep, pltpu.roll(b, s, 0), 0.0)
        b = a * b_sh + b
        a = a * a_sh
        s *= 2
    h = b + a * hc_ref[...]
    hc_ref[...] = h[tt - 1:]
    y = _dot((h * gate).astype(BF16), wout_ref[...])
    h_ref[...] = x + _rms(y, gpost_ref[...])
    conv_ref[...] = xr[tt - (CONV_W - 1):]
    hlast_ref[...] = h[tt - 1:]


def _rec_prompt(x, gpre, win, cw, cb, wa, ba, wx, bx, lam, wout, gpost, tt):
    bsz, t, d = x.shape
    vec = _resident((1, d))
    return pl.pallas_call(
        _rec_prompt_kernel,
        grid=(bsz, t // tt),
        in_specs=[pl.BlockSpec((None, tt, d), lambda b, i: (b, i, 0)), vec, _resident((d, 2 * d)),
                  _resident((CONV_W, d)), vec, _resident(wa.shape), vec, _resident(wx.shape), vec, vec,
                  _resident((d, d)), vec],
        out_specs=[pl.BlockSpec((None, tt, d), lambda b, i: (b, i, 0)),
                   pl.BlockSpec((None, CONV_W - 1, d), lambda b, i: (b, 0, 0)),
                   pl.BlockSpec((None, 1, d), lambda b, i: (b, 0, 0))],
        out_shape=[jax.ShapeDtypeStruct((bsz, t, d), F32), jax.ShapeDtypeStruct((bsz, CONV_W - 1, d), F32),
                   jax.ShapeDtypeStruct((bsz, 1, d), F32)],
        scratch_shapes=[pltpu.VMEM((8, d), F32), pltpu.VMEM((1, d), F32)],
        compiler_params=_cparams("arbitrary", "arbitrary"),
        name="rec_prompt",
    )(x, gpre, win, cw, cb, wa, ba, wx, bx, lam, wout, gpost)


def _rec_step_kernel(x_ref, sc_ref, h0_ref, gpre_ref, win_ref, cw_ref, cb_ref, wa_ref, ba_ref, wx_ref, bx_ref,
                     lam_ref, wout_ref, gpost_ref, h_ref, conv_ref, hnew_ref):
    d = x_ref.shape[1]
    x = x_ref[...]
    xn = _rms(x, gpre_ref[...]).astype(BF16)
    gate = _gelu(_dot(xn, win_ref[:, :d]))
    xr = _dot(xn, win_ref[:, d:])
    xc = cb_ref[...] + cw_ref[CONV_W - 1:CONV_W, :] * xr
    for k in range(CONV_W - 1):
        xc = xc + cw_ref[k:k + 1, :] * sc_ref[k]
    a, b = _lru_gates(xc, wa_ref, ba_ref[...], wx_ref, bx_ref[...], lam_ref[...])
    h = a * h0_ref[...] + b
    y = _dot((h * gate).astype(BF16), wout_ref[...])
    h_ref[...] = x + _rms(y, gpost_ref[...])
    for k in range(CONV_W - 2):
        conv_ref[k] = sc_ref[k + 1]
    conv_ref[CONV_W - 2] = xr
    hnew_ref[...] = h


def _rec_step(x, sc, h0, gpre, win, cw, cb, wa, ba, wx, bx, lam, wout, gpost):
    rows, d = x.shape
    return pl.pallas_call(
        _rec_step_kernel,
        out_shape=[jax.ShapeDtypeStruct((rows, d), F32), jax.ShapeDtypeStruct((CONV_W - 1, rows, d), F32),
                   jax.ShapeDtypeStruct((rows, d), F32)],
        compiler_params=pltpu.CompilerParams(vmem_limit_bytes=VMEM_LIMIT),
        name="rec_step",
    )(x, sc, h0, gpre, win, cw, cb, wa, ba, wx, bx, lam, wout, gpost)


def _ffn_kernel(h_ref, g_ref, wgu_ref, wd_ref, f_ref, *, ck):
    ff = wd_ref.shape[0]
    z = _rms(h_ref[...], g_ref[...]).astype(BF16)
    acc = jnp.zeros(f_ref.shape, F32)
    for c in range(ff // ck):
        g = _dot(z, wgu_ref[:, c * ck:(c + 1) * ck])
        u = _dot(z, wgu_ref[:, ff + c * ck:ff + (c + 1) * ck])
        acc = acc + _dot((_silu(g) * u).astype(BF16), wd_ref[c * ck:(c + 1) * ck, :])
    f_ref[...] = acc


def _ffn(h, g, wgu, wd, tm, ck=512):
    m, d = h.shape
    row = pl.BlockSpec((tm, d), lambda i: (i, 0))
    return pl.pallas_call(
        functools.partial(_ffn_kernel, ck=ck),
        grid=(m // tm,),
        in_specs=[row, _resident((1, d)), _resident(wgu.shape), _resident(wd.shape)],
        out_specs=row,
        out_shape=jax.ShapeDtypeStruct((m, d), F32),
        compiler_params=_cparams("arbitrary"),
        name="ffn_dense",
    )(h, g, wgu, wd)


def _finish_kernel(h_ref, f_ref, p_ref, gpost_ref, gple_ref, wproj_ref, wgate_ref, o_ref):
    h2 = h_ref[...] + _rms(f_ref[...], gpost_ref[...])
    e = _dot(p_ref[...].astype(BF16), wproj_ref[...])
    g = jax.nn.sigmoid(_dot(_rms(h2, gple_ref[...]).astype(BF16), wgate_ref[...]))
    o_ref[...] = h2 + g * e


def _finish(h, f, p, gpost, gple, wproj, wgate, tm):
    m, d = h.shape
    row = pl.BlockSpec((tm, d), lambda i: (i, 0))
    return pl.pallas_call(
        _finish_kernel,
        grid=(m // tm,),
        in_specs=[row, row, pl.BlockSpec((tm, p.shape[1]), lambda i: (i, 0)), _resident((1, d)), _resident((1, d)),
                  _resident(wproj.shape), _resident(wgate.shape)],
        out_specs=row,
        out_shape=jax.ShapeDtypeStruct((m, d), F32),
        compiler_params=_cparams("arbitrary"),
        name="finish",
    )(h, f, p, gpost, gple, wproj, wgate)


def _qkv_kernel(h_ref, g_ref, w_ref, k_ref, v_ref, qb_ref, kb_ref, vb_ref):
    d = h_ref.shape[1]
    xn = _rms(h_ref[...], g_ref[...]).astype(BF16)
    scale = (d // N_HEADS) ** -0.5
    qb_ref[...] = (_dot(xn, w_ref[:, :d]) * scale).astype(BF16)
    k = _dot(xn, w_ref[:, d:2 * d])
    v = _dot(xn, w_ref[:, 2 * d:])
    k_ref[...] = k
    v_ref[...] = v
    kb_ref[...] = k.astype(BF16)
    vb_ref[...] = v.astype(BF16)


def _qkv(h, g, w, tm):
    m, d = h.shape
    row = pl.BlockSpec((tm, d), lambda i: (i, 0))
    return pl.pallas_call(
        _qkv_kernel,
        grid=(m // tm,),
        in_specs=[row, _resident((1, d)), _resident(w.shape)],
        out_specs=[row] * 5,
        out_shape=[jax.ShapeDtypeStruct((m, d), F32)] * 2 + [jax.ShapeDtypeStruct((m, d), BF16)] * 3,
        compiler_params=_cparams("arbitrary"),
        name="qkv",
    )(h, g, w)


def _sb_weights(z, sp, ls, tri, carry):
    hi = ls.astype(BF16)
    lo = (ls - hi.astype(F32)).astype(BF16)
    later = _dot(hi, tri) + _dot(lo, tri)
    return jnp.exp((z - sp) + later + carry)


def _sb_prompt_kernel(bias_ref, q_ref, k_ref, v_ref, o_ref, acc_ref, carry_ref):
    tq, width = q_ref.shape
    hd = width // 2
    hp = pl.program_id(1)
    i = pl.program_id(2)
    lane = lax.broadcasted_iota(jnp.int32, (tq, width), 1)
    row = lax.broadcasted_iota(jnp.int32, (tq, tq), 0)
    col = lax.broadcasted_iota(jnp.int32, (tq, tq), 1)
    tri = jnp.where(row > col, 1.0, 0.0).astype(BF16)
    causal = col < row
    q = q_ref[...]
    zero = jnp.zeros_like(q)
    qs = [jnp.where(lane < hd, q, zero), jnp.where(lane >= hd, q, zero)]
    biases = [bias_ref[2 * hp], bias_ref[2 * hp + 1]]
    acc_ref[...] = jnp.zeros_like(acc_ref)
    carry_ref[...] = jnp.zeros_like(carry_ref)

    def tile(j, diag):
        start = pl.multiple_of(j * tq, tq)
        kj = k_ref[pl.ds(start, tq), :]
        vj = v_ref[pl.ds(start, tq), :]
        for hh in range(2):
            z = _dot_nt(qs[hh], kj) + biases[hh]
            sp = _softplus(z)
            ls = jnp.where(causal, -sp, 0.0) if diag else -sp
            w = _sb_weights(z, sp, ls, tri, carry_ref[hh])
            if diag:
                w = jnp.where(causal, w, 0.0)
            acc_ref[hh] += _dot(w.astype(BF16), vj)
            carry_ref[hh] += jnp.sum(ls, axis=-1, keepdims=True)

    tile(i, True)

    def body(jj, c):
        tile(i - 1 - jj, False)
        return c

    lax.fori_loop(0, i, body, 0)
    o_ref[...] = jnp.where(lane < hd, acc_ref[0], acc_ref[1]).astype(o_ref.dtype)


def _sb_prompt(qb, kb, vb, bias, tq):
    bsz, t, d = qb.shape
    width = 2 * (d // N_HEADS)
    qspec = pl.BlockSpec((None, tq, width), lambda b, hp, i: (b, i, hp))
    kvspec = pl.BlockSpec((None, t, width), lambda b, hp, i: (b, 0, hp))
    return pl.pallas_call(
        _sb_prompt_kernel,
        grid=(bsz, d // width, t // tq),
        in_specs=[pl.BlockSpec(memory_space=pltpu.SMEM), qspec, kvspec, kvspec],
        out_specs=qspec,
        out_shape=jax.ShapeDtypeStruct((bsz, t, d), BF16),
        scratch_shapes=[pltpu.VMEM((2, tq, width), F32), pltpu.VMEM((2, tq, 1), F32)],
        compiler_params=_cparams("arbitrary", "arbitrary", "arbitrary"),
        name="sb_prompt",
    )(bias, qb, kb, vb)


def _sb_decode_kernel(pt_ref, q_ref, bias_ref, *refs, pg):
    del pt_ref
    k_refs, v_refs = refs[:pg], refs[pg:2 * pg]
    o_ref, acc_ref, carry_ref = refs[2 * pg:]
    nh, _, page = k_refs[0].shape
    g = pl.program_id(1)

    @pl.when(g == 0)
    def _():
        acc_ref[...] = jnp.zeros_like(acc_ref)
        carry_ref[...] = jnp.zeros_like(carry_ref)

    row = lax.broadcasted_iota(jnp.int32, (page, page), 0)
    col = lax.broadcasted_iota(jnp.int32, (page, page), 1)
    tri = jnp.where(row > col, 1.0, 0.0).astype(BF16)
    bias = bias_ref[...]
    for r in range(pg):
        z = jnp.concatenate([jnp.sum(q_ref[h] * k_refs[r][h], axis=0, keepdims=True) for h in range(nh)], axis=0)
        z = z + bias
        sp = _softplus(z)
        ls = -sp
        w = _sb_weights(z, sp, ls, tri, carry_ref[...])
        for h in range(nh):
            acc_ref[h] += w[h:h + 1, :] * v_refs[r][h]
        carry_ref[...] += jnp.sum(ls, axis=-1, keepdims=True)

    @pl.when(g == pl.num_programs(1) - 1)
    def _():
        o_ref[...] = jnp.sum(acc_ref[...], axis=-1)


def _sb_decode(q, cache_k, cache_v, page_table, bias, pg):
    s, nh, hd, _ = q.shape
    n_pages = page_table.shape[1]
    page = cache_k.shape[3]

    def page_spec(r):
        return pl.BlockSpec((None, nh, hd, page), lambda b, g, pt: (pt[b, n_pages - 1 - (g * pg + r)], 0, 0, 0))

    grid_spec = pltpu.PrefetchScalarGridSpec(
        num_scalar_prefetch=1,
        grid=(s, n_pages // pg),
        in_specs=[pl.BlockSpec((None, nh, hd, 1), lambda b, g, pt: (b, 0, 0, 0)),
                  pl.BlockSpec((nh, 1), lambda b, g, pt: (0, 0))] + [page_spec(r) for r in range(pg)] * 2,
        out_specs=pl.BlockSpec((None, nh, hd), lambda b, g, pt: (b, 0, 0)),
        scratch_shapes=[pltpu.VMEM((nh, hd, page), F32), pltpu.VMEM((nh, 1), F32)],
    )
    return pl.pallas_call(
        functools.partial(_sb_decode_kernel, pg=pg),
        grid_spec=grid_spec,
        out_shape=jax.ShapeDtypeStruct((s, nh, hd), F32),
        compiler_params=_cparams("arbitrary", "arbitrary"),
        name="sb_decode",
    )(page_table, q, bias, *([cache_k] * pg), *([cache_v] * pg))


def _attn_out_kernel(h_ref, o_ref, wo_ref, gpost_ref, gffn_ref, wr_ref, br_ref, h1_ref, z_ref, comb_ref):
    y = _dot(o_ref[...].astype(BF16), wo_ref[...])
    h1 = h_ref[...] + _rms(y, gpost_ref[...])
    h1_ref[...] = h1
    z = _rms(h1, gffn_ref[...]).astype(BF16)
    z_ref[...] = z
    logits = _dot(z, wr_ref[...]) + br_ref[...]
    idx = lax.broadcasted_iota(jnp.int32, logits.shape, 1)
    m1 = jnp.max(logits, axis=-1, keepdims=True)
    i1 = jnp.min(jnp.where(logits == m1, idx, LANES), axis=-1, keepdims=True)
    rest = jnp.where(idx == i1, NEG_BIG, logits)
    m2 = jnp.max(rest, axis=-1, keepdims=True)
    i2 = jnp.min(jnp.where(rest == m2, idx, LANES), axis=-1, keepdims=True)
    e = jnp.exp(m2 - m1)
    w1 = 1.0 / (1.0 + e)
    comb_ref[...] = jnp.where(idx == i1, w1, 0.0) + jnp.where(idx == i2, e * w1, 0.0)


def _attn_out(h, o, wo, gpost, gffn, wr, br, tm):
    m, d = h.shape
    row = pl.BlockSpec((tm, d), lambda i: (i, 0))
    comb = pl.BlockSpec((tm, LANES), lambda i: (i, 0))
    return pl.pallas_call(
        _attn_out_kernel,
        grid=(m // tm,),
        in_specs=[row, row, _resident(wo.shape), _resident((1, d)), _resident((1, d)), _resident(wr.shape),
                  _resident(br.shape)],
        out_specs=[row, row, comb],
        out_shape=[jax.ShapeDtypeStruct((m, d), F32), jax.ShapeDtypeStruct((m, d), BF16),
                   jax.ShapeDtypeStruct((m, LANES), F32)],
        compiler_params=_cparams("arbitrary"),
        name="attn_out",
    )(h, o, wo, gpost, gffn, wr, br)


def _moe_kernel(z_ref, comb_ref, wg_ref, wu_ref, wd_ref, f_ref):
    e = pl.program_id(1)
    c = pl.program_id(2)

    @pl.when((e == 0) & (c == 0))
    def _():
        f_ref[...] = jnp.zeros_like(f_ref)

    z = z_ref[...]
    act = (_silu(_dot(z, wg_ref[...])) * _dot(z, wu_ref[...])).astype(BF16)
    lane = lax.broadcasted_iota(jnp.int32, comb_ref.shape, 1)
    ce = jnp.sum(jnp.where(lane == e, comb_ref[...], 0.0), axis=-1, keepdims=True)
    f_ref[...] += ce * _dot(act, wd_ref[...])


def _moe(z, comb, wgu, wd, tm, ck=512):
    m, d = z.shape
    n_e, ff, _ = wd.shape
    fc = ff // ck
    return pl.pallas_call(
        _moe_kernel,
        grid=(m // tm, n_e, fc),
        in_specs=[pl.BlockSpec((tm, d), lambda i, e, c: (i, 0)), pl.BlockSpec((tm, LANES), lambda i, e, c: (i, 0)),
                  pl.BlockSpec((None, d, ck), lambda i, e, c: (e, 0, c)),
                  pl.BlockSpec((None, d, ck), lambda i, e, c: (e, 0, fc + c)),
                  pl.BlockSpec((None, ck, d), lambda i, e, c: (e, c, 0))],
        out_specs=pl.BlockSpec((tm, d), lambda i, e, c: (i, 0)),
        out_shape=jax.ShapeDtypeStruct((m, d), F32),
        compiler_params=_cparams("arbitrary", "arbitrary", "arbitrary"),
        name="moe",
    )(z, comb, wgu, wgu, wd)


def _tile(m, pref):
    return pref if m % pref == 0 else m


def kernel(x_prompt, x_sample, state_conv, state_lru, cache_k, cache_v, page_table, p_prompt, p_sample, norm_mix_pre, norm_mix_post, norm_ffn_pre, norm_ffn_post, norm_ple, rec_w_in, rec_conv_w, rec_conv_b, rec_w_a, rec_b_a, rec_w_x, rec_b_x, rec_lambda, rec_w_out, att_w_qkv, att_w_o, att_sb_bias, ffn_w_gu, ffn_w_down, moe_w_router, moe_b_router, moe_w_gu, moe_w_down, ple_w_proj, ple_w_gate):
    bsz, t, d = x_prompt.shape
    s = x_sample.shape[0]
    depth = norm_mix_pre.shape[0]
    mp = bsz * t
    tmp = _tile(mp, 512)
    vec = lambda a: a.reshape(1, -1)
    bf = lambda a: a.astype(BF16)

    hp = x_prompt
    hs = x_sample.reshape(s, d)
    outs = {k: [] for k in ("conv_p", "lru_p", "k_p", "v_p", "conv_s", "lru_s", "k_s", "v_s")}
    for i in range(depth):
        j = i // 2
        if i % 2 == 0:
            rec = (vec(norm_mix_pre[i]), bf(rec_w_in[j]), rec_conv_w[j], vec(rec_conv_b[j]), bf(rec_w_a[j]),
                   vec(rec_b_a[j]), bf(rec_w_x[j]), vec(rec_b_x[j]), vec(rec_lambda[j]), bf(rec_w_out[j]),
                   vec(norm_mix_post[i]))
            hp, conv_new, h_new = _rec_prompt(hp.reshape(bsz, t, d), *rec, tt=_tile(t, 256))
            outs["conv_p"].append(conv_new)
            outs["lru_p"].append(h_new.reshape(bsz, d))
            hs, conv_new, h_new = _rec_step(hs, jnp.swapaxes(state_conv[j], 0, 1), state_lru[j], *rec)
            outs["conv_s"].append(jnp.swapaxes(conv_new, 0, 1))
            outs["lru_s"].append(h_new)
            hp = hp.reshape(mp, d)
            fp = _ffn(hp, vec(norm_ffn_pre[i]), bf(ffn_w_gu[j]), bf(ffn_w_down[j]), tmp)
            fs = _ffn(hs, vec(norm_ffn_pre[i]), bf(ffn_w_gu[j]), bf(ffn_w_down[j]), s)
        else:
            hp = hp.reshape(mp, d)
            wqkv, wo = bf(att_w_qkv[j]), bf(att_w_o[j])
            kp, vp, qb, kb, vb = _qkv(hp, vec(norm_mix_pre[i]), wqkv, tmp)
            outs["k_p"].append(kp.reshape(bsz, t, N_HEADS, d // N_HEADS))
            outs["v_p"].append(vp.reshape(bsz, t, N_HEADS, d // N_HEADS))
            op = _sb_prompt(qb.reshape(bsz, t, d), kb.reshape(bsz, t, d), vb.reshape(bsz, t, d), att_sb_bias[j],
                            tq=_tile(t, 256))
            ks, vs, qs, _, _ = _qkv(hs, vec(norm_mix_pre[i]), wqkv, s)
            outs["k_s"].append(ks.reshape(s, 1, N_HEADS, d // N_HEADS))
            outs["v_s"].append(vs.reshape(s, 1, N_HEADS, d // N_HEADS))
            n_pages = page_table.shape[1]
            hd = d // N_HEADS
            os_ = _sb_decode(qs.astype(F32).reshape(s, N_HEADS, hd, 1), jnp.transpose(cache_k[j], (0, 2, 3, 1)),
                             jnp.transpose(cache_v[j], (0, 2, 3, 1)), page_table, att_sb_bias[j].reshape(N_HEADS, 1),
                             pg=4 if n_pages % 4 == 0 else 1)
            wr = jnp.pad(bf(moe_w_router[j]), ((0, 0), (0, LANES - N_EXPERTS)))
            br = jnp.pad(moe_b_router[j], (0, LANES - N_EXPERTS), constant_values=NEG_BIG).reshape(1, LANES)
            post = (wo, vec(norm_mix_post[i]), vec(norm_ffn_pre[i]), wr, br)
            hp, zp, cp = _attn_out(hp, op.reshape(mp, d), *post, tmp)
            hs, zs, cs = _attn_out(hs, os_.reshape(s, d), *post, s)
            wgu, wd = bf(moe_w_gu[j]), bf(moe_w_down[j])
            fp = _moe(zp, cp, wgu, wd, _tile(mp, 1024))
            fs = _moe(zs, cs, wgu, wd, s)
        tail = (vec(norm_ffn_post[i]), vec(norm_ple[i]), bf(ple_w_proj[i]), bf(ple_w_gate[i]))
        hp = _finish(hp, fp, p_prompt[i].reshape(mp, -1), *tail, tmp)
        hs = _finish(hs, fs, p_sample[i].reshape(s, -1), *tail, s)
    st = lambda k: jnp.stack(outs[k])
    return (hp.reshape(bsz, t, d), hs.reshape(s, 1, d), st("conv_p"), st("lru_p"), st("k_p"), st("v_p"),
            st("conv_s"), st("lru_s"), st("k_s"), st("v_s"))
```

```python
import functools
import math

import jax
import jax.numpy as jnp
from jax import lax
from jax.experimental import pallas as pl
from jax.experimental.pallas import tpu as pltpu

F32 = jnp.float32
BF16 = jnp.bfloat16

RMS_EPS = 1e-6
LRU_C = 8.0
N_HEADS = 16
N_LRU_BLOCKS = 8
CONV_W = 4
N_EXPERTS = 8
LANES = 128
NEG_BIG = -1e30
VMEM_LIMIT = 56 * 1024 * 1024


def _cparams(*sem):
    return pltpu.CompilerParams(dimension_semantics=sem, vmem_limit_bytes=VMEM_LIMIT)


def _resident(shape):
    return pl.BlockSpec(shape, lambda *_: (0,) * len(shape), pipeline_mode=pl.Buffered(1))


def _dot(a, b):
    return jnp.dot(a, b, preferred_element_type=F32)


def _mm(x, w):
    if w.dtype == F32:
        return jnp.dot(x.astype(F32), w, precision=lax.Precision.HIGHEST, preferred_element_type=F32)
    return jnp.dot(x.astype(w.dtype), w, preferred_element_type=F32)


def _dot_nt(a, b):
    return lax.dot_general(a, b, (((1,), (1,)), ((), ())), preferred_element_type=F32)


def _rms(x, g):
    return x * lax.rsqrt(jnp.mean(x * x, axis=-1, keepdims=True) + RMS_EPS) * g


def _softplus(x):
    return jnp.maximum(x, 0.0) + jnp.log1p(jnp.exp(-jnp.abs(x)))


def _gelu(x):
    c = math.sqrt(2.0 / math.pi)
    return 0.5 * x * (1.0 + jnp.tanh(c * (x + 0.044715 * (x * x * x))))


def _silu(x):
    return x * jax.nn.sigmoid(x)


def _lru_gates(xc, wa_ref, ba, wx_ref, bx, lam):
    blk = xc.shape[1] // N_LRU_BLOCKS
    ra, rx = [], []
    for n in range(N_LRU_BLOCKS):
        xb = xc[:, n * blk:(n + 1) * blk]
        ra.append(_mm(xb, wa_ref[n]))
        rx.append(_mm(xb, wx_ref[n]))
    r = jax.nn.sigmoid(jnp.concatenate(ra, axis=1) + ba)
    ig = jax.nn.sigmoid(jnp.concatenate(rx, axis=1) + bx)
    log_a = -LRU_C * r * _softplus(-lam)
    a = jnp.exp(log_a)
    gx = jnp.sqrt(-jnp.tanh(log_a) * (a * a + 1.0)) * ig * xc
    return a, gx


def _rec_prompt_kernel(x_ref, gpre_ref, win_ref, cw_ref, cb_ref, wa_ref, ba_ref, wx_ref, bx_ref, lam_ref,
                       wout_ref, gpost_ref, h_ref, conv_ref, hlast_ref, tail_ref, hc_ref):
    tt, d = x_ref.shape

    @pl.when(pl.program_id(1) == 0)
    def _():
        tail_ref[...] = jnp.zeros_like(tail_ref)
        hc_ref[...] = jnp.zeros_like(hc_ref)

    x = x_ref[...]
    xn = _rms(x, gpre_ref[...]).astype(win_ref.dtype)
    gate = _gelu(_mm(xn, win_ref[:, :d]))
    xr = _mm(xn, win_ref[:, d:])
    xpad = jnp.concatenate([tail_ref[...], xr], axis=0)
    xc = cb_ref[...] + cw_ref[3:4, :] * xr
    for back in range(1, CONV_W):
        xc = xc + cw_ref[CONV_W - 1 - back:CONV_W - back, :] * xpad[8 - back:8 - back + tt]
    tail_ref[...] = xr[tt - 8:]
    a, b = _lru_gates(xc, wa_ref, ba_ref[...], wx_ref, bx_ref[...], lam_ref[...])
    row = lax.broadcasted_iota(jnp.int32, (tt, 1), 0)
    s = 1
    while s < tt:
        keep = row >= s
        a_sh = jnp.where(keep, pltpu.roll(a, s, 0), 1.0)
        b_sh = jnp.where(keep, pltpu.roll(b, s, 0), 0.0)
        b = a * b_sh + b
        a = a * a_sh
        s *= 2
    h = b + a * hc_ref[...]
    hc_ref[...] = h[tt - 1:]
    y = _mm(h * gate, wout_ref[...])
    h_ref[...] = x + _rms(y, gpost_ref[...])
    conv_ref[...] = xr[tt - (CONV_W - 1):]
    hlast_ref[...] = h[tt - 1:]


def _rec_prompt(x, gpre, win, cw, cb, wa, ba, wx, bx, lam, wout, gpost, tt):
    bsz, t, d = x.shape
    vec = _resident((1, d))
    return pl.pallas_call(
        _rec_prompt_kernel,
        grid=(bsz, t // tt),
        in_specs=[pl.BlockSpec((None, tt, d), lambda b, i: (b, i, 0)), vec, _resident((d, 2 * d)),
                  _resident((CONV_W, d)), vec, _resident(wa.shape), vec, _resident(wx.shape), vec, vec,
                  _resident((d, d)), vec],
        out_specs=[pl.BlockSpec((None, tt, d), lambda b, i: (b, i, 0)),
                   pl.BlockSpec((None, CONV_W - 1, d), lambda b, i: (b, 0, 0)),
                   pl.BlockSpec((None, 1, d), lambda b, i: (b, 0, 0))],
        out_shape=[jax.ShapeDtypeStruct((bsz, t, d), F32), jax.ShapeDtypeStruct((bsz, CONV_W - 1, d), F32),
                   jax.ShapeDtypeStruct((bsz, 1, d), F32)],
        scratch_shapes=[pltpu.VMEM((8, d), F32), pltpu.VMEM((1, d), F32)],
        compiler_params=_cparams("arbitrary", "arbitrary"),
        name="rec_prompt",
    )(x, gpre, win, cw, cb, wa, ba, wx, bx, lam, wout, gpost)


def _rec_step_kernel(x_ref, sc_ref, h0_ref, gpre_ref, win_ref, cw_ref, cb_ref, wa_ref, ba_ref, wx_ref, bx_ref,
                     lam_ref, wout_ref, gpost_ref, h_ref, conv_ref, hnew_ref):
    d = x_ref.shape[1]
    x = x_ref[...]
    xn = _rms(x, gpre_ref[...]).astype(win_ref.dtype)
    gate = _gelu(_mm(xn, win_ref[:, :d]))
    xr = _mm(xn, win_ref[:, d:])
    xc = cb_ref[...] + cw_ref[CONV_W - 1:CONV_W, :] * xr
    for k in range(CONV_W - 1):
        xc = xc + cw_ref[k:k + 1, :] * sc_ref[k]
    a, b = _lru_gates(xc, wa_ref, ba_ref[...], wx_ref, bx_ref[...], lam_ref[...])
    h = a * h0_ref[...] + b
    y = _mm(h * gate, wout_ref[...])
    h_ref[...] = x + _rms(y, gpost_ref[...])
    for k in range(CONV_W - 2):
        conv_ref[k] = sc_ref[k + 1]
    conv_ref[CONV_W - 2] = xr
    hnew_ref[...] = h


def _rec_step(x, sc, h0, gpre, win, cw, cb, wa, ba, wx, bx, lam, wout, gpost):
    rows, d = x.shape
    return pl.pallas_call(
        _rec_step_kernel,
        out_shape=[jax.ShapeDtypeStruct((rows, d), F32), jax.ShapeDtypeStruct((CONV_W - 1, rows, d), F32),
                   jax.ShapeDtypeStruct((rows, d), F32)],
        compiler_params=pltpu.CompilerParams(vmem_limit_bytes=VMEM_LIMIT),
        name="rec_step",
    )(x, sc, h0, gpre, win, cw, cb, wa, ba, wx, bx, lam, wout, gpost)


def _ffn_kernel(h_ref, g_ref, wgu_ref, wd_ref, f_ref, *, ck):
    ff = wd_ref.shape[0]
    z = _rms(h_ref[...], g_ref[...]).astype(wgu_ref.dtype)
    acc = jnp.zeros(f_ref.shape, F32)
    for c in range(ff // ck):
        g = _mm(z, wgu_ref[:, c * ck:(c + 1) * ck])
        u = _mm(z, wgu_ref[:, ff + c * ck:ff + (c + 1) * ck])
        acc = acc + _mm(_silu(g) * u, wd_ref[c * ck:(c + 1) * ck, :])
    f_ref[...] = acc


def _ffn(h, g, wgu, wd, tm, ck=512):
    m, d = h.shape
    row = pl.BlockSpec((tm, d), lambda i: (i, 0))
    return pl.pallas_call(
        functools.partial(_ffn_kernel, ck=ck),
        grid=(m // tm,),
        in_specs=[row, _resident((1, d)), _resident(wgu.shape), _resident(wd.shape)],
        out_specs=row,
        out_shape=jax.ShapeDtypeStruct((m, d), F32),
        compiler_params=_cparams("arbitrary"),
        name="ffn_dense",
    )(h, g, wgu, wd)


def _finish_kernel(h_ref, f_ref, p_ref, gpost_ref, gple_ref, wproj_ref, wgate_ref, o_ref):
    h2 = h_ref[...] + _rms(f_ref[...], gpost_ref[...])
    e = _mm(p_ref[...], wproj_ref[...])
    g = jax.nn.sigmoid(_mm(_rms(h2, gple_ref[...]), wgate_ref[...]))
    o_ref[...] = h2 + g * e


def _finish(h, f, p, gpost, gple, wproj, wgate, tm):
    m, d = h.shape
    row = pl.BlockSpec((tm, d), lambda i: (i, 0))
    return pl.pallas_call(
        _finish_kernel,
        grid=(m // tm,),
        in_specs=[row, row, pl.BlockSpec((tm, p.shape[1]), lambda i: (i, 0)), _resident((1, d)), _resident((1, d)),
                  _resident(wproj.shape), _resident(wgate.shape)],
        out_specs=row,
        out_shape=jax.ShapeDtypeStruct((m, d), F32),
        compiler_params=_cparams("arbitrary"),
        name="finish",
    )(h, f, p, gpost, gple, wproj, wgate)


def _qkv_kernel(h_ref, g_ref, w_ref, k_ref, v_ref, qb_ref, kb_ref, vb_ref):
    d = h_ref.shape[1]
    xn = _rms(h_ref[...], g_ref[...]).astype(w_ref.dtype)
    scale = (d // N_HEADS) ** -0.5
    qb_ref[...] = (_mm(xn, w_ref[:, :d]) * scale).astype(qb_ref.dtype)
    k = _mm(xn, w_ref[:, d:2 * d])
    v = _mm(xn, w_ref[:, 2 * d:])
    k_ref[...] = k
    v_ref[...] = v
    kb_ref[...] = k.astype(BF16)
    vb_ref[...] = v.astype(BF16)


def _qkv(h, g, w, tm, q_dtype=BF16):
    m, d = h.shape
    row = pl.BlockSpec((tm, d), lambda i: (i, 0))
    return pl.pallas_call(
        _qkv_kernel,
        grid=(m // tm,),
        in_specs=[row, _resident((1, d)), _resident(w.shape)],
        out_specs=[row] * 5,
        out_shape=[jax.ShapeDtypeStruct((m, d), F32)] * 2 + [jax.ShapeDtypeStruct((m, d), q_dtype)]
        + [jax.ShapeDtypeStruct((m, d), BF16)] * 2,
        compiler_params=_cparams("arbitrary"),
        name="qkv",
    )(h, g, w)


def _sb_weights(z, sp, ls, tri, carry):
    hi = ls.astype(BF16)
    lo = (ls - hi.astype(F32)).astype(BF16)
    later = _dot(hi, tri) + _dot(lo, tri)
    return jnp.exp((z - sp) + later + carry)


def _sb_prompt_kernel(bias_ref, q_ref, k_ref, v_ref, o_ref, qs_ref, w_ref, acc_ref, carry_ref, *, hps):
    tq, width = q_ref.shape
    hd = width // hps
    grp = pl.program_id(1)
    i = pl.program_id(2)
    lane = lax.broadcasted_iota(jnp.int32, (tq, width), 1)
    q = q_ref[...]
    for h in range(hps):
        qs_ref[h * tq:(h + 1) * tq, :] = jnp.where((lane >= h * hd) & (lane < (h + 1) * hd), q, jnp.zeros_like(q))
    acc_ref[...] = jnp.zeros_like(acc_ref)
    carry_ref[...] = jnp.zeros_like(carry_ref)
    r2 = lax.broadcasted_iota(jnp.int32, (tq, tq), 0)
    c2 = lax.broadcasted_iota(jnp.int32, (tq, tq), 1)
    neg_tri = jnp.where(r2 > c2, -1.0, 0.0).astype(BF16)
    causal = jnp.concatenate([c2 < r2] * hps, axis=0)
    bias = jnp.concatenate([jnp.full((tq, 1), bias_ref[hps * grp + h], F32) for h in range(hps)], axis=0)

    def weights(j, diag):
        z = _dot_nt(qs_ref[...], k_ref[pl.ds(pl.multiple_of(j * tq, tq), tq), :]) + bias
        neg_abs = pltpu.bitcast(pltpu.bitcast(z, jnp.uint32) | jnp.uint32(0x80000000), F32)
        sp = jnp.maximum(z, 0.0) + jnp.log(1.0 + jnp.exp(neg_abs))
        if diag:
            sp = jnp.where(causal, sp, 0.0)
        later = _dot(sp.astype(BF16), neg_tri)
        w = jnp.exp((z - sp) + later + carry_ref[...])
        if diag:
            w = jnp.where(causal, w, 0.0)
        w_ref[...] = w.astype(BF16)
        carry_ref[...] -= jnp.sum(sp, axis=-1, keepdims=True)

    def values(j):
        acc_ref[...] += _dot(w_ref[...], v_ref[pl.ds(pl.multiple_of(j * tq, tq), tq), :])

    weights(i, True)

    def body(jj, c):
        j = i - 1 - jj
        values(j + 1)
        weights(j, False)
        return c

    lax.fori_loop(0, i, body, 0)
    values(0)
    out = acc_ref[0:tq, :]
    for h in range(1, hps):
        out = jnp.where(lane >= h * hd, acc_ref[h * tq:(h + 1) * tq, :], out)
    o_ref[...] = out.astype(o_ref.dtype)


def _sb_prompt(qb, kb, vb, bias, tq, hps=4):
    bsz, t, d = qb.shape
    width = hps * (d // N_HEADS)
    qspec = pl.BlockSpec((None, tq, width), lambda b, g, i: (b, i, g))
    kvspec = pl.BlockSpec((None, t, width), lambda b, g, i: (b, 0, g))
    return pl.pallas_call(
        functools.partial(_sb_prompt_kernel, hps=hps),
        grid=(bsz, d // width, t // tq),
        in_specs=[pl.BlockSpec(memory_space=pltpu.SMEM), qspec, kvspec, kvspec],
        out_specs=qspec,
        out_shape=jax.ShapeDtypeStruct((bsz, t, d), BF16),
        scratch_shapes=[pltpu.VMEM((hps * tq, width), BF16), pltpu.VMEM((hps * tq, tq), BF16),
                        pltpu.VMEM((hps * tq, width), F32),
                        pltpu.VMEM((hps * tq, 1), F32)],
        compiler_params=_cparams("arbitrary", "arbitrary", "arbitrary"),
        name="sb_prompt",
    )(bias, qb, kb, vb)


def _sb_decode_kernel(pt_ref, q_ref, bias_ref, *refs, pg):
    del pt_ref
    k_refs, v_refs = refs[:pg], refs[pg:2 * pg]
    o_ref, acc_ref, carry_ref = refs[2 * pg:]
    nh, _, page = k_refs[0].shape
    g = pl.program_id(1)

    @pl.when(g == 0)
    def _():
        acc_ref[...] = jnp.zeros_like(acc_ref)
        carry_ref[...] = jnp.zeros_like(carry_ref)

    row = lax.broadcasted_iota(jnp.int32, (page, page), 0)
    col = lax.broadcasted_iota(jnp.int32, (page, page), 1)
    tri = jnp.where(row > col, 1.0, 0.0).astype(BF16)
    bias = bias_ref[...]
    for r in range(pg):
        z = jnp.concatenate([jnp.sum(q_ref[h] * k_refs[r][h], axis=0, keepdims=True) for h in range(nh)], axis=0)
        z = z + bias
        sp = _softplus(z)
        ls = -sp
        w = _sb_weights(z, sp, ls, tri, carry_ref[...])
        for h in range(nh):
            acc_ref[h] += w[h:h + 1, :] * v_refs[r][h]
        carry_ref[...] += jnp.sum(ls, axis=-1, keepdims=True)

    @pl.when(g == pl.num_programs(1) - 1)
    def _():
        o_ref[...] = jnp.sum(acc_ref[...], axis=-1)


def _sb_decode(q, cache_k, cache_v, page_table, bias, pg):
    s, nh, hd, _ = q.shape
    n_pages = page_table.shape[1]
    page = cache_k.shape[3]

    def page_spec(r):
        return pl.BlockSpec((None, nh, hd, page), lambda b, g, pt: (pt[b, n_pages - 1 - (g * pg + r)], 0, 0, 0))

    grid_spec = pltpu.PrefetchScalarGridSpec(
        num_scalar_prefetch=1,
        grid=(s, n_pages // pg),
        in_specs=[pl.BlockSpec((None, nh, hd, 1), lambda b, g, pt: (b, 0, 0, 0)),
                  pl.BlockSpec((nh, 1), lambda b, g, pt: (0, 0))] + [page_spec(r) for r in range(pg)] * 2,
        out_specs=pl.BlockSpec((None, nh, hd), lambda b, g, pt: (b, 0, 0)),
        scratch_shapes=[pltpu.VMEM((nh, hd, page), F32), pltpu.VMEM((nh, 1), F32)],
    )
    return pl.pallas_call(
        functools.partial(_sb_decode_kernel, pg=pg),
        grid_spec=grid_spec,
        out_shape=jax.ShapeDtypeStruct((s, nh, hd), F32),
        compiler_params=_cparams("arbitrary", "arbitrary"),
        name="sb_decode",
    )(page_table, q, bias, *([cache_k] * pg), *([cache_v] * pg))


def _attn_out_kernel(h_ref, o_ref, wo_ref, gpost_ref, gffn_ref, wr_ref, br_ref, h1_ref, z_ref, comb_ref):
    y = _mm(o_ref[...], wo_ref[...])
    h1 = h_ref[...] + _rms(y, gpost_ref[...])
    h1_ref[...] = h1
    z = _rms(h1, gffn_ref[...])
    z_ref[...] = z.astype(z_ref.dtype)
    logits = _mm(z, wr_ref[...]) + br_ref[...]
    idx = lax.broadcasted_iota(jnp.int32, logits.shape, 1)
    m1 = jnp.max(logits, axis=-1, keepdims=True)
    i1 = jnp.min(jnp.where(logits == m1, idx, LANES), axis=-1, keepdims=True)
    rest = jnp.where(idx == i1, NEG_BIG, logits)
    m2 = jnp.max(rest, axis=-1, keepdims=True)
    i2 = jnp.min(jnp.where(rest == m2, idx, LANES), axis=-1, keepdims=True)
    e = jnp.exp(m2 - m1)
    w1 = 1.0 / (1.0 + e)
    comb_ref[...] = jnp.where(idx == i1, w1, 0.0) + jnp.where(idx == i2, e * w1, 0.0)


def _attn_out(h, o, wo, gpost, gffn, wr, br, tm):
    m, d = h.shape
    row = pl.BlockSpec((tm, d), lambda i: (i, 0))
    comb = pl.BlockSpec((tm, LANES), lambda i: (i, 0))
    return pl.pallas_call(
        _attn_out_kernel,
        grid=(m // tm,),
        in_specs=[row, row, _resident(wo.shape), _resident((1, d)), _resident((1, d)), _resident(wr.shape),
                  _resident(br.shape)],
        out_specs=[row, row, comb],
        out_shape=[jax.ShapeDtypeStruct((m, d), F32), jax.ShapeDtypeStruct((m, d), BF16),
                   jax.ShapeDtypeStruct((m, LANES), F32)],
        compiler_params=_cparams("arbitrary"),
        name="attn_out",
    )(h, o, wo, gpost, gffn, wr, br)


def _moe_kernel(z_ref, comb_ref, wg_ref, wu_ref, wd_ref, f_ref):
    e = pl.program_id(1)
    c = pl.program_id(2)

    @pl.when((e == 0) & (c == 0))
    def _():
        f_ref[...] = jnp.zeros_like(f_ref)

    z = z_ref[...]
    act = (_silu(_dot(z, wg_ref[...])) * _dot(z, wu_ref[...])).astype(BF16)
    lane = lax.broadcasted_iota(jnp.int32, comb_ref.shape, 1)
    ce = jnp.sum(jnp.where(lane == e, comb_ref[...], 0.0), axis=-1, keepdims=True)
    f_ref[...] += ce * _dot(act, wd_ref[...])


def _moe(z, comb, wgu, wd, tm, ck=512):
    m, d = z.shape
    n_e, ff, _ = wd.shape
    fc = ff // ck
    return pl.pallas_call(
        _moe_kernel,
        grid=(m // tm, n_e, fc),
        in_specs=[pl.BlockSpec((tm, d), lambda i, e, c: (i, 0)), pl.BlockSpec((tm, LANES), lambda i, e, c: (i, 0)),
                  pl.BlockSpec((None, d, ck), lambda i, e, c: (e, 0, c)),
                  pl.BlockSpec((None, d, ck), lambda i, e, c: (e, 0, fc + c)),
                  pl.BlockSpec((None, ck, d), lambda i, e, c: (e, c, 0))],
        out_specs=pl.BlockSpec((tm, d), lambda i, e, c: (i, 0)),
        out_shape=jax.ShapeDtypeStruct((m, d), F32),
        compiler_params=_cparams("arbitrary", "arbitrary", "arbitrary"),
        name="moe",
    )(z, comb, wgu, wgu, wd)


def _tile(m, pref):
    return pref if m % pref == 0 else m


def kernel(x_prompt, x_sample, state_conv, state_lru, cache_k, cache_v, page_table, p_prompt, p_sample, norm_mix_pre, norm_mix_post, norm_ffn_pre, norm_ffn_post, norm_ple, rec_w_in, rec_conv_w, rec_conv_b, rec_w_a, rec_b_a, rec_w_x, rec_b_x, rec_lambda, rec_w_out, att_w_qkv, att_w_o, att_sb_bias, ffn_w_gu, ffn_w_down, moe_w_router, moe_b_router, moe_w_gu, moe_w_down, ple_w_proj, ple_w_gate):
    bsz, t, d = x_prompt.shape
    s = x_sample.shape[0]
    depth = norm_mix_pre.shape[0]
    mp = bsz * t
    tmp = _tile(mp, 512)
    vec = lambda a: a.reshape(1, -1)
    bf = lambda a: a.astype(BF16)
    hp = x_prompt.reshape(mp, d)
    hs = x_sample.reshape(s, d)
    keep = lambda a: a
    outs = {k: [] for k in ("conv_p", "lru_p", "k_p", "v_p", "conv_s", "lru_s", "k_s", "v_s")}
    for i in range(depth):
        j = i // 2
        if i % 2 == 0:
            def rec(cast):
                return (vec(norm_mix_pre[i]), cast(rec_w_in[j]), rec_conv_w[j], vec(rec_conv_b[j]), cast(rec_w_a[j]),
                        vec(rec_b_a[j]), cast(rec_w_x[j]), vec(rec_b_x[j]), vec(rec_lambda[j]), cast(rec_w_out[j]),
                        vec(norm_mix_post[i]))
            hp, conv_new, h_new = _rec_prompt(hp.reshape(bsz, t, d), *rec(bf), tt=_tile(t, 256))
            outs["conv_p"].append(conv_new)
            outs["lru_p"].append(h_new.reshape(bsz, d))
            hs, conv_new, h_new = _rec_step(hs, jnp.swapaxes(state_conv[j], 0, 1), state_lru[j], *rec(keep))
            outs["conv_s"].append(jnp.swapaxes(conv_new, 0, 1))
            outs["lru_s"].append(h_new)
            hp = hp.reshape(mp, d)
            fp = _ffn(hp, vec(norm_ffn_pre[i]), bf(ffn_w_gu[j]), bf(ffn_w_down[j]), tmp)
            fs = _ffn(hs, vec(norm_ffn_pre[i]), ffn_w_gu[j], ffn_w_down[j], s)
        else:
            hd = d // N_HEADS
            kp, vp, qb, kb, vb = _qkv(hp, vec(norm_mix_pre[i]), bf(att_w_qkv[j]), tmp)
            outs["k_p"].append(kp.reshape(bsz, t, N_HEADS, hd))
            outs["v_p"].append(vp.reshape(bsz, t, N_HEADS, hd))
            op = _sb_prompt(qb.reshape(bsz, t, d), kb.reshape(bsz, t, d), vb.reshape(bsz, t, d), att_sb_bias[j],
                            tq=_tile(t, 256))
            ks, vs, qs, _, _ = _qkv(hs, vec(norm_mix_pre[i]), att_w_qkv[j], s, q_dtype=F32)
            outs["k_s"].append(ks.reshape(s, 1, N_HEADS, hd))
            outs["v_s"].append(vs.reshape(s, 1, N_HEADS, hd))
            n_pages = page_table.shape[1]
            os_ = _sb_decode(qs.reshape(s, N_HEADS, hd, 1), jnp.transpose(cache_k[j], (0, 2, 3, 1)),
                             jnp.transpose(cache_v[j], (0, 2, 3, 1)), page_table, att_sb_bias[j].reshape(N_HEADS, 1),
                             pg=4 if n_pages % 4 == 0 else 1)
            wr = jnp.pad(moe_w_router[j], ((0, 0), (0, LANES - N_EXPERTS)))
            br = jnp.pad(moe_b_router[j], (0, LANES - N_EXPERTS), constant_values=NEG_BIG).reshape(1, LANES)
            norms = (vec(norm_mix_post[i]), vec(norm_ffn_pre[i]))
            hp, zp, cp = _attn_out(hp, op.reshape(mp, d), bf(att_w_o[j]), *norms, bf(wr), br, tmp)
            hs, zs, cs = _attn_out(hs, os_.reshape(s, d), att_w_o[j], *norms, wr, br, s)
            wgu, wd = bf(moe_w_gu[j]), bf(moe_w_down[j])
            fp = _moe(zp, cp, wgu, wd, _tile(mp, 1024))
            fs = _moe(zs, cs, wgu, wd, s)
        tail = (vec(norm_ffn_post[i]), vec(norm_ple[i]))
        hp = _finish(hp, fp, p_prompt[i].reshape(mp, -1), *tail, bf(ple_w_proj[i]), bf(ple_w_gate[i]), tmp)
        hs = _finish(hs, fs, p_sample[i].reshape(s, -1), *tail, ple_w_proj[i], ple_w_gate[i], s)
    st = lambda k: jnp.stack(outs[k])
    return (hp.reshape(bsz, t, d), hs.reshape(s, 1, d), st("conv_p"), st("lru_p"), st("k_p"), st("v_p"),
            st("conv_s"), st("lru_s"), st("k_s"), st("v_s"))
```

```python
import functools
import math

import jax
import jax.numpy as jnp
from jax import lax
from jax.experimental import pallas as pl
from jax.experimental.pallas import tpu as pltpu

F32 = jnp.float32
BF16 = jnp.bfloat16

RMS_EPS = 1e-6
LRU_C = 8.0
N_HEADS = 16
N_LRU_BLOCKS = 8
CONV_W = 4
N_EXPERTS = 8
TOP_K = 2
LANES = 128
NEG_BIG = -1e30
VMEM_LIMIT = 56 * 1024 * 1024


def _cparams(*sem):
    return pltpu.CompilerParams(dimension_semantics=sem, vmem_limit_bytes=VMEM_LIMIT)


def _resident(shape):
    return pl.BlockSpec(shape, lambda *_: (0,) * len(shape), pipeline_mode=pl.Buffered(1))


def _dot(a, b):
    return jnp.dot(a, b, preferred_element_type=F32)


def _mm(x, w):
    if w.dtype == F32:
        return jnp.dot(x.astype(F32), w, precision=lax.Precision.HIGHEST, preferred_element_type=F32)
    return jnp.dot(x.astype(w.dtype), w, preferred_element_type=F32)


def _dot_nt(a, b):
    return lax.dot_general(a, b, (((1,), (1,)), ((), ())), preferred_element_type=F32)


def _rms(x, g):
    return x * lax.rsqrt(jnp.mean(x * x, axis=-1, keepdims=True) + RMS_EPS) * g


def _softplus(x):
    return jnp.maximum(x, 0.0) + jnp.log1p(jnp.exp(-jnp.abs(x)))


def _gelu(x):
    c = math.sqrt(2.0 / math.pi)
    return 0.5 * x * (1.0 + jnp.tanh(c * (x + 0.044715 * (x * x * x))))


def _silu(x):
    return x * jax.nn.sigmoid(x)


def _lru_gates(xc, wa_ref, ba, wx_ref, bx, lam):
    blk = xc.shape[1] // N_LRU_BLOCKS
    ra, rx = [], []
    for n in range(N_LRU_BLOCKS):
        xb = xc[:, n * blk:(n + 1) * blk]
        ra.append(_mm(xb, wa_ref[n]))
        rx.append(_mm(xb, wx_ref[n]))
    r = jax.nn.sigmoid(jnp.concatenate(ra, axis=1) + ba)
    ig = jax.nn.sigmoid(jnp.concatenate(rx, axis=1) + bx)
    log_a = -LRU_C * r * _softplus(-lam)
    a = jnp.exp(log_a)
    gx = jnp.sqrt(-jnp.tanh(log_a) * (a * a + 1.0)) * ig * xc
    return a, gx


def _rec_prompt_kernel(x_ref, gpre_ref, win_ref, cw_ref, cb_ref, wa_ref, ba_ref, wx_ref, bx_ref, lam_ref,
                       wout_ref, gpost_ref, h_ref, conv_ref, hlast_ref, tail_ref, hc_ref):
    tt, d = x_ref.shape

    @pl.when(pl.program_id(1) == 0)
    def _():
        tail_ref[...] = jnp.zeros_like(tail_ref)
        hc_ref[...] = jnp.zeros_like(hc_ref)

    x = x_ref[...]
    xn = _rms(x, gpre_ref[...]).astype(win_ref.dtype)
    gate = _gelu(_mm(xn, win_ref[:, :d]))
    xr = _mm(xn, win_ref[:, d:])
    xpad = jnp.concatenate([tail_ref[...], xr], axis=0)
    xc = cb_ref[...] + cw_ref[3:4, :] * xr
    for back in range(1, CONV_W):
        xc = xc + cw_ref[CONV_W - 1 - back:CONV_W - back, :] * xpad[8 - back:8 - back + tt]
    tail_ref[...] = xr[tt - 8:]
    a, b = _lru_gates(xc, wa_ref, ba_ref[...], wx_ref, bx_ref[...], lam_ref[...])
    row = lax.broadcasted_iota(jnp.int32, (tt, 1), 0)
    s = 1
    while s < tt:
        keep = row >= s
        a_sh = jnp.where(keep, pltpu.roll(a, s, 0), 1.0)
        b_sh = jnp.where(keep, pltpu.roll(b, s, 0), 0.0)
        b = a * b_sh + b
        a = a * a_sh
        s *= 2
    h = b + a * hc_ref[...]
    hc_ref[...] = h[tt - 1:]
    y = _mm(h * gate, wout_ref[...])
    h_ref[...] = x + _rms(y, gpost_ref[...])
    conv_ref[...] = xr[tt - (CONV_W - 1):]
    hlast_ref[...] = h[tt - 1:]


def _rec_prompt(x, gpre, win, cw, cb, wa, ba, wx, bx, lam, wout, gpost, tt):
    bsz, t, d = x.shape
    vec = _resident((1, d))
    return pl.pallas_call(
        _rec_prompt_kernel,
        grid=(bsz, t // tt),
        in_specs=[pl.BlockSpec((None, tt, d), lambda b, i: (b, i, 0)), vec, _resident((d, 2 * d)),
                  _resident((CONV_W, d)), vec, _resident(wa.shape), vec, _resident(wx.shape), vec, vec,
                  _resident((d, d)), vec],
        out_specs=[pl.BlockSpec((None, tt, d), lambda b, i: (b, i, 0)),
                   pl.BlockSpec((None, CONV_W - 1, d), lambda b, i: (b, 0, 0)),
                   pl.BlockSpec((None, 1, d), lambda b, i: (b, 0, 0))],
        out_shape=[jax.ShapeDtypeStruct((bsz, t, d), F32), jax.ShapeDtypeStruct((bsz, CONV_W - 1, d), F32),
                   jax.ShapeDtypeStruct((bsz, 1, d), F32)],
        scratch_shapes=[pltpu.VMEM((8, d), F32), pltpu.VMEM((1, d), F32)],
        compiler_params=_cparams("arbitrary", "arbitrary"),
        name="rec_prompt",
    )(x, gpre, win, cw, cb, wa, ba, wx, bx, lam, wout, gpost)


def _rec_step_kernel(x_ref, sc_ref, h0_ref, gpre_ref, win_ref, cw_ref, cb_ref, wa_ref, ba_ref, wx_ref, bx_ref,
                     lam_ref, wout_ref, gpost_ref, h_ref, conv_ref, hnew_ref):
    d = x_ref.shape[1]
    x = x_ref[...]
    xn = _rms(x, gpre_ref[...]).astype(win_ref.dtype)
    gate = _gelu(_mm(xn, win_ref[:, :d]))
    xr = _mm(xn, win_ref[:, d:])
    xc = cb_ref[...] + cw_ref[CONV_W - 1:CONV_W, :] * xr
    for k in range(CONV_W - 1):
        xc = xc + cw_ref[k:k + 1, :] * sc_ref[k]
    a, b = _lru_gates(xc, wa_ref, ba_ref[...], wx_ref, bx_ref[...], lam_ref[...])
    h = a * h0_ref[...] + b
    y = _mm(h * gate, wout_ref[...])
    h_ref[...] = x + _rms(y, gpost_ref[...])
    for k in range(CONV_W - 2):
        conv_ref[k] = sc_ref[k + 1]
    conv_ref[CONV_W - 2] = xr
    hnew_ref[...] = h


def _rec_step(x, sc, h0, gpre, win, cw, cb, wa, ba, wx, bx, lam, wout, gpost):
    rows, d = x.shape
    return pl.pallas_call(
        _rec_step_kernel,
        out_shape=[jax.ShapeDtypeStruct((rows, d), F32), jax.ShapeDtypeStruct((CONV_W - 1, rows, d), F32),
                   jax.ShapeDtypeStruct((rows, d), F32)],
        compiler_params=pltpu.CompilerParams(vmem_limit_bytes=VMEM_LIMIT),
        name="rec_step",
    )(x, sc, h0, gpre, win, cw, cb, wa, ba, wx, bx, lam, wout, gpost)


def _ffn_kernel(h_ref, g_ref, wgu_ref, wd_ref, f_ref, *, ck):
    ff = wd_ref.shape[0]
    z = _rms(h_ref[...], g_ref[...]).astype(wgu_ref.dtype)
    acc = jnp.zeros(f_ref.shape, F32)
    for c in range(ff // ck):
        g = _mm(z, wgu_ref[:, c * ck:(c + 1) * ck])
        u = _mm(z, wgu_ref[:, ff + c * ck:ff + (c + 1) * ck])
        acc = acc + _mm(_silu(g) * u, wd_ref[c * ck:(c + 1) * ck, :])
    f_ref[...] = acc


def _ffn(h, g, wgu, wd, tm, ck=512):
    m, d = h.shape
    row = pl.BlockSpec((tm, d), lambda i: (i, 0))
    return pl.pallas_call(
        functools.partial(_ffn_kernel, ck=ck),
        grid=(m // tm,),
        in_specs=[row, _resident((1, d)), _resident(wgu.shape), _resident(wd.shape)],
        out_specs=row,
        out_shape=jax.ShapeDtypeStruct((m, d), F32),
        compiler_params=_cparams("arbitrary"),
        name="ffn_dense",
    )(h, g, wgu, wd)


def _finish_kernel(h_ref, f_ref, p_ref, gpost_ref, gple_ref, wproj_ref, wgate_ref, o_ref):
    h2 = h_ref[...] + _rms(f_ref[...], gpost_ref[...])
    e = _mm(p_ref[...], wproj_ref[...])
    g = jax.nn.sigmoid(_mm(_rms(h2, gple_ref[...]), wgate_ref[...]))
    o_ref[...] = h2 + g * e


def _finish(h, f, p, gpost, gple, wproj, wgate, tm):
    m, d = h.shape
    row = pl.BlockSpec((tm, d), lambda i: (i, 0))
    return pl.pallas_call(
        _finish_kernel,
        grid=(m // tm,),
        in_specs=[row, row, pl.BlockSpec((tm, p.shape[1]), lambda i: (i, 0)), _resident((1, d)), _resident((1, d)),
                  _resident(wproj.shape), _resident(wgate.shape)],
        out_specs=row,
        out_shape=jax.ShapeDtypeStruct((m, d), F32),
        compiler_params=_cparams("arbitrary"),
        name="finish",
    )(h, f, p, gpost, gple, wproj, wgate)


def _qkv_kernel(h_ref, g_ref, w_ref, k_ref, v_ref, qb_ref, kb_ref, vb_ref):
    d = h_ref.shape[1]
    xn = _rms(h_ref[...], g_ref[...]).astype(w_ref.dtype)
    scale = (d // N_HEADS) ** -0.5
    qb_ref[...] = (_mm(xn, w_ref[:, :d]) * scale).astype(qb_ref.dtype)
    k = _mm(xn, w_ref[:, d:2 * d])
    v = _mm(xn, w_ref[:, 2 * d:])
    k_ref[...] = k
    v_ref[...] = v
    kb_ref[...] = k.astype(BF16)
    vb_ref[...] = v.astype(BF16)


def _qkv(h, g, w, tm, q_dtype=BF16):
    m, d = h.shape
    row = pl.BlockSpec((tm, d), lambda i: (i, 0))
    return pl.pallas_call(
        _qkv_kernel,
        grid=(m // tm,),
        in_specs=[row, _resident((1, d)), _resident(w.shape)],
        out_specs=[row] * 5,
        out_shape=[jax.ShapeDtypeStruct((m, d), F32)] * 2 + [jax.ShapeDtypeStruct((m, d), q_dtype)]
        + [jax.ShapeDtypeStruct((m, d), BF16)] * 2,
        compiler_params=_cparams("arbitrary"),
        name="qkv",
    )(h, g, w)


def _sb_weights(z, sp, ls, tri, carry):
    hi = ls.astype(BF16)
    lo = (ls - hi.astype(F32)).astype(BF16)
    later = _dot(hi, tri) + _dot(lo, tri)
    return jnp.exp((z - sp) + later + carry)


def _sb_prompt_kernel(bias_ref, q_ref, k_ref, v_ref, o_ref, qs_ref, w_ref, acc_ref, carry_ref, *, hps):
    tq, width = q_ref.shape
    hd = width // hps
    grp = pl.program_id(1)
    i = pl.program_id(2)
    lane = lax.broadcasted_iota(jnp.int32, (tq, width), 1)
    q = q_ref[...]
    for h in range(hps):
        qs_ref[h * tq:(h + 1) * tq, :] = jnp.where((lane >= h * hd) & (lane < (h + 1) * hd), q, jnp.zeros_like(q))
    acc_ref[...] = jnp.zeros_like(acc_ref)
    carry_ref[...] = jnp.zeros_like(carry_ref)
    r2 = lax.broadcasted_iota(jnp.int32, (tq, tq), 0)
    c2 = lax.broadcasted_iota(jnp.int32, (tq, tq), 1)
    neg_tri = jnp.where(r2 > c2, -1.0, 0.0).astype(BF16)
    causal = jnp.concatenate([c2 < r2] * hps, axis=0)
    bias = jnp.concatenate([jnp.full((tq, 1), bias_ref[hps * grp + h], F32) for h in range(hps)], axis=0)

    def weights(j, diag):
        z = _dot_nt(qs_ref[...], k_ref[pl.ds(pl.multiple_of(j * tq, tq), tq), :]) + bias
        sp = jnp.maximum(z, 0.0) + jnp.log(1.0 + jnp.exp(-jnp.abs(z)))
        if diag:
            sp = jnp.where(causal, sp, 0.0)
        later = _dot(sp.astype(BF16), neg_tri)
        w = jnp.exp((z - sp) + later + carry_ref[...])
        if diag:
            w = jnp.where(causal, w, 0.0)
        w_ref[...] = w.astype(BF16)
        carry_ref[...] -= jnp.sum(sp, axis=-1, keepdims=True)

    def values(j):
        acc_ref[...] += _dot(w_ref[...], v_ref[pl.ds(pl.multiple_of(j * tq, tq), tq), :])

    weights(i, True)

    def body(jj, c):
        j = i - 1 - jj
        values(j + 1)
        weights(j, False)
        return c

    lax.fori_loop(0, i, body, 0)
    values(0)
    out = acc_ref[0:tq, :]
    for h in range(1, hps):
        out = jnp.where(lane >= h * hd, acc_ref[h * tq:(h + 1) * tq, :], out)
    o_ref[...] = out.astype(o_ref.dtype)


def _sb_prompt(qb, kb, vb, bias, tq, hps=4):
    bsz, t, d = qb.shape
    width = hps * (d // N_HEADS)
    qspec = pl.BlockSpec((None, tq, width), lambda b, g, i: (b, i, g))
    kvspec = pl.BlockSpec((None, t, width), lambda b, g, i: (b, 0, g))
    return pl.pallas_call(
        functools.partial(_sb_prompt_kernel, hps=hps),
        grid=(bsz, d // width, t // tq),
        in_specs=[pl.BlockSpec(memory_space=pltpu.SMEM), qspec, kvspec, kvspec],
        out_specs=qspec,
        out_shape=jax.ShapeDtypeStruct((bsz, t, d), BF16),
        scratch_shapes=[pltpu.VMEM((hps * tq, width), BF16), pltpu.VMEM((hps * tq, tq), BF16),
                        pltpu.VMEM((hps * tq, width), F32),
                        pltpu.VMEM((hps * tq, 1), F32)],
        compiler_params=_cparams("arbitrary", "arbitrary", "arbitrary"),
        name="sb_prompt",
    )(bias, qb, kb, vb)


def _sb_decode_kernel(pt_ref, q_ref, bias_ref, *refs, pg):
    del pt_ref
    k_refs, v_refs = refs[:pg], refs[pg:2 * pg]
    o_ref, acc_ref, carry_ref = refs[2 * pg:]
    nh, _, page = k_refs[0].shape
    g = pl.program_id(1)

    @pl.when(g == 0)
    def _():
        acc_ref[...] = jnp.zeros_like(acc_ref)
        carry_ref[...] = jnp.zeros_like(carry_ref)

    row = lax.broadcasted_iota(jnp.int32, (page, page), 0)
    col = lax.broadcasted_iota(jnp.int32, (page, page), 1)
    tri = jnp.where(row > col, 1.0, 0.0).astype(BF16)
    bias = bias_ref[...]
    for r in range(pg):
        z = jnp.concatenate([jnp.sum(q_ref[h] * k_refs[r][h], axis=0, keepdims=True) for h in range(nh)], axis=0)
        z = z + bias
        sp = _softplus(z)
        ls = -sp
        w = _sb_weights(z, sp, ls, tri, carry_ref[...])
        for h in range(nh):
            acc_ref[h] += w[h:h + 1, :] * v_refs[r][h]
        carry_ref[...] += jnp.sum(ls, axis=-1, keepdims=True)

    @pl.when(g == pl.num_programs(1) - 1)
    def _():
        o_ref[...] = jnp.sum(acc_ref[...], axis=-1)


def _sb_decode(q, cache_k, cache_v, page_table, bias, pg):
    s, nh, hd, _ = q.shape
    n_pages = page_table.shape[1]
    page = cache_k.shape[3]

    def page_spec(r):
        return pl.BlockSpec((None, nh, hd, page), lambda b, g, pt: (pt[b, n_pages - 1 - (g * pg + r)], 0, 0, 0))

    grid_spec = pltpu.PrefetchScalarGridSpec(
        num_scalar_prefetch=1,
        grid=(s, n_pages // pg),
        in_specs=[pl.BlockSpec((None, nh, hd, 1), lambda b, g, pt: (b, 0, 0, 0)),
                  pl.BlockSpec((nh, 1), lambda b, g, pt: (0, 0))] + [page_spec(r) for r in range(pg)] * 2,
        out_specs=pl.BlockSpec((None, nh, hd), lambda b, g, pt: (b, 0, 0)),
        scratch_shapes=[pltpu.VMEM((nh, hd, page), F32), pltpu.VMEM((nh, 1), F32)],
    )
    return pl.pallas_call(
        functools.partial(_sb_decode_kernel, pg=pg),
        grid_spec=grid_spec,
        out_shape=jax.ShapeDtypeStruct((s, nh, hd), F32),
        compiler_params=_cparams("arbitrary", "arbitrary"),
        name="sb_decode",
    )(page_table, q, bias, *([cache_k] * pg), *([cache_v] * pg))


def _mix_residual(h_ref, o_ref, wo_ref, gpost_ref, gffn_ref):
    h1 = h_ref[...] + _rms(_mm(o_ref[...], wo_ref[...]), gpost_ref[...])
    return h1, _rms(h1, gffn_ref[...])


def _top2(logits):
    idx = lax.broadcasted_iota(jnp.int32, logits.shape, 1)
    m1 = jnp.max(logits, axis=-1, keepdims=True)
    i1 = jnp.min(jnp.where(logits == m1, idx, LANES), axis=-1, keepdims=True)
    rest = jnp.where(idx == i1, NEG_BIG, logits)
    m2 = jnp.max(rest, axis=-1, keepdims=True)
    i2 = jnp.min(jnp.where(rest == m2, idx, LANES), axis=-1, keepdims=True)
    e = jnp.exp(m2 - m1)
    w1 = 1.0 / (1.0 + e)
    return idx, i1, i2, w1, e * w1


def _attn_out_kernel(h_ref, o_ref, wo_ref, gpost_ref, gffn_ref, wr_ref, br_ref, h1_ref, z_ref, comb_ref):
    h1, z = _mix_residual(h_ref, o_ref, wo_ref, gpost_ref, gffn_ref)
    h1_ref[...] = h1
    z_ref[...] = z.astype(z_ref.dtype)
    idx, i1, i2, w1, w2 = _top2(_mm(z, wr_ref[...]) + br_ref[...])
    comb_ref[...] = jnp.where(idx == i1, w1, 0.0) + jnp.where(idx == i2, w2, 0.0)


def _attn_out_route_kernel(h_ref, o_ref, wo_ref, gpost_ref, gffn_ref, wr_ref, br_ref, h1_ref, z_ref, cols_ref,
                           rows_ref):
    h1, z = _mix_residual(h_ref, o_ref, wo_ref, gpost_ref, gffn_ref)
    h1_ref[...] = h1
    z_ref[...] = z
    idx, i1, i2, w1, w2 = _top2(_mm(z, wr_ref[...]) + br_ref[...])
    cols = jnp.where(idx == 0, i1.astype(F32), jnp.where(idx == 1, i2.astype(F32),
                                                         jnp.where(idx == 2, w1, jnp.where(idx == 3, w2, 0.0))))
    cols_ref[...] = cols
    pick = (lax.broadcasted_iota(jnp.int32, (8, LANES), 0) == lax.broadcasted_iota(jnp.int32, (8, LANES), 1))
    rows_ref[...] = lax.dot_general(pick.astype(F32), cols, (((1,), (1,)), ((), ())),
                                    precision=lax.Precision.HIGHEST, preferred_element_type=F32)


def _attn_out(h, o, wo, gpost, gffn, wr, br, tm, route=False):
    m, d = h.shape
    row = pl.BlockSpec((tm, d), lambda i: (i, 0))
    comb = pl.BlockSpec((tm, LANES), lambda i: (i, 0))
    in_specs = [row, row, _resident(wo.shape), _resident((1, d)), _resident((1, d)), _resident(wr.shape),
                _resident(br.shape)]
    if route:
        return pl.pallas_call(
            _attn_out_route_kernel,
            grid=(m // tm,),
            in_specs=in_specs,
            out_specs=[row, row, comb, pl.BlockSpec((8, tm), lambda i: (0, i))],
            out_shape=[jax.ShapeDtypeStruct((m, d), F32), jax.ShapeDtypeStruct((m, d), F32),
                       jax.ShapeDtypeStruct((m, LANES), F32), jax.ShapeDtypeStruct((8, m), F32)],
            compiler_params=_cparams("arbitrary"),
            name="attn_out_route",
        )(h, o, wo, gpost, gffn, wr, br)
    return pl.pallas_call(
        _attn_out_kernel,
        grid=(m // tm,),
        in_specs=in_specs,
        out_specs=[row, row, comb],
        out_shape=[jax.ShapeDtypeStruct((m, d), F32), jax.ShapeDtypeStruct((m, d), BF16),
                   jax.ShapeDtypeStruct((m, LANES), F32)],
        compiler_params=_cparams("arbitrary"),
        name="attn_out",
    )(h, o, wo, gpost, gffn, wr, br)


def _moe_kernel(z_ref, comb_ref, wg_ref, wu_ref, wd_ref, f_ref):
    e = pl.program_id(1)
    c = pl.program_id(2)

    @pl.when((e == 0) & (c == 0))
    def _():
        f_ref[...] = jnp.zeros_like(f_ref)

    z = z_ref[...]
    act = (_silu(_dot(z, wg_ref[...])) * _dot(z, wu_ref[...])).astype(BF16)
    lane = lax.broadcasted_iota(jnp.int32, comb_ref.shape, 1)
    ce = jnp.sum(jnp.where(lane == e, comb_ref[...], 0.0), axis=-1, keepdims=True)
    f_ref[...] += ce * _dot(act, wd_ref[...])


def _moe(z, comb, wgu, wd, tm, ck=512):
    m, d = z.shape
    n_e, ff, _ = wd.shape
    fc = ff // ck
    return pl.pallas_call(
        _moe_kernel,
        grid=(m // tm, n_e, fc),
        in_specs=[pl.BlockSpec((tm, d), lambda i, e, c: (i, 0)), pl.BlockSpec((tm, LANES), lambda i, e, c: (i, 0)),
                  pl.BlockSpec((None, d, ck), lambda i, e, c: (e, 0, c)),
                  pl.BlockSpec((None, d, ck), lambda i, e, c: (e, 0, fc + c)),
                  pl.BlockSpec((None, ck, d), lambda i, e, c: (e, c, 0))],
        out_specs=pl.BlockSpec((tm, d), lambda i, e, c: (i, 0)),
        out_shape=jax.ShapeDtypeStruct((m, d), F32),
        compiler_params=_cparams("arbitrary", "arbitrary", "arbitrary"),
        name="moe",
    )(z, comb, wgu, wgu, wd)


def _route_plan(i1, i2, n_e, tm, n_tiles):
    experts = jnp.arange(n_e, dtype=jnp.int32)[None, :]
    hit = (experts == i1[:, None]).astype(jnp.int32) + (experts == i2[:, None]).astype(jnp.int32)
    upto = jnp.cumsum(hit, axis=0)
    before = upto - hit
    padded = ((upto[-1] + tm - 1) // tm) * tm
    ends = jnp.cumsum(padded)
    starts = ends - padded
    slot1 = starts[i1] + jnp.take_along_axis(before, i1[:, None], axis=1)[:, 0]
    slot2 = starts[i2] + jnp.take_along_axis(before, i2[:, None], axis=1)[:, 0]
    tile_start = jnp.arange(n_tiles, dtype=jnp.int32) * tm
    tile_expert = jnp.minimum(jnp.sum(tile_start[:, None] >= ends[None, :], axis=1), n_e - 1).astype(jnp.int32)
    return jnp.stack([slot1, slot2]).astype(jnp.int32), tile_expert, (ends[-1:] // tm).astype(jnp.int32)


def _row_copy(src_ref, src_row, dst_ref, dst_row, sem):
    return pltpu.make_async_copy(src_ref.at[pl.ds(src_row, 1), :], dst_ref.at[pl.ds(dst_row, 1), :], sem)


def _permute_kernel(slots_ref, z_ref, init_ref, xs_ref, sem):
    del init_ref
    rb = slots_ref.shape[1]
    base = pl.program_id(0) * rb

    def issue(r, c):
        for k in range(2):
            _row_copy(z_ref, base + r, xs_ref, slots_ref[k, r], sem).start()
        return c

    def drain(r, c):
        for k in range(2):
            _row_copy(z_ref, base + r, xs_ref, slots_ref[k, r], sem).wait()
        return c

    lax.fori_loop(0, rb, issue, 0)
    lax.fori_loop(0, rb, drain, 0)


def _permute(z, slots, n_slots, rb):
    m, d = z.shape
    any_spec = pl.BlockSpec(memory_space=pl.ANY)
    return pl.pallas_call(
        _permute_kernel,
        grid=(m // rb,),
        in_specs=[pl.BlockSpec((2, rb), lambda b: (0, b), memory_space=pltpu.SMEM), any_spec, any_spec],
        out_specs=any_spec,
        out_shape=jax.ShapeDtypeStruct((n_slots, d), z.dtype),
        scratch_shapes=[pltpu.SemaphoreType.DMA(())],
        input_output_aliases={2: 0},
        compiler_params=_cparams("arbitrary"),
        name="moe_permute",
    )(slots, z, jnp.zeros((n_slots, d), z.dtype))


def _experts_kernel(te_ref, nu_ref, x_ref, wg_ref, wu_ref, wd_ref, y_ref, xb_ref):
    del te_ref
    i = pl.program_id(0)
    c = pl.program_id(1)
    used = i < nu_ref[0]

    @pl.when(used & (c == 0))
    def _():
        xb_ref[...] = x_ref[...].astype(BF16)

    @pl.when(used)
    def _():
        xb = xb_ref[...]
        act = (_silu(_dot(xb, wg_ref[...])) * _dot(xb, wu_ref[...])).astype(BF16)
        y = _dot(act, wd_ref[...])

        @pl.when(c == 0)
        def _():
            y_ref[...] = y

        @pl.when(c > 0)
        def _():
            y_ref[...] += y

    @pl.when(jnp.logical_not(used) & (c == 0))
    def _():
        y_ref[...] = jnp.zeros_like(y_ref)


def _experts(xs, tile_expert, n_used, wgu, wd, tm, ck=512):
    n_slots, d = xs.shape
    ff = wd.shape[1]
    fc = ff // ck
    grid_spec = pltpu.PrefetchScalarGridSpec(
        num_scalar_prefetch=2,
        grid=(n_slots // tm, fc),
        in_specs=[pl.BlockSpec((tm, d), lambda i, c, te, nu: (jnp.minimum(i, nu[0] - 1), 0)),
                  pl.BlockSpec((None, d, ck), lambda i, c, te, nu: (te[i], 0, c)),
                  pl.BlockSpec((None, d, ck), lambda i, c, te, nu: (te[i], 0, fc + c)),
                  pl.BlockSpec((None, ck, d), lambda i, c, te, nu: (te[i], c, 0))],
        out_specs=pl.BlockSpec((tm, d), lambda i, c, te, nu: (i, 0)),
        scratch_shapes=[pltpu.VMEM((tm, d), BF16)],
    )
    return pl.pallas_call(
        _experts_kernel,
        grid_spec=grid_spec,
        out_shape=jax.ShapeDtypeStruct((n_slots, d), F32),
        compiler_params=_cparams("arbitrary", "arbitrary"),
        name="moe_experts",
    )(tile_expert, n_used, xs, wgu, wgu, wd)


def _finish_routed_kernel(slots_ref, h_ref, cols_ref, p_ref, gpost_ref, gple_ref, wproj_ref, wgate_ref, ys_ref,
                          o_ref, y_ref, sem):
    tm = h_ref.shape[0]

    def issue(r, c):
        for k in range(2):
            _row_copy(ys_ref, slots_ref[k, r], y_ref.at[k], r, sem).start()
        return c

    def drain(r, c):
        for k in range(2):
            _row_copy(ys_ref, slots_ref[k, r], y_ref.at[k], r, sem).wait()
        return c

    lax.fori_loop(0, tm, issue, 0)
    lax.fori_loop(0, tm, drain, 0)
    cols = cols_ref[...]
    f = cols[:, 2:3] * y_ref[0] + cols[:, 3:4] * y_ref[1]
    h2 = h_ref[...] + _rms(f, gpost_ref[...])
    e = _mm(p_ref[...], wproj_ref[...])
    g = jax.nn.sigmoid(_mm(_rms(h2, gple_ref[...]), wgate_ref[...]))
    o_ref[...] = h2 + g * e


def _finish_routed(h, ys, slots, cols, p, gpost, gple, wproj, wgate, tm):
    m, d = h.shape
    row = pl.BlockSpec((tm, d), lambda i: (i, 0))
    return pl.pallas_call(
        _finish_routed_kernel,
        grid=(m // tm,),
        in_specs=[pl.BlockSpec((2, tm), lambda i: (0, i), memory_space=pltpu.SMEM), row,
                  pl.BlockSpec((tm, LANES), lambda i: (i, 0)), pl.BlockSpec((tm, p.shape[1]), lambda i: (i, 0)),
                  _resident((1, d)), _resident((1, d)), _resident(wproj.shape), _resident(wgate.shape),
                  pl.BlockSpec(memory_space=pl.ANY)],
        out_specs=row,
        out_shape=jax.ShapeDtypeStruct((m, d), F32),
        scratch_shapes=[pltpu.VMEM((2, tm, d), F32), pltpu.SemaphoreType.DMA(())],
        compiler_params=_cparams("arbitrary"),
        name="finish_routed",
    )(slots, h, cols, p, gpost, gple, wproj, wgate, ys)


def _tile(m, pref):
    return pref if m % pref == 0 else m


def kernel(x_prompt, x_sample, state_conv, state_lru, cache_k, cache_v, page_table, p_prompt, p_sample, norm_mix_pre, norm_mix_post, norm_ffn_pre, norm_ffn_post, norm_ple, rec_w_in, rec_conv_w, rec_conv_b, rec_w_a, rec_b_a, rec_w_x, rec_b_x, rec_lambda, rec_w_out, att_w_qkv, att_w_o, att_sb_bias, ffn_w_gu, ffn_w_down, moe_w_router, moe_b_router, moe_w_gu, moe_w_down, ple_w_proj, ple_w_gate):
    bsz, t, d = x_prompt.shape
    s = x_sample.shape[0]
    depth = norm_mix_pre.shape[0]
    mp = bsz * t
    tmp = _tile(mp, 512)
    vec = lambda a: a.reshape(1, -1)
    bf = lambda a: a.astype(BF16)
    hp = x_prompt.reshape(mp, d)
    hs = x_sample.reshape(s, d)
    keep = lambda a: a
    outs = {k: [] for k in ("conv_p", "lru_p", "k_p", "v_p", "conv_s", "lru_s", "k_s", "v_s")}
    for i in range(depth):
        j = i // 2
        if i % 2 == 0:
            def rec(cast):
                return (vec(norm_mix_pre[i]), cast(rec_w_in[j]), rec_conv_w[j], vec(rec_conv_b[j]), cast(rec_w_a[j]),
                        vec(rec_b_a[j]), cast(rec_w_x[j]), vec(rec_b_x[j]), vec(rec_lambda[j]), cast(rec_w_out[j]),
                        vec(norm_mix_post[i]))
            hp, conv_new, h_new = _rec_prompt(hp.reshape(bsz, t, d), *rec(bf), tt=_tile(t, 256))
            outs["conv_p"].append(conv_new)
            outs["lru_p"].append(h_new.reshape(bsz, d))
            hs, conv_new, h_new = _rec_step(hs, jnp.swapaxes(state_conv[j], 0, 1), state_lru[j], *rec(keep))
            outs["conv_s"].append(jnp.swapaxes(conv_new, 0, 1))
            outs["lru_s"].append(h_new)
            hp = hp.reshape(mp, d)
            fp = _ffn(hp, vec(norm_ffn_pre[i]), bf(ffn_w_gu[j]), bf(ffn_w_down[j]), tmp)
            fs = _ffn(hs, vec(norm_ffn_pre[i]), ffn_w_gu[j], ffn_w_down[j], s)
        else:
            hd = d // N_HEADS
            kp, vp, qb, kb, vb = _qkv(hp, vec(norm_mix_pre[i]), bf(att_w_qkv[j]), tmp)
            outs["k_p"].append(kp.reshape(bsz, t, N_HEADS, hd))
            outs["v_p"].append(vp.reshape(bsz, t, N_HEADS, hd))
            op = _sb_prompt(qb.reshape(bsz, t, d), kb.reshape(bsz, t, d), vb.reshape(bsz, t, d), att_sb_bias[j],
                            tq=_tile(t, 256))
            ks, vs, qs, _, _ = _qkv(hs, vec(norm_mix_pre[i]), att_w_qkv[j], s, q_dtype=F32)
            outs["k_s"].append(ks.reshape(s, 1, N_HEADS, hd))
            outs["v_s"].append(vs.reshape(s, 1, N_HEADS, hd))
            n_pages = page_table.shape[1]
            os_ = _sb_decode(qs.reshape(s, N_HEADS, hd, 1), jnp.transpose(cache_k[j], (0, 2, 3, 1)),
                             jnp.transpose(cache_v[j], (0, 2, 3, 1)), page_table, att_sb_bias[j].reshape(N_HEADS, 1),
                             pg=4 if n_pages % 4 == 0 else 1)
            wr = jnp.pad(moe_w_router[j], ((0, 0), (0, LANES - N_EXPERTS)))
            br = jnp.pad(moe_b_router[j], (0, LANES - N_EXPERTS), constant_values=NEG_BIG).reshape(1, LANES)
            norms = (vec(norm_mix_post[i]), vec(norm_ffn_pre[i]))
            hp, zp, cols, rows = _attn_out(hp, op.reshape(mp, d), bf(att_w_o[j]), *norms, bf(wr), br, tmp, route=True)
            hs, zs, cs = _attn_out(hs, os_.reshape(s, d), att_w_o[j], *norms, wr, br, s)
            wgu, wd = bf(moe_w_gu[j]), bf(moe_w_down[j])
            fs = _moe(zs, cs, wgu, wd, s)
            tme = _tile(mp, 512)
            n_tiles = (TOP_K * mp + N_EXPERTS * (tme - 1)) // tme
            slots, tile_expert, n_used = _route_plan(rows[0].astype(jnp.int32), rows[1].astype(jnp.int32),
                                                     N_EXPERTS, tme, n_tiles)
            ys = _experts(_permute(zp, slots, n_tiles * tme, tmp), tile_expert, n_used, wgu, wd, tme)
        tail = (vec(norm_ffn_post[i]), vec(norm_ple[i]))
        if i % 2 == 0:
            hp = _finish(hp, fp, p_prompt[i].reshape(mp, -1), *tail, bf(ple_w_proj[i]), bf(ple_w_gate[i]), tmp)
        else:
            hp = _finish_routed(hp, ys, slots, cols, p_prompt[i].reshape(mp, -1), *tail, bf(ple_w_proj[i]),
                                bf(ple_w_gate[i]), tmp)
        hs = _finish(hs, fs, p_sample[i].reshape(s, -1), *tail, ple_w_proj[i], ple_w_gate[i], s)
    st = lambda k: jnp.stack(outs[k])
    return (hp.reshape(bsz, t, d), hs.reshape(s, 1, d), st("conv_p"), st("lru_p"), st("k_p"), st("v_p"),
            st("conv_s"), st("lru_s"), st("k_s"), st("v_s"))
```

```python
import functools
import math

import jax
import jax.numpy as jnp
from jax import lax
from jax.experimental import pallas as pl
from jax.experimental.pallas import tpu as pltpu

F32 = jnp.float32
BF16 = jnp.bfloat16

RMS_EPS = 1e-6
LRU_C = 8.0
N_HEADS = 16
N_LRU_BLOCKS = 8
CONV_W = 4
N_EXPERTS = 8
TOP_K = 2
LANES = 128
NEG_BIG = -1e30
VMEM_LIMIT = 56 * 1024 * 1024


def _cparams(*sem):
    return pltpu.CompilerParams(dimension_semantics=sem, vmem_limit_bytes=VMEM_LIMIT)


def _resident(shape):
    return pl.BlockSpec(shape, lambda *_: (0,) * len(shape), pipeline_mode=pl.Buffered(1))


def _dot(a, b):
    return jnp.dot(a, b, preferred_element_type=F32)


def _mm(x, w):
    if w.dtype == F32:
        return jnp.dot(x.astype(F32), w, precision=lax.Precision.HIGHEST, preferred_element_type=F32)
    return jnp.dot(x.astype(w.dtype), w, preferred_element_type=F32)


def _dot_nt(a, b):
    return lax.dot_general(a, b, (((1,), (1,)), ((), ())), preferred_element_type=F32)


def _rms(x, g):
    return x * lax.rsqrt(jnp.mean(x * x, axis=-1, keepdims=True) + RMS_EPS) * g


def _softplus(x):
    return jnp.maximum(x, 0.0) + jnp.log1p(jnp.exp(-jnp.abs(x)))


def _gelu(x):
    c = math.sqrt(2.0 / math.pi)
    return 0.5 * x * (1.0 + jnp.tanh(c * (x + 0.044715 * (x * x * x))))


def _silu(x):
    return x * jax.nn.sigmoid(x)


def _lru_gates(xc, wa_ref, ba, wx_ref, bx, lam):
    blk = xc.shape[1] // N_LRU_BLOCKS
    ra, rx = [], []
    for n in range(N_LRU_BLOCKS):
        xb = xc[:, n * blk:(n + 1) * blk]
        ra.append(_mm(xb, wa_ref[n]))
        rx.append(_mm(xb, wx_ref[n]))
    r = jax.nn.sigmoid(jnp.concatenate(ra, axis=1) + ba)
    ig = jax.nn.sigmoid(jnp.concatenate(rx, axis=1) + bx)
    log_a = -LRU_C * r * _softplus(-lam)
    a = jnp.exp(log_a)
    gx = jnp.sqrt(-jnp.tanh(log_a) * (a * a + 1.0)) * ig * xc
    return a, gx


def _rec_prompt_kernel(x_ref, gpre_ref, win_ref, cw_ref, cb_ref, wa_ref, ba_ref, wx_ref, bx_ref, lam_ref,
                       wout_ref, gpost_ref, h_ref, conv_ref, hlast_ref, tail_ref, hc_ref):
    tt, d = x_ref.shape

    @pl.when(pl.program_id(1) == 0)
    def _():
        tail_ref[...] = jnp.zeros_like(tail_ref)
        hc_ref[...] = jnp.zeros_like(hc_ref)

    x = x_ref[...]
    xn = _rms(x, gpre_ref[...]).astype(win_ref.dtype)
    gate = _gelu(_mm(xn, win_ref[:, :d]))
    xr = _mm(xn, win_ref[:, d:])
    xpad = jnp.concatenate([tail_ref[...], xr], axis=0)
    xc = cb_ref[...] + cw_ref[3:4, :] * xr
    for back in range(1, CONV_W):
        xc = xc + cw_ref[CONV_W - 1 - back:CONV_W - back, :] * xpad[8 - back:8 - back + tt]
    tail_ref[...] = xr[tt - 8:]
    a, b = _lru_gates(xc, wa_ref, ba_ref[...], wx_ref, bx_ref[...], lam_ref[...])
    row = lax.broadcasted_iota(jnp.int32, (tt, 1), 0)
    s = 1
    while s < tt:
        keep = row >= s
        a_sh = jnp.where(keep, pltpu.roll(a, s, 0), 1.0)
        b_sh = jnp.where(keep, pltpu.roll(b, s, 0), 0.0)
        b = a * b_sh + b
        a = a * a_sh
        s *= 2
    h = b + a * hc_ref[...]
    hc_ref[...] = h[tt - 1:]
    y = _mm(h * gate, wout_ref[...])
    h_ref[...] = x + _rms(y, gpost_ref[...])
    conv_ref[...] = xr[tt - (CONV_W - 1):]
    hlast_ref[...] = h[tt - 1:]


def _rec_prompt(x, gpre, win, cw, cb, wa, ba, wx, bx, lam, wout, gpost, tt):
    bsz, t, d = x.shape
    vec = _resident((1, d))
    return pl.pallas_call(
        _rec_prompt_kernel,
        grid=(bsz, t // tt),
        in_specs=[pl.BlockSpec((None, tt, d), lambda b, i: (b, i, 0)), vec, _resident((d, 2 * d)),
                  _resident((CONV_W, d)), vec, _resident(wa.shape), vec, _resident(wx.shape), vec, vec,
                  _resident((d, d)), vec],
        out_specs=[pl.BlockSpec((None, tt, d), lambda b, i: (b, i, 0)),
                   pl.BlockSpec((None, CONV_W - 1, d), lambda b, i: (b, 0, 0)),
                   pl.BlockSpec((None, 1, d), lambda b, i: (b, 0, 0))],
        out_shape=[jax.ShapeDtypeStruct((bsz, t, d), F32), jax.ShapeDtypeStruct((bsz, CONV_W - 1, d), F32),
                   jax.ShapeDtypeStruct((bsz, 1, d), F32)],
        scratch_shapes=[pltpu.VMEM((8, d), F32), pltpu.VMEM((1, d), F32)],
        compiler_params=_cparams("arbitrary", "arbitrary"),
        name="rec_prompt",
    )(x, gpre, win, cw, cb, wa, ba, wx, bx, lam, wout, gpost)


def _rec_step_kernel(x_ref, sc_ref, h0_ref, gpre_ref, win_ref, cw_ref, cb_ref, wa_ref, ba_ref, wx_ref, bx_ref,
                     lam_ref, wout_ref, gpost_ref, h_ref, conv_ref, hnew_ref):
    d = x_ref.shape[1]
    x = x_ref[...]
    xn = _rms(x, gpre_ref[...]).astype(win_ref.dtype)
    gate = _gelu(_mm(xn, win_ref[:, :d]))
    xr = _mm(xn, win_ref[:, d:])
    xc = cb_ref[...] + cw_ref[CONV_W - 1:CONV_W, :] * xr
    for k in range(CONV_W - 1):
        xc = xc + cw_ref[k:k + 1, :] * sc_ref[k]
    a, b = _lru_gates(xc, wa_ref, ba_ref[...], wx_ref, bx_ref[...], lam_ref[...])
    h = a * h0_ref[...] + b
    y = _mm(h * gate, wout_ref[...])
    h_ref[...] = x + _rms(y, gpost_ref[...])
    for k in range(CONV_W - 2):
        conv_ref[k] = sc_ref[k + 1]
    conv_ref[CONV_W - 2] = xr
    hnew_ref[...] = h


def _rec_step(x, sc, h0, gpre, win, cw, cb, wa, ba, wx, bx, lam, wout, gpost):
    rows, d = x.shape
    return pl.pallas_call(
        _rec_step_kernel,
        out_shape=[jax.ShapeDtypeStruct((rows, d), F32), jax.ShapeDtypeStruct((CONV_W - 1, rows, d), F32),
                   jax.ShapeDtypeStruct((rows, d), F32)],
        compiler_params=pltpu.CompilerParams(vmem_limit_bytes=VMEM_LIMIT),
        name="rec_step",
    )(x, sc, h0, gpre, win, cw, cb, wa, ba, wx, bx, lam, wout, gpost)


def _ffn_kernel(h_ref, g_ref, wgu_ref, wd_ref, f_ref, *, ck):
    ff = wd_ref.shape[0]
    z = _rms(h_ref[...], g_ref[...]).astype(wgu_ref.dtype)
    acc = jnp.zeros(f_ref.shape, F32)
    for c in range(ff // ck):
        g = _mm(z, wgu_ref[:, c * ck:(c + 1) * ck])
        u = _mm(z, wgu_ref[:, ff + c * ck:ff + (c + 1) * ck])
        acc = acc + _mm(_silu(g) * u, wd_ref[c * ck:(c + 1) * ck, :])
    f_ref[...] = acc


def _ffn(h, g, wgu, wd, tm, ck=512):
    m, d = h.shape
    row = pl.BlockSpec((tm, d), lambda i: (i, 0))
    return pl.pallas_call(
        functools.partial(_ffn_kernel, ck=ck),
        grid=(m // tm,),
        in_specs=[row, _resident((1, d)), _resident(wgu.shape), _resident(wd.shape)],
        out_specs=row,
        out_shape=jax.ShapeDtypeStruct((m, d), F32),
        compiler_params=_cparams("arbitrary"),
        name="ffn_dense",
    )(h, g, wgu, wd)


def _finish_kernel(h_ref, f_ref, p_ref, gpost_ref, gple_ref, wproj_ref, wgate_ref, o_ref):
    h2 = h_ref[...] + _rms(f_ref[...], gpost_ref[...])
    e = _mm(p_ref[...], wproj_ref[...])
    g = jax.nn.sigmoid(_mm(_rms(h2, gple_ref[...]), wgate_ref[...]))
    o_ref[...] = h2 + g * e


def _finish(h, f, p, gpost, gple, wproj, wgate, tm):
    m, d = h.shape
    row = pl.BlockSpec((tm, d), lambda i: (i, 0))
    return pl.pallas_call(
        _finish_kernel,
        grid=(m // tm,),
        in_specs=[row, row, pl.BlockSpec((tm, p.shape[1]), lambda i: (i, 0)), _resident((1, d)), _resident((1, d)),
                  _resident(wproj.shape), _resident(wgate.shape)],
        out_specs=row,
        out_shape=jax.ShapeDtypeStruct((m, d), F32),
        compiler_params=_cparams("arbitrary"),
        name="finish",
    )(h, f, p, gpost, gple, wproj, wgate)


def _qkv_kernel(h_ref, g_ref, w_ref, k_ref, v_ref, qb_ref, kb_ref, vb_ref):
    d = h_ref.shape[1]
    xn = _rms(h_ref[...], g_ref[...]).astype(w_ref.dtype)
    scale = (d // N_HEADS) ** -0.5
    qb_ref[...] = (_mm(xn, w_ref[:, :d]) * scale).astype(qb_ref.dtype)
    k = _mm(xn, w_ref[:, d:2 * d])
    v = _mm(xn, w_ref[:, 2 * d:])
    k_ref[...] = k
    v_ref[...] = v
    kb_ref[...] = k.astype(BF16)
    vb_ref[...] = v.astype(BF16)


def _qkv(h, g, w, tm, q_dtype=BF16):
    m, d = h.shape
    row = pl.BlockSpec((tm, d), lambda i: (i, 0))
    return pl.pallas_call(
        _qkv_kernel,
        grid=(m // tm,),
        in_specs=[row, _resident((1, d)), _resident(w.shape)],
        out_specs=[row] * 5,
        out_shape=[jax.ShapeDtypeStruct((m, d), F32)] * 2 + [jax.ShapeDtypeStruct((m, d), q_dtype)]
        + [jax.ShapeDtypeStruct((m, d), BF16)] * 2,
        compiler_params=_cparams("arbitrary"),
        name="qkv",
    )(h, g, w)


def _sb_weights(z, sp, ls, tri, carry):
    hi = ls.astype(BF16)
    lo = (ls - hi.astype(F32)).astype(BF16)
    later = _dot(hi, tri) + _dot(lo, tri)
    return jnp.exp((z - sp) + later + carry)


def _sb_prompt_kernel(bias_ref, q_ref, k_ref, v_ref, o_ref, qs_ref, w_ref, acc_ref, carry_ref, *, hps):
    tq, width = q_ref.shape
    hd = width // hps
    grp = pl.program_id(1)
    i = pl.program_id(2)
    lane = lax.broadcasted_iota(jnp.int32, (tq, width), 1)
    q = q_ref[...]
    for h in range(hps):
        qs_ref[h * tq:(h + 1) * tq, :] = jnp.where((lane >= h * hd) & (lane < (h + 1) * hd), q, jnp.zeros_like(q))
    acc_ref[...] = jnp.zeros_like(acc_ref)
    carry_ref[...] = jnp.zeros_like(carry_ref)
    r2 = lax.broadcasted_iota(jnp.int32, (tq, tq), 0)
    c2 = lax.broadcasted_iota(jnp.int32, (tq, tq), 1)
    neg_tri = jnp.where(r2 > c2, -1.0, 0.0).astype(BF16)
    causal = jnp.concatenate([c2 < r2] * hps, axis=0)
    bias = jnp.concatenate([jnp.full((tq, 1), bias_ref[hps * grp + h], F32) for h in range(hps)], axis=0)

    def weights(j, diag):
        z = _dot_nt(qs_ref[...], k_ref[pl.ds(pl.multiple_of(j * tq, tq), tq), :]) + bias
        sp = jnp.maximum(z, 0.0) + jnp.log(1.0 + jnp.exp(-jnp.abs(z)))
        if diag:
            sp = jnp.where(causal, sp, 0.0)
        later = _dot(sp.astype(BF16), neg_tri)
        w = jnp.exp((z - sp) + later + carry_ref[...])
        if diag:
            w = jnp.where(causal, w, 0.0)
        w_ref[...] = w.astype(BF16)
        carry_ref[...] -= jnp.sum(sp, axis=-1, keepdims=True)

    def values(j):
        acc_ref[...] += _dot(w_ref[...], v_ref[pl.ds(pl.multiple_of(j * tq, tq), tq), :])

    weights(i, True)

    def body(jj, c):
        j = i - 1 - jj
        values(j + 1)
        weights(j, False)
        return c

    lax.fori_loop(0, i, body, 0)
    values(0)
    out = acc_ref[0:tq, :]
    for h in range(1, hps):
        out = jnp.where(lane >= h * hd, acc_ref[h * tq:(h + 1) * tq, :], out)
    o_ref[...] = out.astype(o_ref.dtype)


def _sb_prompt(qb, kb, vb, bias, tq, hps=4):
    bsz, t, d = qb.shape
    width = hps * (d // N_HEADS)
    qspec = pl.BlockSpec((None, tq, width), lambda b, g, i: (b, i, g))
    kvspec = pl.BlockSpec((None, t, width), lambda b, g, i: (b, 0, g))
    return pl.pallas_call(
        functools.partial(_sb_prompt_kernel, hps=hps),
        grid=(bsz, d // width, t // tq),
        in_specs=[pl.BlockSpec(memory_space=pltpu.SMEM), qspec, kvspec, kvspec],
        out_specs=qspec,
        out_shape=jax.ShapeDtypeStruct((bsz, t, d), BF16),
        scratch_shapes=[pltpu.VMEM((hps * tq, width), BF16), pltpu.VMEM((hps * tq, tq), BF16),
                        pltpu.VMEM((hps * tq, width), F32),
                        pltpu.VMEM((hps * tq, 1), F32)],
        compiler_params=_cparams("arbitrary", "arbitrary", "arbitrary"),
        name="sb_prompt",
    )(bias, qb, kb, vb)


def _sb_decode_kernel(pt_ref, q_ref, bias_ref, *refs, pg):
    del pt_ref
    k_refs, v_refs = refs[:pg], refs[pg:2 * pg]
    o_ref, acc_ref, carry_ref = refs[2 * pg:]
    nh, _, page = k_refs[0].shape
    g = pl.program_id(1)

    @pl.when(g == 0)
    def _():
        acc_ref[...] = jnp.zeros_like(acc_ref)
        carry_ref[...] = jnp.zeros_like(carry_ref)

    row = lax.broadcasted_iota(jnp.int32, (page, page), 0)
    col = lax.broadcasted_iota(jnp.int32, (page, page), 1)
    tri = jnp.where(row > col, 1.0, 0.0).astype(BF16)
    bias = bias_ref[...]
    for r in range(pg):
        z = jnp.concatenate([jnp.sum(q_ref[h] * k_refs[r][h], axis=0, keepdims=True) for h in range(nh)], axis=0)
        z = z + bias
        sp = _softplus(z)
        ls = -sp
        w = _sb_weights(z, sp, ls, tri, carry_ref[...])
        for h in range(nh):
            acc_ref[h] += w[h:h + 1, :] * v_refs[r][h]
        carry_ref[...] += jnp.sum(ls, axis=-1, keepdims=True)

    @pl.when(g == pl.num_programs(1) - 1)
    def _():
        o_ref[...] = jnp.sum(acc_ref[...], axis=-1)


def _sb_decode(q, cache_k, cache_v, page_table, bias, pg):
    s, nh, hd, _ = q.shape
    n_pages = page_table.shape[1]
    page = cache_k.shape[3]

    def page_spec(r):
        return pl.BlockSpec((None, nh, hd, page), lambda b, g, pt: (pt[b, n_pages - 1 - (g * pg + r)], 0, 0, 0))

    grid_spec = pltpu.PrefetchScalarGridSpec(
        num_scalar_prefetch=1,
        grid=(s, n_pages // pg),
        in_specs=[pl.BlockSpec((None, nh, hd, 1), lambda b, g, pt: (b, 0, 0, 0)),
                  pl.BlockSpec((nh, 1), lambda b, g, pt: (0, 0))] + [page_spec(r) for r in range(pg)] * 2,
        out_specs=pl.BlockSpec((None, nh, hd), lambda b, g, pt: (b, 0, 0)),
        scratch_shapes=[pltpu.VMEM((nh, hd, page), F32), pltpu.VMEM((nh, 1), F32)],
    )
    return pl.pallas_call(
        functools.partial(_sb_decode_kernel, pg=pg),
        grid_spec=grid_spec,
        out_shape=jax.ShapeDtypeStruct((s, nh, hd), F32),
        compiler_params=_cparams("arbitrary", "arbitrary"),
        name="sb_decode",
    )(page_table, q, bias, *([cache_k] * pg), *([cache_v] * pg))


def _mix_residual(h_ref, o_ref, wo_ref, gpost_ref, gffn_ref):
    h1 = h_ref[...] + _rms(_mm(o_ref[...], wo_ref[...]), gpost_ref[...])
    return h1, _rms(h1, gffn_ref[...])


def _top2(logits):
    idx = lax.broadcasted_iota(jnp.int32, logits.shape, 1)
    m1 = jnp.max(logits, axis=-1, keepdims=True)
    i1 = jnp.min(jnp.where(logits == m1, idx, LANES), axis=-1, keepdims=True)
    rest = jnp.where(idx == i1, NEG_BIG, logits)
    m2 = jnp.max(rest, axis=-1, keepdims=True)
    i2 = jnp.min(jnp.where(rest == m2, idx, LANES), axis=-1, keepdims=True)
    e = jnp.exp(m2 - m1)
    w1 = 1.0 / (1.0 + e)
    return idx, i1, i2, w1, e * w1


def _attn_out_kernel(h_ref, o_ref, wo_ref, gpost_ref, gffn_ref, wr_ref, br_ref, h1_ref, z_ref, comb_ref):
    h1, z = _mix_residual(h_ref, o_ref, wo_ref, gpost_ref, gffn_ref)
    h1_ref[...] = h1
    z_ref[...] = z.astype(z_ref.dtype)
    idx, i1, i2, w1, w2 = _top2(_mm(z, wr_ref[...]) + br_ref[...])
    comb_ref[...] = jnp.where(idx == i1, w1, 0.0) + jnp.where(idx == i2, w2, 0.0)


def _attn_out_route_kernel(h_ref, o_ref, wo_ref, gpost_ref, gffn_ref, wr_ref, br_ref, h1_ref, z_ref, cols_ref,
                           rows_ref):
    h1, z = _mix_residual(h_ref, o_ref, wo_ref, gpost_ref, gffn_ref)
    h1_ref[...] = h1
    z_ref[...] = z
    idx, i1, i2, w1, w2 = _top2(_mm(z, wr_ref[...]) + br_ref[...])
    cols = jnp.where(idx == 0, i1.astype(F32), jnp.where(idx == 1, i2.astype(F32),
                                                         jnp.where(idx == 2, w1, jnp.where(idx == 3, w2, 0.0))))
    cols_ref[...] = cols
    pick = (lax.broadcasted_iota(jnp.int32, (8, LANES), 0) == lax.broadcasted_iota(jnp.int32, (8, LANES), 1))
    rows_ref[...] = lax.dot_general(pick.astype(F32), cols, (((1,), (1,)), ((), ())),
                                    precision=lax.Precision.HIGHEST, preferred_element_type=F32)


def _attn_out(h, o, wo, gpost, gffn, wr, br, tm, route=False):
    m, d = h.shape
    row = pl.BlockSpec((tm, d), lambda i: (i, 0))
    comb = pl.BlockSpec((tm, LANES), lambda i: (i, 0))
    in_specs = [row, row, _resident(wo.shape), _resident((1, d)), _resident((1, d)), _resident(wr.shape),
                _resident(br.shape)]
    if route:
        return pl.pallas_call(
            _attn_out_route_kernel,
            grid=(m // tm,),
            in_specs=in_specs,
            out_specs=[row, row, comb, pl.BlockSpec((8, tm), lambda i: (0, i))],
            out_shape=[jax.ShapeDtypeStruct((m, d), F32), jax.ShapeDtypeStruct((m, d), F32),
                       jax.ShapeDtypeStruct((m, LANES), F32), jax.ShapeDtypeStruct((8, m), F32)],
            compiler_params=_cparams("arbitrary"),
            name="attn_out_route",
        )(h, o, wo, gpost, gffn, wr, br)
    return pl.pallas_call(
        _attn_out_kernel,
        grid=(m // tm,),
        in_specs=in_specs,
        out_specs=[row, row, comb],
        out_shape=[jax.ShapeDtypeStruct((m, d), F32), jax.ShapeDtypeStruct((m, d), BF16),
                   jax.ShapeDtypeStruct((m, LANES), F32)],
        compiler_params=_cparams("arbitrary"),
        name="attn_out",
    )(h, o, wo, gpost, gffn, wr, br)


def _moe_kernel(z_ref, comb_ref, wg_ref, wu_ref, wd_ref, f_ref):
    e = pl.program_id(1)
    c = pl.program_id(2)

    @pl.when((e == 0) & (c == 0))
    def _():
        f_ref[...] = jnp.zeros_like(f_ref)

    z = z_ref[...]
    act = (_silu(_dot(z, wg_ref[...])) * _dot(z, wu_ref[...])).astype(BF16)
    lane = lax.broadcasted_iota(jnp.int32, comb_ref.shape, 1)
    ce = jnp.sum(jnp.where(lane == e, comb_ref[...], 0.0), axis=-1, keepdims=True)
    f_ref[...] += ce * _dot(act, wd_ref[...])


def _moe(z, comb, wgu, wd, tm, ck=512):
    m, d = z.shape
    n_e, ff, _ = wd.shape
    fc = ff // ck
    return pl.pallas_call(
        _moe_kernel,
        grid=(m // tm, n_e, fc),
        in_specs=[pl.BlockSpec((tm, d), lambda i, e, c: (i, 0)), pl.BlockSpec((tm, LANES), lambda i, e, c: (i, 0)),
                  pl.BlockSpec((None, d, ck), lambda i, e, c: (e, 0, c)),
                  pl.BlockSpec((None, d, ck), lambda i, e, c: (e, 0, fc + c)),
                  pl.BlockSpec((None, ck, d), lambda i, e, c: (e, c, 0))],
        out_specs=pl.BlockSpec((tm, d), lambda i, e, c: (i, 0)),
        out_shape=jax.ShapeDtypeStruct((m, d), F32),
        compiler_params=_cparams("arbitrary", "arbitrary", "arbitrary"),
        name="moe",
    )(z, comb, wgu, wgu, wd)


def _route_plan(i1, i2, n_e, tm, n_tiles):
    m = i1.shape[0]
    experts = jnp.arange(n_e, dtype=jnp.int32)[None, :]
    hit = (experts == i1[:, None]).astype(jnp.int32) + (experts == i2[:, None]).astype(jnp.int32)
    upto = jnp.cumsum(hit, axis=0)
    before = upto - hit
    padded = ((upto[-1] + tm - 1) // tm) * tm
    ends = jnp.cumsum(padded)
    starts = ends - padded
    slot1 = starts[i1] + jnp.take_along_axis(before, i1[:, None], axis=1)[:, 0]
    slot2 = starts[i2] + jnp.take_along_axis(before, i2[:, None], axis=1)[:, 0]
    slots = jnp.stack([slot1, slot2]).astype(jnp.int32)
    token = jnp.tile(jnp.arange(m, dtype=jnp.int32), 2)
    source = jnp.zeros((n_tiles * tm,), jnp.int32).at[slots.reshape(-1)].set(token).reshape(n_tiles, 1, tm)
    tile_start = jnp.arange(n_tiles, dtype=jnp.int32) * tm
    tile_expert = jnp.minimum(jnp.sum(tile_start[:, None] >= ends[None, :], axis=1), n_e - 1).astype(jnp.int32)
    return slots, source, tile_expert, (ends[-1:] // tm).astype(jnp.int32)


def _row_copy(src_ref, src_row, dst_ref, dst_row, sem):
    return pltpu.make_async_copy(src_ref.at[pl.ds(src_row, 1), :], dst_ref.at[pl.ds(dst_row, 1), :], sem)


def _experts_kernel(te_ref, nu_ref, src_ref, nxt_ref, z_ref, wg_ref, wu_ref, wd_ref, y_ref, xg_ref, xb_ref, sem):
    del te_ref
    i = pl.program_id(0)
    c = pl.program_id(1)
    tm = xb_ref.shape[0]
    n_used = nu_ref[0]
    used = i < n_used

    def gather(rows_ref, buf, start):
        def body(r, carry):
            cp = _row_copy(z_ref, rows_ref[0, r], xg_ref.at[buf], r, sem.at[buf])
            cp.start() if start else cp.wait()
            return carry

        lax.fori_loop(0, tm, body, 0)

    @pl.when(used & (c == 0))
    def _():
        buf = i % 2

        @pl.when(i == 0)
        def _():
            gather(src_ref, 0, True)

        gather(src_ref, buf, False)
        xb_ref[...] = xg_ref[buf].astype(BF16)

        @pl.when(i + 1 < n_used)
        def _():
            gather(nxt_ref, 1 - buf, True)

    @pl.when(used)
    def _():
        xb = xb_ref[...]
        act = (_silu(_dot(xb, wg_ref[...])) * _dot(xb, wu_ref[...])).astype(BF16)
        y = _dot(act, wd_ref[...])

        @pl.when(c == 0)
        def _():
            y_ref[...] = y

        @pl.when(c > 0)
        def _():
            y_ref[...] += y

    @pl.when(jnp.logical_not(used) & (c == 0))
    def _():
        y_ref[...] = jnp.zeros_like(y_ref)


def _experts(z, source, tile_expert, n_used, wgu, wd, ck=512):
    n_tiles, _, tm = source.shape
    d = z.shape[1]
    ff = wd.shape[1]
    fc = ff // ck
    grid_spec = pltpu.PrefetchScalarGridSpec(
        num_scalar_prefetch=2,
        grid=(n_tiles, fc),
        in_specs=[pl.BlockSpec((None, 1, tm), lambda i, c, te, nu: (i, 0, 0), memory_space=pltpu.SMEM),
                  pl.BlockSpec((None, 1, tm), lambda i, c, te, nu: (jnp.minimum(i + 1, n_tiles - 1), 0, 0),
                               memory_space=pltpu.SMEM),
                  pl.BlockSpec(memory_space=pl.ANY),
                  pl.BlockSpec((None, d, ck), lambda i, c, te, nu: (te[i], 0, c)),
                  pl.BlockSpec((None, d, ck), lambda i, c, te, nu: (te[i], 0, fc + c)),
                  pl.BlockSpec((None, ck, d), lambda i, c, te, nu: (te[i], c, 0))],
        out_specs=pl.BlockSpec((tm, d), lambda i, c, te, nu: (i, 0)),
        scratch_shapes=[pltpu.VMEM((2, tm, d), z.dtype), pltpu.VMEM((tm, d), BF16), pltpu.SemaphoreType.DMA((2,))],
    )
    return pl.pallas_call(
        _experts_kernel,
        grid_spec=grid_spec,
        out_shape=jax.ShapeDtypeStruct((n_tiles * tm, d), F32),
        compiler_params=_cparams("arbitrary", "arbitrary"),
        name="moe_experts",
    )(tile_expert, n_used, source, source, z, wgu, wgu, wd)


def _finish_routed_kernel(slots_ref, nxt_ref, h_ref, cols_ref, p_ref, gpost_ref, gple_ref, wproj_ref, wgate_ref,
                          ys_ref, o_ref, y_ref, sem):
    i = pl.program_id(0)
    tm = h_ref.shape[0]
    buf = i % 2

    def gather(rows_ref, b, start):
        def body(r, carry):
            for k in range(TOP_K):
                cp = _row_copy(ys_ref, rows_ref[k, r], y_ref.at[b, k], r, sem.at[b])
                cp.start() if start else cp.wait()
            return carry

        lax.fori_loop(0, tm, body, 0)

    @pl.when(i == 0)
    def _():
        gather(slots_ref, 0, True)

    gather(slots_ref, buf, False)

    @pl.when(i + 1 < pl.num_programs(0))
    def _():
        gather(nxt_ref, 1 - buf, True)

    cols = cols_ref[...]
    f = cols[:, 2:3] * y_ref[buf, 0] + cols[:, 3:4] * y_ref[buf, 1]
    h2 = h_ref[...] + _rms(f, gpost_ref[...])
    e = _mm(p_ref[...], wproj_ref[...])
    g = jax.nn.sigmoid(_mm(_rms(h2, gple_ref[...]), wgate_ref[...]))
    o_ref[...] = h2 + g * e


def _finish_routed(h, ys, slots, cols, p, gpost, gple, wproj, wgate, tm):
    m, d = h.shape
    n = m // tm
    row = pl.BlockSpec((tm, d), lambda i: (i, 0))
    return pl.pallas_call(
        _finish_routed_kernel,
        grid=(n,),
        in_specs=[pl.BlockSpec((TOP_K, tm), lambda i: (0, i), memory_space=pltpu.SMEM),
                  pl.BlockSpec((TOP_K, tm), lambda i: (0, jnp.minimum(i + 1, n - 1)), memory_space=pltpu.SMEM), row,
                  pl.BlockSpec((tm, LANES), lambda i: (i, 0)), pl.BlockSpec((tm, p.shape[1]), lambda i: (i, 0)),
                  _resident((1, d)), _resident((1, d)), _resident(wproj.shape), _resident(wgate.shape),
                  pl.BlockSpec(memory_space=pl.ANY)],
        out_specs=row,
        out_shape=jax.ShapeDtypeStruct((m, d), F32),
        scratch_shapes=[pltpu.VMEM((2, TOP_K, tm, d), F32), pltpu.SemaphoreType.DMA((2,))],
        compiler_params=_cparams("arbitrary"),
        name="finish_routed",
    )(slots, slots, h, cols, p, gpost, gple, wproj, wgate, ys)


def _tile(m, pref):
    return pref if m % pref == 0 else m


def kernel(x_prompt, x_sample, state_conv, state_lru, cache_k, cache_v, page_table, p_prompt, p_sample, norm_mix_pre, norm_mix_post, norm_ffn_pre, norm_ffn_post, norm_ple, rec_w_in, rec_conv_w, rec_conv_b, rec_w_a, rec_b_a, rec_w_x, rec_b_x, rec_lambda, rec_w_out, att_w_qkv, att_w_o, att_sb_bias, ffn_w_gu, ffn_w_down, moe_w_router, moe_b_router, moe_w_gu, moe_w_down, ple_w_proj, ple_w_gate):
    bsz, t, d = x_prompt.shape
    s = x_sample.shape[0]
    depth = norm_mix_pre.shape[0]
    mp = bsz * t
    tmp = _tile(mp, 512)
    vec = lambda a: a.reshape(1, -1)
    bf = lambda a: a.astype(BF16)
    hp = x_prompt.reshape(mp, d)
    hs = x_sample.reshape(s, d)
    keep = lambda a: a
    outs = {k: [] for k in ("conv_p", "lru_p", "k_p", "v_p", "conv_s", "lru_s", "k_s", "v_s")}
    for i in range(depth):
        j = i // 2
        if i % 2 == 0:
            def rec(cast):
                return (vec(norm_mix_pre[i]), cast(rec_w_in[j]), rec_conv_w[j], vec(rec_conv_b[j]), cast(rec_w_a[j]),
                        vec(rec_b_a[j]), cast(rec_w_x[j]), vec(rec_b_x[j]), vec(rec_lambda[j]), cast(rec_w_out[j]),
                        vec(norm_mix_post[i]))
            hp, conv_new, h_new = _rec_prompt(hp.reshape(bsz, t, d), *rec(bf), tt=_tile(t, 256))
            outs["conv_p"].append(conv_new)
            outs["lru_p"].append(h_new.reshape(bsz, d))
            hs, conv_new, h_new = _rec_step(hs, jnp.swapaxes(state_conv[j], 0, 1), state_lru[j], *rec(keep))
            outs["conv_s"].append(jnp.swapaxes(conv_new, 0, 1))
            outs["lru_s"].append(h_new)
            hp = hp.reshape(mp, d)
            fp = _ffn(hp, vec(norm_ffn_pre[i]), bf(ffn_w_gu[j]), bf(ffn_w_down[j]), tmp)
            fs = _ffn(hs, vec(norm_ffn_pre[i]), ffn_w_gu[j], ffn_w_down[j], s)
        else:
            hd = d // N_HEADS
            kp, vp, qb, kb, vb = _qkv(hp, vec(norm_mix_pre[i]), bf(att_w_qkv[j]), tmp)
            outs["k_p"].append(kp.reshape(bsz, t, N_HEADS, hd))
            outs["v_p"].append(vp.reshape(bsz, t, N_HEADS, hd))
            op = _sb_prompt(qb.reshape(bsz, t, d), kb.reshape(bsz, t, d), vb.reshape(bsz, t, d), att_sb_bias[j],
                            tq=_tile(t, 256))
            ks, vs, qs, _, _ = _qkv(hs, vec(norm_mix_pre[i]), att_w_qkv[j], s, q_dtype=F32)
            outs["k_s"].append(ks.reshape(s, 1, N_HEADS, hd))
            outs["v_s"].append(vs.reshape(s, 1, N_HEADS, hd))
            n_pages = page_table.shape[1]
            os_ = _sb_decode(qs.reshape(s, N_HEADS, hd, 1), jnp.transpose(cache_k[j], (0, 2, 3, 1)),
                             jnp.transpose(cache_v[j], (0, 2, 3, 1)), page_table, att_sb_bias[j].reshape(N_HEADS, 1),
                             pg=4 if n_pages % 4 == 0 else 1)
            wr = jnp.pad(moe_w_router[j], ((0, 0), (0, LANES - N_EXPERTS)))
            br = jnp.pad(moe_b_router[j], (0, LANES - N_EXPERTS), constant_values=NEG_BIG).reshape(1, LANES)
            norms = (vec(norm_mix_post[i]), vec(norm_ffn_pre[i]))
            hp, zp, cols, rows = _attn_out(hp, op.reshape(mp, d), bf(att_w_o[j]), *norms, bf(wr), br, tmp, route=True)
            hs, zs, cs = _attn_out(hs, os_.reshape(s, d), att_w_o[j], *norms, wr, br, s)
            wgu, wd = bf(moe_w_gu[j]), bf(moe_w_down[j])
            fs = _moe(zs, cs, wgu, wd, s)
            tme = _tile(mp, 512)
            n_tiles = (TOP_K * mp + N_EXPERTS * (tme - 1)) // tme
            slots, source, tile_expert, n_used = _route_plan(rows[0].astype(jnp.int32), rows[1].astype(jnp.int32),
                                                             N_EXPERTS, tme, n_tiles)
            ys = _experts(zp, source, tile_expert, n_used, wgu, wd)
        tail = (vec(norm_ffn_post[i]), vec(norm_ple[i]))
        if i % 2 == 0:
            hp = _finish(hp, fp, p_prompt[i].reshape(mp, -1), *tail, bf(ple_w_proj[i]), bf(ple_w_gate[i]), tmp)
        else:
            hp = _finish_routed(hp, ys, slots, cols, p_prompt[i].reshape(mp, -1), *tail, bf(ple_w_proj[i]),
                                bf(ple_w_gate[i]), tmp)
        hs = _finish(hs, fs, p_sample[i].reshape(s, -1), *tail, ple_w_proj[i], ple_w_gate[i], s)
    st = lambda k: jnp.stack(outs[k])
    return (hp.reshape(bsz, t, d), hs.reshape(s, 1, d), st("conv_p"), st("lru_p"), st("k_p"), st("v_p"),
            st("conv_s"), st("lru_s"), st("k_s"), st("v_s"))
```

```python
import functools
import math

import jax
import jax.numpy as jnp
from jax import lax
from jax.experimental import pallas as pl
from jax.experimental.pallas import tpu as pltpu

F32 = jnp.float32
BF16 = jnp.bfloat16

RMS_EPS = 1e-6
LRU_C = 8.0
N_HEADS = 16
N_LRU_BLOCKS = 8
CONV_W = 4
N_EXPERTS = 8
TOP_K = 2
LANES = 128
NEG_BIG = -1e30
VMEM_LIMIT = 56 * 1024 * 1024


def _cparams(*sem):
    return pltpu.CompilerParams(dimension_semantics=sem, vmem_limit_bytes=VMEM_LIMIT)


def _resident(shape):
    return pl.BlockSpec(shape, lambda *_: (0,) * len(shape), pipeline_mode=pl.Buffered(1))


def _dot(a, b):
    return jnp.dot(a, b, preferred_element_type=F32)


def _mm(x, w):
    if w.dtype == F32:
        return jnp.dot(x.astype(F32), w, precision=lax.Precision.HIGHEST, preferred_element_type=F32)
    return jnp.dot(x.astype(w.dtype), w, preferred_element_type=F32)


def _dot_nt(a, b):
    return lax.dot_general(a, b, (((1,), (1,)), ((), ())), preferred_element_type=F32)


def _rms(x, g):
    return x * lax.rsqrt(jnp.mean(x * x, axis=-1, keepdims=True) + RMS_EPS) * g


def _softplus(x):
    return jnp.maximum(x, 0.0) + jnp.log1p(jnp.exp(-jnp.abs(x)))


def _gelu(x):
    c = math.sqrt(2.0 / math.pi)
    return 0.5 * x * (1.0 + jnp.tanh(c * (x + 0.044715 * (x * x * x))))


def _silu(x):
    return x * jax.nn.sigmoid(x)


def _lru_gates(xc, wa_ref, ba, wx_ref, bx, lam):
    blk = xc.shape[1] // N_LRU_BLOCKS
    ra, rx = [], []
    for n in range(N_LRU_BLOCKS):
        xb = xc[:, n * blk:(n + 1) * blk]
        ra.append(_mm(xb, wa_ref[n]))
        rx.append(_mm(xb, wx_ref[n]))
    r = jax.nn.sigmoid(jnp.concatenate(ra, axis=1) + ba)
    ig = jax.nn.sigmoid(jnp.concatenate(rx, axis=1) + bx)
    log_a = -LRU_C * r * _softplus(-lam)
    a = jnp.exp(log_a)
    gx = jnp.sqrt(-jnp.tanh(log_a) * (a * a + 1.0)) * ig * xc
    return a, gx


def _rec_prompt_kernel(x_ref, gpre_ref, win_ref, cw_ref, cb_ref, wa_ref, ba_ref, wx_ref, bx_ref, lam_ref,
                       wout_ref, gpost_ref, h_ref, conv_ref, hlast_ref, tail_ref, hc_ref):
    tt, d = x_ref.shape

    @pl.when(pl.program_id(1) == 0)
    def _():
        tail_ref[...] = jnp.zeros_like(tail_ref)
        hc_ref[...] = jnp.zeros_like(hc_ref)

    x = x_ref[...]
    xn = _rms(x, gpre_ref[...]).astype(win_ref.dtype)
    gate = _gelu(_mm(xn, win_ref[:, :d]))
    xr = _mm(xn, win_ref[:, d:])
    xpad = jnp.concatenate([tail_ref[...], xr], axis=0)
    xc = cb_ref[...] + cw_ref[3:4, :] * xr
    for back in range(1, CONV_W):
        xc = xc + cw_ref[CONV_W - 1 - back:CONV_W - back, :] * xpad[8 - back:8 - back + tt]
    tail_ref[...] = xr[tt - 8:]
    a, b = _lru_gates(xc, wa_ref, ba_ref[...], wx_ref, bx_ref[...], lam_ref[...])
    row = lax.broadcasted_iota(jnp.int32, (tt, 1), 0)
    s = 1
    while s < tt:
        keep = row >= s
        a_sh = jnp.where(keep, pltpu.roll(a, s, 0), 1.0)
        b_sh = jnp.where(keep, pltpu.roll(b, s, 0), 0.0)
        b = a * b_sh + b
        a = a * a_sh
        s *= 2
    h = b + a * hc_ref[...]
    hc_ref[...] = h[tt - 1:]
    y = _mm(h * gate, wout_ref[...])
    h_ref[...] = x + _rms(y, gpost_ref[...])
    conv_ref[...] = xr[tt - (CONV_W - 1):]
    hlast_ref[...] = h[tt - 1:]


def _rec_prompt(x, gpre, win, cw, cb, wa, ba, wx, bx, lam, wout, gpost, tt):
    bsz, t, d = x.shape
    vec = _resident((1, d))
    return pl.pallas_call(
        _rec_prompt_kernel,
        grid=(bsz, t // tt),
        in_specs=[pl.BlockSpec((None, tt, d), lambda b, i: (b, i, 0)), vec, _resident((d, 2 * d)),
                  _resident((CONV_W, d)), vec, _resident(wa.shape), vec, _resident(wx.shape), vec, vec,
                  _resident((d, d)), vec],
        out_specs=[pl.BlockSpec((None, tt, d), lambda b, i: (b, i, 0)),
                   pl.BlockSpec((None, CONV_W - 1, d), lambda b, i: (b, 0, 0)),
                   pl.BlockSpec((None, 1, d), lambda b, i: (b, 0, 0))],
        out_shape=[jax.ShapeDtypeStruct((bsz, t, d), F32), jax.ShapeDtypeStruct((bsz, CONV_W - 1, d), F32),
                   jax.ShapeDtypeStruct((bsz, 1, d), F32)],
        scratch_shapes=[pltpu.VMEM((8, d), F32), pltpu.VMEM((1, d), F32)],
        compiler_params=_cparams("arbitrary", "arbitrary"),
        name="rec_prompt",
    )(x, gpre, win, cw, cb, wa, ba, wx, bx, lam, wout, gpost)


def _rec_step_kernel(x_ref, sc_ref, h0_ref, gpre_ref, win_ref, cw_ref, cb_ref, wa_ref, ba_ref, wx_ref, bx_ref,
                     lam_ref, wout_ref, gpost_ref, h_ref, conv_ref, hnew_ref):
    d = x_ref.shape[1]
    x = x_ref[...]
    xn = _rms(x, gpre_ref[...]).astype(win_ref.dtype)
    gate = _gelu(_mm(xn, win_ref[:, :d]))
    xr = _mm(xn, win_ref[:, d:])
    xc = cb_ref[...] + cw_ref[CONV_W - 1:CONV_W, :] * xr
    for k in range(CONV_W - 1):
        xc = xc + cw_ref[k:k + 1, :] * sc_ref[k]
    a, b = _lru_gates(xc, wa_ref, ba_ref[...], wx_ref, bx_ref[...], lam_ref[...])
    h = a * h0_ref[...] + b
    y = _mm(h * gate, wout_ref[...])
    h_ref[...] = x + _rms(y, gpost_ref[...])
    for k in range(CONV_W - 2):
        conv_ref[k] = sc_ref[k + 1]
    conv_ref[CONV_W - 2] = xr
    hnew_ref[...] = h


def _rec_step(x, sc, h0, gpre, win, cw, cb, wa, ba, wx, bx, lam, wout, gpost):
    rows, d = x.shape
    return pl.pallas_call(
        _rec_step_kernel,
        out_shape=[jax.ShapeDtypeStruct((rows, d), F32), jax.ShapeDtypeStruct((CONV_W - 1, rows, d), F32),
                   jax.ShapeDtypeStruct((rows, d), F32)],
        compiler_params=pltpu.CompilerParams(vmem_limit_bytes=VMEM_LIMIT),
        name="rec_step",
    )(x, sc, h0, gpre, win, cw, cb, wa, ba, wx, bx, lam, wout, gpost)


def _ffn_kernel(h_ref, g_ref, wgu_ref, wd_ref, f_ref, *, ck):
    ff = wd_ref.shape[0]
    z = _rms(h_ref[...], g_ref[...]).astype(wgu_ref.dtype)
    acc = jnp.zeros(f_ref.shape, F32)
    for c in range(ff // ck):
        g = _mm(z, wgu_ref[:, c * ck:(c + 1) * ck])
        u = _mm(z, wgu_ref[:, ff + c * ck:ff + (c + 1) * ck])
        acc = acc + _mm(_silu(g) * u, wd_ref[c * ck:(c + 1) * ck, :])
    f_ref[...] = acc


def _ffn(h, g, wgu, wd, tm, ck=512):
    m, d = h.shape
    row = pl.BlockSpec((tm, d), lambda i: (i, 0))
    return pl.pallas_call(
        functools.partial(_ffn_kernel, ck=ck),
        grid=(m // tm,),
        in_specs=[row, _resident((1, d)), _resident(wgu.shape), _resident(wd.shape)],
        out_specs=row,
        out_shape=jax.ShapeDtypeStruct((m, d), F32),
        compiler_params=_cparams("arbitrary"),
        name="ffn_dense",
    )(h, g, wgu, wd)


def _finish_kernel(h_ref, f_ref, p_ref, gpost_ref, gple_ref, wproj_ref, wgate_ref, o_ref):
    h2 = h_ref[...] + _rms(f_ref[...], gpost_ref[...])
    e = _mm(p_ref[...], wproj_ref[...])
    g = jax.nn.sigmoid(_mm(_rms(h2, gple_ref[...]), wgate_ref[...]))
    o_ref[...] = h2 + g * e


def _finish(h, f, p, gpost, gple, wproj, wgate, tm):
    m, d = h.shape
    row = pl.BlockSpec((tm, d), lambda i: (i, 0))
    return pl.pallas_call(
        _finish_kernel,
        grid=(m // tm,),
        in_specs=[row, row, pl.BlockSpec((tm, p.shape[1]), lambda i: (i, 0)), _resident((1, d)), _resident((1, d)),
                  _resident(wproj.shape), _resident(wgate.shape)],
        out_specs=row,
        out_shape=jax.ShapeDtypeStruct((m, d), F32),
        compiler_params=_cparams("arbitrary"),
        name="finish",
    )(h, f, p, gpost, gple, wproj, wgate)


def _qkv_kernel(h_ref, g_ref, w_ref, k_ref, v_ref, qb_ref, kb_ref, vb_ref):
    d = h_ref.shape[1]
    xn = _rms(h_ref[...], g_ref[...]).astype(w_ref.dtype)
    scale = (d // N_HEADS) ** -0.5
    qb_ref[...] = (_mm(xn, w_ref[:, :d]) * scale).astype(qb_ref.dtype)
    k = _mm(xn, w_ref[:, d:2 * d])
    v = _mm(xn, w_ref[:, 2 * d:])
    k_ref[...] = k
    v_ref[...] = v
    kb_ref[...] = k.astype(BF16)
    vb_ref[...] = v.astype(BF16)


def _qkv(h, g, w, tm, q_dtype=BF16):
    m, d = h.shape
    row = pl.BlockSpec((tm, d), lambda i: (i, 0))
    return pl.pallas_call(
        _qkv_kernel,
        grid=(m // tm,),
        in_specs=[row, _resident((1, d)), _resident(w.shape)],
        out_specs=[row] * 5,
        out_shape=[jax.ShapeDtypeStruct((m, d), F32)] * 2 + [jax.ShapeDtypeStruct((m, d), q_dtype)]
        + [jax.ShapeDtypeStruct((m, d), BF16)] * 2,
        compiler_params=_cparams("arbitrary"),
        name="qkv",
    )(h, g, w)


def _sb_weights(z, sp, ls, tri, carry):
    hi = ls.astype(BF16)
    lo = (ls - hi.astype(F32)).astype(BF16)
    later = _dot(hi, tri) + _dot(lo, tri)
    return jnp.exp((z - sp) + later + carry)


def _sb_prompt_kernel(bias_ref, q_ref, k_ref, v_ref, o_ref, qs_ref, w_ref, acc_ref, carry_ref, *, hps):
    tq, width = q_ref.shape
    hd = width // hps
    grp = pl.program_id(1)
    i = pl.program_id(2)
    lane = lax.broadcasted_iota(jnp.int32, (tq, width), 1)
    q = q_ref[...]
    for h in range(hps):
        qs_ref[h * tq:(h + 1) * tq, :] = jnp.where((lane >= h * hd) & (lane < (h + 1) * hd), q, jnp.zeros_like(q))
    acc_ref[...] = jnp.zeros_like(acc_ref)
    carry_ref[...] = jnp.zeros_like(carry_ref)
    r2 = lax.broadcasted_iota(jnp.int32, (tq, tq), 0)
    c2 = lax.broadcasted_iota(jnp.int32, (tq, tq), 1)
    neg_tri = jnp.where(r2 > c2, -1.0, 0.0).astype(BF16)
    causal = jnp.concatenate([c2 < r2] * hps, axis=0)
    bias = jnp.concatenate([jnp.full((tq, 1), bias_ref[hps * grp + h], F32) for h in range(hps)], axis=0)

    def weights(j, diag):
        z = _dot_nt(qs_ref[...], k_ref[pl.ds(pl.multiple_of(j * tq, tq), tq), :]) + bias
        sp = jnp.maximum(z, 0.0) + jnp.log(1.0 + jnp.exp(-jnp.abs(z)))
        if diag:
            sp = jnp.where(causal, sp, 0.0)
        later = _dot(sp.astype(BF16), neg_tri)
        w = jnp.exp((z - sp) + later + carry_ref[...])
        if diag:
            w = jnp.where(causal, w, 0.0)
        w_ref[...] = w.astype(BF16)
        carry_ref[...] -= jnp.sum(sp, axis=-1, keepdims=True)

    def values(j):
        acc_ref[...] += _dot(w_ref[...], v_ref[pl.ds(pl.multiple_of(j * tq, tq), tq), :])

    weights(i, True)

    def body(jj, c):
        j = i - 1 - jj
        values(j + 1)
        weights(j, False)
        return c

    lax.fori_loop(0, i, body, 0)
    values(0)
    out = acc_ref[0:tq, :]
    for h in range(1, hps):
        out = jnp.where(lane >= h * hd, acc_ref[h * tq:(h + 1) * tq, :], out)
    o_ref[...] = out.astype(o_ref.dtype)


def _sb_prompt(qb, kb, vb, bias, tq, hps=4):
    bsz, t, d = qb.shape
    width = hps * (d // N_HEADS)
    qspec = pl.BlockSpec((None, tq, width), lambda b, g, i: (b, i, g))
    kvspec = pl.BlockSpec((None, t, width), lambda b, g, i: (b, 0, g))
    return pl.pallas_call(
        functools.partial(_sb_prompt_kernel, hps=hps),
        grid=(bsz, d // width, t // tq),
        in_specs=[pl.BlockSpec(memory_space=pltpu.SMEM), qspec, kvspec, kvspec],
        out_specs=qspec,
        out_shape=jax.ShapeDtypeStruct((bsz, t, d), BF16),
        scratch_shapes=[pltpu.VMEM((hps * tq, width), BF16), pltpu.VMEM((hps * tq, tq), BF16),
                        pltpu.VMEM((hps * tq, width), F32),
                        pltpu.VMEM((hps * tq, 1), F32)],
        compiler_params=_cparams("arbitrary", "arbitrary", "arbitrary"),
        name="sb_prompt",
    )(bias, qb, kb, vb)


def _sb_decode_kernel(pt_ref, q_ref, bias_ref, *refs, pg):
    del pt_ref
    k_refs, v_refs = refs[:pg], refs[pg:2 * pg]
    o_ref, qb_ref, acc_ref, carry_ref = refs[2 * pg:]
    nh, _, page = k_refs[0].shape
    g = pl.program_id(1)

    @pl.when(g == 0)
    def _():
        acc_ref[...] = jnp.zeros_like(acc_ref)
        carry_ref[...] = jnp.zeros_like(carry_ref)
        qb_ref[...] = jnp.broadcast_to(q_ref[...], qb_ref.shape)

    row = lax.broadcasted_iota(jnp.int32, (page, page), 0)
    col = lax.broadcasted_iota(jnp.int32, (page, page), 1)
    tri = jnp.where(row > col, 1.0, 0.0).astype(BF16)
    z = jnp.concatenate([jnp.sum(qb_ref[h] * k_refs[r][h], axis=0, keepdims=True)
                         for r in range(pg) for h in range(nh)], axis=0)
    z = z + jnp.concatenate([bias_ref[...]] * pg, axis=0)
    sp = _softplus(z)
    ls = -sp
    total = jnp.sum(ls, axis=-1, keepdims=True)
    carries = [carry_ref[...]]
    for r in range(pg):
        carries.append(carries[-1] + total[r * nh:(r + 1) * nh])
    carry_ref[...] = carries[pg]
    w = _sb_weights(z, sp, ls, tri, jnp.concatenate(carries[:pg], axis=0))
    for h in range(nh):
        part = w[h:h + 1, :] * v_refs[0][h]
        for r in range(1, pg):
            part = part + w[r * nh + h:r * nh + h + 1, :] * v_refs[r][h]
        acc_ref[h] += part

    @pl.when(g == pl.num_programs(1) - 1)
    def _():
        o_ref[...] = jnp.sum(acc_ref[...], axis=-1)


def _sb_decode(q, cache_k, cache_v, page_table, bias, pg):
    s, nh, hd, _ = q.shape
    n_pages = page_table.shape[1]
    page = cache_k.shape[3]

    def page_spec(r):
        return pl.BlockSpec((None, nh, hd, page), lambda b, g, pt: (pt[b, n_pages - 1 - (g * pg + r)], 0, 0, 0))

    grid_spec = pltpu.PrefetchScalarGridSpec(
        num_scalar_prefetch=1,
        grid=(s, n_pages // pg),
        in_specs=[pl.BlockSpec((None, nh, hd, 1), lambda b, g, pt: (b, 0, 0, 0)),
                  pl.BlockSpec((nh, 1), lambda b, g, pt: (0, 0))] + [page_spec(r) for r in range(pg)] * 2,
        out_specs=pl.BlockSpec((None, nh, hd), lambda b, g, pt: (b, 0, 0)),
        scratch_shapes=[pltpu.VMEM((nh, hd, page), F32), pltpu.VMEM((nh, hd, page), F32), pltpu.VMEM((nh, 1), F32)],
    )
    return pl.pallas_call(
        functools.partial(_sb_decode_kernel, pg=pg),
        grid_spec=grid_spec,
        out_shape=jax.ShapeDtypeStruct((s, nh, hd), F32),
        compiler_params=_cparams("arbitrary", "arbitrary"),
        name="sb_decode",
    )(page_table, q, bias, *([cache_k] * pg), *([cache_v] * pg))


def _mix_residual(h_ref, o_ref, wo_ref, gpost_ref, gffn_ref):
    h1 = h_ref[...] + _rms(_mm(o_ref[...], wo_ref[...]), gpost_ref[...])
    return h1, _rms(h1, gffn_ref[...])


def _top2(logits):
    idx = lax.broadcasted_iota(jnp.int32, logits.shape, 1)
    m1 = jnp.max(logits, axis=-1, keepdims=True)
    i1 = jnp.min(jnp.where(logits == m1, idx, LANES), axis=-1, keepdims=True)
    rest = jnp.where(idx == i1, NEG_BIG, logits)
    m2 = jnp.max(rest, axis=-1, keepdims=True)
    i2 = jnp.min(jnp.where(rest == m2, idx, LANES), axis=-1, keepdims=True)
    e = jnp.exp(m2 - m1)
    w1 = 1.0 / (1.0 + e)
    return idx, i1, i2, w1, e * w1


def _attn_out_kernel(h_ref, o_ref, wo_ref, gpost_ref, gffn_ref, wr_ref, br_ref, h1_ref, z_ref, comb_ref):
    h1, z = _mix_residual(h_ref, o_ref, wo_ref, gpost_ref, gffn_ref)
    h1_ref[...] = h1
    z_ref[...] = z.astype(z_ref.dtype)
    idx, i1, i2, w1, w2 = _top2(_mm(z, wr_ref[...]) + br_ref[...])
    comb_ref[...] = jnp.where(idx == i1, w1, 0.0) + jnp.where(idx == i2, w2, 0.0)


def _attn_out_route_kernel(h_ref, o_ref, wo_ref, gpost_ref, gffn_ref, wr_ref, br_ref, h1_ref, z_ref, cols_ref,
                           rows_ref):
    h1, z = _mix_residual(h_ref, o_ref, wo_ref, gpost_ref, gffn_ref)
    h1_ref[...] = h1
    z_ref[...] = z
    idx, i1, i2, w1, w2 = _top2(_mm(z, wr_ref[...]) + br_ref[...])
    cols = jnp.where(idx == 0, i1.astype(F32), jnp.where(idx == 1, i2.astype(F32),
                                                         jnp.where(idx == 2, w1, jnp.where(idx == 3, w2, 0.0))))
    cols_ref[...] = cols
    pick = (lax.broadcasted_iota(jnp.int32, (8, LANES), 0) == lax.broadcasted_iota(jnp.int32, (8, LANES), 1))
    rows_ref[...] = lax.dot_general(pick.astype(F32), cols, (((1,), (1,)), ((), ())),
                                    precision=lax.Precision.HIGHEST, preferred_element_type=F32)


def _attn_out(h, o, wo, gpost, gffn, wr, br, tm, route=False):
    m, d = h.shape
    row = pl.BlockSpec((tm, d), lambda i: (i, 0))
    comb = pl.BlockSpec((tm, LANES), lambda i: (i, 0))
    in_specs = [row, row, _resident(wo.shape), _resident((1, d)), _resident((1, d)), _resident(wr.shape),
                _resident(br.shape)]
    if route:
        return pl.pallas_call(
            _attn_out_route_kernel,
            grid=(m // tm,),
            in_specs=in_specs,
            out_specs=[row, row, comb, pl.BlockSpec((8, tm), lambda i: (0, i))],
            out_shape=[jax.ShapeDtypeStruct((m, d), F32), jax.ShapeDtypeStruct((m, d), F32),
                       jax.ShapeDtypeStruct((m, LANES), F32), jax.ShapeDtypeStruct((8, m), F32)],
            compiler_params=_cparams("arbitrary"),
            name="attn_out_route",
        )(h, o, wo, gpost, gffn, wr, br)
    return pl.pallas_call(
        _attn_out_kernel,
        grid=(m // tm,),
        in_specs=in_specs,
        out_specs=[row, row, comb],
        out_shape=[jax.ShapeDtypeStruct((m, d), F32), jax.ShapeDtypeStruct((m, d), BF16),
                   jax.ShapeDtypeStruct((m, LANES), F32)],
        compiler_params=_cparams("arbitrary"),
        name="attn_out",
    )(h, o, wo, gpost, gffn, wr, br)


def _moe_kernel(z_ref, comb_ref, wg_ref, wu_ref, wd_ref, f_ref):
    e = pl.program_id(1)
    c = pl.program_id(2)

    @pl.when((e == 0) & (c == 0))
    def _():
        f_ref[...] = jnp.zeros_like(f_ref)

    z = z_ref[...]
    act = (_silu(_dot(z, wg_ref[...])) * _dot(z, wu_ref[...])).astype(BF16)
    lane = lax.broadcasted_iota(jnp.int32, comb_ref.shape, 1)
    ce = jnp.sum(jnp.where(lane == e, comb_ref[...], 0.0), axis=-1, keepdims=True)
    f_ref[...] += ce * _dot(act, wd_ref[...])


def _moe(z, comb, wgu, wd, tm, ck=512):
    m, d = z.shape
    n_e, ff, _ = wd.shape
    fc = ff // ck
    return pl.pallas_call(
        _moe_kernel,
        grid=(m // tm, n_e, fc),
        in_specs=[pl.BlockSpec((tm, d), lambda i, e, c: (i, 0)), pl.BlockSpec((tm, LANES), lambda i, e, c: (i, 0)),
                  pl.BlockSpec((None, d, ck), lambda i, e, c: (e, 0, c)),
                  pl.BlockSpec((None, d, ck), lambda i, e, c: (e, 0, fc + c)),
                  pl.BlockSpec((None, ck, d), lambda i, e, c: (e, c, 0))],
        out_specs=pl.BlockSpec((tm, d), lambda i, e, c: (i, 0)),
        out_shape=jax.ShapeDtypeStruct((m, d), F32),
        compiler_params=_cparams("arbitrary", "arbitrary", "arbitrary"),
        name="moe",
    )(z, comb, wgu, wgu, wd)


def _route_plan(i1, i2, n_e, tm, n_tiles):
    m = i1.shape[0]
    experts = jnp.arange(n_e, dtype=jnp.int32)[None, :]
    hit = (experts == i1[:, None]).astype(jnp.int32) + (experts == i2[:, None]).astype(jnp.int32)
    upto = jnp.cumsum(hit, axis=0)
    before = upto - hit
    padded = ((upto[-1] + tm - 1) // tm) * tm
    ends = jnp.cumsum(padded)
    starts = ends - padded
    slot1 = starts[i1] + jnp.take_along_axis(before, i1[:, None], axis=1)[:, 0]
    slot2 = starts[i2] + jnp.take_along_axis(before, i2[:, None], axis=1)[:, 0]
    slots = jnp.stack([slot1, slot2]).astype(jnp.int32)
    token = jnp.tile(jnp.arange(m, dtype=jnp.int32), 2)
    source = jnp.zeros((n_tiles * tm,), jnp.int32).at[slots.reshape(-1)].set(token).reshape(n_tiles, 1, tm)
    tile_start = jnp.arange(n_tiles, dtype=jnp.int32) * tm
    tile_expert = jnp.minimum(jnp.sum(tile_start[:, None] >= ends[None, :], axis=1), n_e - 1).astype(jnp.int32)
    return slots, source, tile_expert, (ends[-1:] // tm).astype(jnp.int32)


GATHER_UNROLL = 8


def _start_row_gather(src_ref, row_of, dst_ref, sem):
    rows = dst_ref.shape[0]

    def body(r, carry):
        pltpu.make_async_copy(src_ref.at[pl.ds(row_of(r), 1), :], dst_ref.at[pl.ds(r, 1), :], sem).start()
        return carry

    lax.fori_loop(0, rows, body, 0, unroll=GATHER_UNROLL)


def _wait_row_gather(src_ref, dst_ref, sem):
    pltpu.make_async_copy(src_ref.at[pl.ds(0, dst_ref.shape[0]), :], dst_ref, sem).wait()


def _experts_kernel(te_ref, nu_ref, src_ref, nxt_ref, z_ref, wg_ref, wu_ref, wd_ref, y_ref, xg_ref, xb_ref, sem):
    del te_ref
    i = pl.program_id(0)
    c = pl.program_id(1)
    n_used = nu_ref[0]
    used = i < n_used

    @pl.when(used & (c == 0))
    def _():
        buf = i % 2

        @pl.when(i == 0)
        def _():
            _start_row_gather(z_ref, lambda r: src_ref[0, r], xg_ref.at[0], sem.at[0])

        _wait_row_gather(z_ref, xg_ref.at[buf], sem.at[buf])
        xb_ref[...] = xg_ref[buf].astype(BF16)

        @pl.when(i + 1 < n_used)
        def _():
            _start_row_gather(z_ref, lambda r: nxt_ref[0, r], xg_ref.at[1 - buf], sem.at[1 - buf])

    @pl.when(used)
    def _():
        xb = xb_ref[...]
        act = (_silu(_dot(xb, wg_ref[...])) * _dot(xb, wu_ref[...])).astype(BF16)
        y = _dot(act, wd_ref[...])

        @pl.when(c == 0)
        def _():
            y_ref[...] = y

        @pl.when(c > 0)
        def _():
            y_ref[...] += y

    @pl.when(jnp.logical_not(used) & (c == 0))
    def _():
        y_ref[...] = jnp.zeros_like(y_ref)


def _experts(z, source, tile_expert, n_used, wgu, wd, ck=512):
    n_tiles, _, tm = source.shape
    d = z.shape[1]
    ff = wd.shape[1]
    fc = ff // ck
    grid_spec = pltpu.PrefetchScalarGridSpec(
        num_scalar_prefetch=2,
        grid=(n_tiles, fc),
        in_specs=[pl.BlockSpec((None, 1, tm), lambda i, c, te, nu: (i, 0, 0), memory_space=pltpu.SMEM),
                  pl.BlockSpec((None, 1, tm), lambda i, c, te, nu: (jnp.minimum(i + 1, n_tiles - 1), 0, 0),
                               memory_space=pltpu.SMEM),
                  pl.BlockSpec(memory_space=pl.ANY),
                  pl.BlockSpec((None, d, ck), lambda i, c, te, nu: (te[i], 0, c)),
                  pl.BlockSpec((None, d, ck), lambda i, c, te, nu: (te[i], 0, fc + c)),
                  pl.BlockSpec((None, ck, d), lambda i, c, te, nu: (te[i], c, 0))],
        out_specs=pl.BlockSpec((tm, d), lambda i, c, te, nu: (i, 0)),
        scratch_shapes=[pltpu.VMEM((2, tm, d), z.dtype), pltpu.VMEM((tm, d), BF16), pltpu.SemaphoreType.DMA((2,))],
    )
    return pl.pallas_call(
        _experts_kernel,
        grid_spec=grid_spec,
        out_shape=jax.ShapeDtypeStruct((n_tiles * tm, d), F32),
        compiler_params=_cparams("arbitrary", "arbitrary"),
        name="moe_experts",
    )(tile_expert, n_used, source, source, z, wgu, wgu, wd)


def _finish_routed_kernel(slots_ref, nxt_ref, h_ref, cols_ref, p_ref, gpost_ref, gple_ref, wproj_ref, wgate_ref,
                          ys_ref, o_ref, y_ref, sem):
    i = pl.program_id(0)
    buf = i % 2

    def start(rows_ref, b):
        for k in range(TOP_K):
            _start_row_gather(ys_ref, lambda r, k=k: rows_ref[k, r], y_ref.at[b, k], sem.at[b])

    @pl.when(i == 0)
    def _():
        start(slots_ref, 0)

    for k in range(TOP_K):
        _wait_row_gather(ys_ref, y_ref.at[buf, k], sem.at[buf])

    @pl.when(i + 1 < pl.num_programs(0))
    def _():
        start(nxt_ref, 1 - buf)

    cols = cols_ref[...]
    f = cols[:, 2:3] * y_ref[buf, 0] + cols[:, 3:4] * y_ref[buf, 1]
    h2 = h_ref[...] + _rms(f, gpost_ref[...])
    e = _mm(p_ref[...], wproj_ref[...])
    g = jax.nn.sigmoid(_mm(_rms(h2, gple_ref[...]), wgate_ref[...]))
    o_ref[...] = h2 + g * e


def _finish_routed(h, ys, slots, cols, p, gpost, gple, wproj, wgate, tm):
    m, d = h.shape
    n = m // tm
    row = pl.BlockSpec((tm, d), lambda i: (i, 0))
    return pl.pallas_call(
        _finish_routed_kernel,
        grid=(n,),
        in_specs=[pl.BlockSpec((TOP_K, tm), lambda i: (0, i), memory_space=pltpu.SMEM),
                  pl.BlockSpec((TOP_K, tm), lambda i: (0, jnp.minimum(i + 1, n - 1)), memory_space=pltpu.SMEM), row,
                  pl.BlockSpec((tm, LANES), lambda i: (i, 0)), pl.BlockSpec((tm, p.shape[1]), lambda i: (i, 0)),
                  _resident((1, d)), _resident((1, d)), _resident(wproj.shape), _resident(wgate.shape),
                  pl.BlockSpec(memory_space=pl.ANY)],
        out_specs=row,
        out_shape=jax.ShapeDtypeStruct((m, d), F32),
        scratch_shapes=[pltpu.VMEM((2, TOP_K, tm, d), F32), pltpu.SemaphoreType.DMA((2,))],
        compiler_params=_cparams("arbitrary"),
        name="finish_routed",
    )(slots, slots, h, cols, p, gpost, gple, wproj, wgate, ys)


def _tile(m, pref):
    return pref if m % pref == 0 else m


def kernel(x_prompt, x_sample, state_conv, state_lru, cache_k, cache_v, page_table, p_prompt, p_sample, norm_mix_pre, norm_mix_post, norm_ffn_pre, norm_ffn_post, norm_ple, rec_w_in, rec_conv_w, rec_conv_b, rec_w_a, rec_b_a, rec_w_x, rec_b_x, rec_lambda, rec_w_out, att_w_qkv, att_w_o, att_sb_bias, ffn_w_gu, ffn_w_down, moe_w_router, moe_b_router, moe_w_gu, moe_w_down, ple_w_proj, ple_w_gate):
    bsz, t, d = x_prompt.shape
    s = x_sample.shape[0]
    depth = norm_mix_pre.shape[0]
    mp = bsz * t
    tmp = _tile(mp, 512)
    vec = lambda a: a.reshape(1, -1)
    bf = lambda a: a.astype(BF16)
    hp = x_prompt.reshape(mp, d)
    hs = x_sample.reshape(s, d)
    keep = lambda a: a
    outs = {k: [] for k in ("conv_p", "lru_p", "k_p", "v_p", "conv_s", "lru_s", "k_s", "v_s")}
    for i in range(depth):
        j = i // 2
        if i % 2 == 0:
            def rec(cast):
                return (vec(norm_mix_pre[i]), cast(rec_w_in[j]), rec_conv_w[j], vec(rec_conv_b[j]), cast(rec_w_a[j]),
                        vec(rec_b_a[j]), cast(rec_w_x[j]), vec(rec_b_x[j]), vec(rec_lambda[j]), cast(rec_w_out[j]),
                        vec(norm_mix_post[i]))
            hp, conv_new, h_new = _rec_prompt(hp.reshape(bsz, t, d), *rec(bf), tt=_tile(t, 256))
            outs["conv_p"].append(conv_new)
            outs["lru_p"].append(h_new.reshape(bsz, d))
            hs, conv_new, h_new = _rec_step(hs, jnp.swapaxes(state_conv[j], 0, 1), state_lru[j], *rec(keep))
            outs["conv_s"].append(jnp.swapaxes(conv_new, 0, 1))
            outs["lru_s"].append(h_new)
            hp = hp.reshape(mp, d)
            fp = _ffn(hp, vec(norm_ffn_pre[i]), bf(ffn_w_gu[j]), bf(ffn_w_down[j]), tmp)
            fs = _ffn(hs, vec(norm_ffn_pre[i]), ffn_w_gu[j], ffn_w_down[j], s)
        else:
            hd = d // N_HEADS
            kp, vp, qb, kb, vb = _qkv(hp, vec(norm_mix_pre[i]), bf(att_w_qkv[j]), tmp)
            outs["k_p"].append(kp.reshape(bsz, t, N_HEADS, hd))
            outs["v_p"].append(vp.reshape(bsz, t, N_HEADS, hd))
            op = _sb_prompt(qb.reshape(bsz, t, d), kb.reshape(bsz, t, d), vb.reshape(bsz, t, d), att_sb_bias[j],
                            tq=_tile(t, 256))
            ks, vs, qs, _, _ = _qkv(hs, vec(norm_mix_pre[i]), att_w_qkv[j], s, q_dtype=F32)
            outs["k_s"].append(ks.reshape(s, 1, N_HEADS, hd))
            outs["v_s"].append(vs.reshape(s, 1, N_HEADS, hd))
            n_pages = page_table.shape[1]
            os_ = _sb_decode(qs.reshape(s, N_HEADS, hd, 1), jnp.transpose(cache_k[j], (0, 2, 3, 1)),
                             jnp.transpose(cache_v[j], (0, 2, 3, 1)), page_table, att_sb_bias[j].reshape(N_HEADS, 1),
                             pg=8 if n_pages % 8 == 0 else 1)
            wr = jnp.pad(moe_w_router[j], ((0, 0), (0, LANES - N_EXPERTS)))
            br = jnp.pad(moe_b_router[j], (0, LANES - N_EXPERTS), constant_values=NEG_BIG).reshape(1, LANES)
            norms = (vec(norm_mix_post[i]), vec(norm_ffn_pre[i]))
            hp, zp, cols, rows = _attn_out(hp, op.reshape(mp, d), bf(att_w_o[j]), *norms, bf(wr), br, tmp, route=True)
            hs, zs, cs = _attn_out(hs, os_.reshape(s, d), att_w_o[j], *norms, wr, br, s)
            wgu, wd = bf(moe_w_gu[j]), bf(moe_w_down[j])
            fs = _moe(zs, cs, wgu, wd, s)
            tme = _tile(mp, 512)
            n_tiles = (TOP_K * mp + N_EXPERTS * (tme - 1)) // tme
            slots, source, tile_expert, n_used = _route_plan(rows[0].astype(jnp.int32), rows[1].astype(jnp.int32),
                                                             N_EXPERTS, tme, n_tiles)
            ys = _experts(zp, source, tile_expert, n_used, wgu, wd)
        tail = (vec(norm_ffn_post[i]), vec(norm_ple[i]))
        if i % 2 == 0:
            hp = _finish(hp, fp, p_prompt[i].reshape(mp, -1), *tail, bf(ple_w_proj[i]), bf(ple_w_gate[i]), tmp)
        else:
            hp = _finish_routed(hp, ys, slots, cols, p_prompt[i].reshape(mp, -1), *tail, bf(ple_w_proj[i]),
                                bf(ple_w_gate[i]), tmp)
        hs = _finish(hs, fs, p_sample[i].reshape(s, -1), *tail, ple_w_proj[i], ple_w_gate[i], s)
    st = lambda k: jnp.stack(outs[k])
    return (hp.reshape(bsz, t, d), hs.reshape(s, 1, d), st("conv_p"), st("lru_p"), st("k_p"), st("v_p"),
            st("conv_s"), st("lru_s"), st("k_s"), st("v_s"))
```

```python
import functools
import math

import jax
import jax.numpy as jnp
from jax import lax
from jax.experimental import pallas as pl
from jax.experimental.pallas import tpu as pltpu

F32 = jnp.float32
BF16 = jnp.bfloat16

RMS_EPS = 1e-6
LRU_C = 8.0
N_HEADS = 16
N_LRU_BLOCKS = 8
CONV_W = 4
N_EXPERTS = 8
TOP_K = 2
LANES = 128
NEG_BIG = -1e30
VMEM_LIMIT = 56 * 1024 * 1024


def _cparams(*sem):
    return pltpu.CompilerParams(dimension_semantics=sem, vmem_limit_bytes=VMEM_LIMIT)


def _resident(shape):
    return pl.BlockSpec(shape, lambda *_: (0,) * len(shape), pipeline_mode=pl.Buffered(1))


def _dot(a, b):
    return jnp.dot(a, b, preferred_element_type=F32)


def _mm(x, w):
    if w.dtype == F32:
        return jnp.dot(x.astype(F32), w, precision=lax.Precision.HIGHEST, preferred_element_type=F32)
    return jnp.dot(x.astype(w.dtype), w, preferred_element_type=F32)


def _dot_nt(a, b):
    return lax.dot_general(a, b, (((1,), (1,)), ((), ())), preferred_element_type=F32)


def _rms(x, g):
    return x * lax.rsqrt(jnp.mean(x * x, axis=-1, keepdims=True) + RMS_EPS) * g


def _softplus(x):
    return jnp.maximum(x, 0.0) + jnp.log1p(jnp.exp(-jnp.abs(x)))


def _gelu(x):
    c = math.sqrt(2.0 / math.pi)
    return 0.5 * x * (1.0 + jnp.tanh(c * (x + 0.044715 * (x * x * x))))


def _silu(x):
    return x * jax.nn.sigmoid(x)


def _lru_gates(xc, wa_ref, ba, wx_ref, bx, lam):
    blk = xc.shape[1] // N_LRU_BLOCKS
    ra, rx = [], []
    for n in range(N_LRU_BLOCKS):
        xb = xc[:, n * blk:(n + 1) * blk]
        ra.append(_mm(xb, wa_ref[n]))
        rx.append(_mm(xb, wx_ref[n]))
    r = jax.nn.sigmoid(jnp.concatenate(ra, axis=1) + ba)
    ig = jax.nn.sigmoid(jnp.concatenate(rx, axis=1) + bx)
    log_a = -LRU_C * r * _softplus(-lam)
    a = jnp.exp(log_a)
    gx = jnp.sqrt(-jnp.tanh(log_a) * (a * a + 1.0)) * ig * xc
    return a, gx


def _rec_prompt_kernel(x_ref, gpre_ref, win_ref, cw_ref, cb_ref, wa_ref, ba_ref, wx_ref, bx_ref, lam_ref,
                       wout_ref, gpost_ref, h_ref, conv_ref, hlast_ref, tail_ref, hc_ref):
    tt, d = x_ref.shape

    @pl.when(pl.program_id(1) == 0)
    def _():
        tail_ref[...] = jnp.zeros_like(tail_ref)
        hc_ref[...] = jnp.zeros_like(hc_ref)

    x = x_ref[...]
    xn = _rms(x, gpre_ref[...]).astype(win_ref.dtype)
    gate = _gelu(_mm(xn, win_ref[:, :d]))
    xr = _mm(xn, win_ref[:, d:])
    xpad = jnp.concatenate([tail_ref[...], xr], axis=0)
    xc = cb_ref[...] + cw_ref[3:4, :] * xr
    for back in range(1, CONV_W):
        xc = xc + cw_ref[CONV_W - 1 - back:CONV_W - back, :] * xpad[8 - back:8 - back + tt]
    tail_ref[...] = xr[tt - 8:]
    a, b = _lru_gates(xc, wa_ref, ba_ref[...], wx_ref, bx_ref[...], lam_ref[...])
    row = lax.broadcasted_iota(jnp.int32, (tt, 1), 0)
    s = 1
    while s < tt:
        keep = row >= s
        a_sh = jnp.where(keep, pltpu.roll(a, s, 0), 1.0)
        b_sh = jnp.where(keep, pltpu.roll(b, s, 0), 0.0)
        b = a * b_sh + b
        a = a * a_sh
        s *= 2
    h = b + a * hc_ref[...]
    hc_ref[...] = h[tt - 1:]
    y = _mm(h * gate, wout_ref[...])
    h_ref[...] = x + _rms(y, gpost_ref[...])
    conv_ref[...] = xr[tt - (CONV_W - 1):]
    hlast_ref[...] = h[tt - 1:]


def _rec_prompt(x, gpre, win, cw, cb, wa, ba, wx, bx, lam, wout, gpost, tt):
    bsz, t, d = x.shape
    vec = _resident((1, d))
    return pl.pallas_call(
        _rec_prompt_kernel,
        grid=(bsz, t // tt),
        in_specs=[pl.BlockSpec((None, tt, d), lambda b, i: (b, i, 0)), vec, _resident((d, 2 * d)),
                  _resident((CONV_W, d)), vec, _resident(wa.shape), vec, _resident(wx.shape), vec, vec,
                  _resident((d, d)), vec],
        out_specs=[pl.BlockSpec((None, tt, d), lambda b, i: (b, i, 0)),
                   pl.BlockSpec((None, CONV_W - 1, d), lambda b, i: (b, 0, 0)),
                   pl.BlockSpec((None, 1, d), lambda b, i: (b, 0, 0))],
        out_shape=[jax.ShapeDtypeStruct((bsz, t, d), F32), jax.ShapeDtypeStruct((bsz, CONV_W - 1, d), F32),
                   jax.ShapeDtypeStruct((bsz, 1, d), F32)],
        scratch_shapes=[pltpu.VMEM((8, d), F32), pltpu.VMEM((1, d), F32)],
        compiler_params=_cparams("arbitrary", "arbitrary"),
        name="rec_prompt",
    )(x, gpre, win, cw, cb, wa, ba, wx, bx, lam, wout, gpost)


def _rec_step_kernel(x_ref, sc_ref, h0_ref, gpre_ref, win_ref, cw_ref, cb_ref, wa_ref, ba_ref, wx_ref, bx_ref,
                     lam_ref, wout_ref, gpost_ref, h_ref, conv_ref, hnew_ref):
    d = x_ref.shape[1]
    x = x_ref[...]
    xn = _rms(x, gpre_ref[...]).astype(win_ref.dtype)
    gate = _gelu(_mm(xn, win_ref[:, :d]))
    xr = _mm(xn, win_ref[:, d:])
    xc = cb_ref[...] + cw_ref[CONV_W - 1:CONV_W, :] * xr
    for k in range(CONV_W - 1):
        xc = xc + cw_ref[k:k + 1, :] * sc_ref[k]
    a, b = _lru_gates(xc, wa_ref, ba_ref[...], wx_ref, bx_ref[...], lam_ref[...])
    h = a * h0_ref[...] + b
    y = _mm(h * gate, wout_ref[...])
    h_ref[...] = x + _rms(y, gpost_ref[...])
    for k in range(CONV_W - 2):
        conv_ref[k] = sc_ref[k + 1]
    conv_ref[CONV_W - 2] = xr
    hnew_ref[...] = h


def _rec_step(x, sc, h0, gpre, win, cw, cb, wa, ba, wx, bx, lam, wout, gpost):
    rows, d = x.shape
    return pl.pallas_call(
        _rec_step_kernel,
        out_shape=[jax.ShapeDtypeStruct((rows, d), F32), jax.ShapeDtypeStruct((CONV_W - 1, rows, d), F32),
                   jax.ShapeDtypeStruct((rows, d), F32)],
        compiler_params=pltpu.CompilerParams(vmem_limit_bytes=VMEM_LIMIT),
        name="rec_step",
    )(x, sc, h0, gpre, win, cw, cb, wa, ba, wx, bx, lam, wout, gpost)


def _ffn_kernel(h_ref, g_ref, wgu_ref, wd_ref, f_ref, *, ck):
    ff = wd_ref.shape[0]
    z = _rms(h_ref[...], g_ref[...]).astype(wgu_ref.dtype)
    acc = jnp.zeros(f_ref.shape, F32)
    for c in range(ff // ck):
        g = _mm(z, wgu_ref[:, c * ck:(c + 1) * ck])
        u = _mm(z, wgu_ref[:, ff + c * ck:ff + (c + 1) * ck])
        acc = acc + _mm(_silu(g) * u, wd_ref[c * ck:(c + 1) * ck, :])
    f_ref[...] = acc


def _ffn(h, g, wgu, wd, tm, ck=512):
    m, d = h.shape
    row = pl.BlockSpec((tm, d), lambda i: (i, 0))
    return pl.pallas_call(
        functools.partial(_ffn_kernel, ck=ck),
        grid=(m // tm,),
        in_specs=[row, _resident((1, d)), _resident(wgu.shape), _resident(wd.shape)],
        out_specs=row,
        out_shape=jax.ShapeDtypeStruct((m, d), F32),
        compiler_params=_cparams("arbitrary"),
        name="ffn_dense",
    )(h, g, wgu, wd)


def _finish_kernel(h_ref, f_ref, p_ref, gpost_ref, gple_ref, wproj_ref, wgate_ref, o_ref):
    h2 = h_ref[...] + _rms(f_ref[...], gpost_ref[...])
    e = _mm(p_ref[...], wproj_ref[...])
    g = jax.nn.sigmoid(_mm(_rms(h2, gple_ref[...]), wgate_ref[...]))
    o_ref[...] = h2 + g * e


def _finish(h, f, p, gpost, gple, wproj, wgate, tm):
    m, d = h.shape
    row = pl.BlockSpec((tm, d), lambda i: (i, 0))
    return pl.pallas_call(
        _finish_kernel,
        grid=(m // tm,),
        in_specs=[row, row, pl.BlockSpec((tm, p.shape[1]), lambda i: (i, 0)), _resident((1, d)), _resident((1, d)),
                  _resident(wproj.shape), _resident(wgate.shape)],
        out_specs=row,
        out_shape=jax.ShapeDtypeStruct((m, d), F32),
        compiler_params=_cparams("arbitrary"),
        name="finish",
    )(h, f, p, gpost, gple, wproj, wgate)


def _qkv_kernel(h_ref, g_ref, w_ref, k_ref, v_ref, qb_ref, kb_ref, vb_ref):
    d = h_ref.shape[1]
    xn = _rms(h_ref[...], g_ref[...]).astype(w_ref.dtype)
    scale = (d // N_HEADS) ** -0.5
    qb_ref[...] = (_mm(xn, w_ref[:, :d]) * scale).astype(qb_ref.dtype)
    k = _mm(xn, w_ref[:, d:2 * d])
    v = _mm(xn, w_ref[:, 2 * d:])
    if k_ref.shape == k.shape:
        k_ref[...] = k
        v_ref[...] = v
    else:
        k_ref[...] = k.T
        v_ref[...] = v.T
    kb_ref[...] = k.astype(BF16)
    vb_ref[...] = v.astype(BF16)


def _qkv(h, g, w, tm, q_dtype=BF16, seq=None):
    m, d = h.shape
    row = pl.BlockSpec((tm, d), lambda i: (i, 0))
    if seq is None:
        kv_spec, kv_shape = row, jax.ShapeDtypeStruct((m, d), F32)
    else:
        per = seq // tm
        kv_spec = pl.BlockSpec((None, d, tm), lambda i: (i // per, 0, i % per))
        kv_shape = jax.ShapeDtypeStruct((m // seq, d, seq), F32)
    return pl.pallas_call(
        _qkv_kernel,
        grid=(m // tm,),
        in_specs=[row, _resident((1, d)), _resident(w.shape)],
        out_specs=[kv_spec, kv_spec, row, row, row],
        out_shape=[kv_shape, kv_shape, jax.ShapeDtypeStruct((m, d), q_dtype)]
        + [jax.ShapeDtypeStruct((m, d), BF16)] * 2,
        compiler_params=_cparams("arbitrary"),
        name="qkv",
    )(h, g, w)


def _sb_weights(z, sp, ls, tri, carry):
    hi = ls.astype(BF16)
    lo = (ls - hi.astype(F32)).astype(BF16)
    later = _dot(hi, tri) + _dot(lo, tri)
    return jnp.exp((z - sp) + later + carry)


def _sb_prompt_kernel(bias_ref, q_ref, k_ref, v_ref, o_ref, qs_ref, w_ref, acc_ref, carry_ref, *, hps, tk):
    tq, width = q_ref.shape
    hd = width // hps
    rows = hps * tq
    per = tq // tk
    grp = pl.program_id(1)
    i = pl.program_id(2)
    lane = lax.broadcasted_iota(jnp.int32, (tq, width), 1)
    q = q_ref[...]
    for h in range(hps):
        qs_ref[h * tq:(h + 1) * tq, :] = jnp.where((lane >= h * hd) & (lane < (h + 1) * hd), q, jnp.zeros_like(q))
    acc_ref[...] = jnp.zeros_like(acc_ref)
    carry_ref[...] = jnp.zeros_like(carry_ref)
    r2 = lax.broadcasted_iota(jnp.int32, (tk, tk), 0)
    c2 = lax.broadcasted_iota(jnp.int32, (tk, tk), 1)
    neg_tri = jnp.where(r2 > c2, -1.0, 0.0).astype(BF16)
    ahead = (lax.broadcasted_iota(jnp.int32, (rows, tk), 1)
             - lax.rem(lax.broadcasted_iota(jnp.int32, (rows, tk), 0), tq))
    bias = jnp.concatenate([jnp.full((tq, 1), bias_ref[hps * grp + h], F32) for h in range(hps)], axis=0)

    def weights(j, diag):
        z = _dot_nt(qs_ref[...], k_ref[pl.ds(pl.multiple_of(j * tk, tk), tk), :]) + bias
        sp = jnp.maximum(z, 0.0) + jnp.log(1.0 + jnp.exp(-jnp.abs(z)))
        if diag:
            causal = ahead < i * tq - j * tk
            sp = jnp.where(causal, sp, 0.0)
        later = _dot(sp.astype(BF16), neg_tri)
        w = jnp.exp((z - sp) + later + carry_ref[...])
        if diag:
            w = jnp.where(causal, w, 0.0)
        w_ref[...] = w.astype(BF16)
        carry_ref[...] -= jnp.sum(sp, axis=-1, keepdims=True)

    def values(j):
        acc_ref[...] += _dot(w_ref[...], v_ref[pl.ds(pl.multiple_of(j * tk, tk), tk), :])

    top = (i + 1) * per - 1
    weights(top, True)
    for back in range(1, per):
        values(top - back + 1)
        weights(top - back, True)

    def body(jj, c):
        j = i * per - 1 - jj
        values(j + 1)
        weights(j, False)
        return c

    lax.fori_loop(0, i * per, body, 0)
    values(0)
    out = acc_ref[0:tq, :]
    for h in range(1, hps):
        out = jnp.where(lane >= h * hd, acc_ref[h * tq:(h + 1) * tq, :], out)
    o_ref[...] = out.astype(o_ref.dtype)


def _sb_prompt(qb, kb, vb, bias, tq, tk, hps=4):
    bsz, t, d = qb.shape
    width = hps * (d // N_HEADS)
    qspec = pl.BlockSpec((None, tq, width), lambda b, g, i: (b, i, g))
    kvspec = pl.BlockSpec((None, t, width), lambda b, g, i: (b, 0, g))
    return pl.pallas_call(
        functools.partial(_sb_prompt_kernel, hps=hps, tk=tk),
        grid=(bsz, d // width, t // tq),
        in_specs=[pl.BlockSpec(memory_space=pltpu.SMEM), qspec, kvspec, kvspec],
        out_specs=qspec,
        out_shape=jax.ShapeDtypeStruct((bsz, t, d), BF16),
        scratch_shapes=[pltpu.VMEM((hps * tq, width), BF16), pltpu.VMEM((hps * tq, tk), BF16),
                        pltpu.VMEM((hps * tq, width), F32),
                        pltpu.VMEM((hps * tq, 1), F32)],
        compiler_params=_cparams("arbitrary", "arbitrary", "arbitrary"),
        name="sb_prompt",
    )(bias, qb, kb, vb)


def _sb_decode_kernel(pt_ref, q_ref, bias_ref, *refs, pg):
    del pt_ref
    k_refs, v_refs = refs[:pg], refs[pg:2 * pg]
    o_ref, qb_ref, acc_ref, carry_ref = refs[2 * pg:]
    nh, _, page = k_refs[0].shape
    g = pl.program_id(1)

    @pl.when(g == 0)
    def _():
        acc_ref[...] = jnp.zeros_like(acc_ref)
        carry_ref[...] = jnp.zeros_like(carry_ref)
        qb_ref[...] = jnp.broadcast_to(q_ref[...], qb_ref.shape)

    row = lax.broadcasted_iota(jnp.int32, (page, page), 0)
    col = lax.broadcasted_iota(jnp.int32, (page, page), 1)
    tri = jnp.where(row > col, 1.0, 0.0).astype(BF16)
    z = jnp.concatenate([jnp.sum(qb_ref[h] * k_refs[r][h], axis=0, keepdims=True)
                         for r in range(pg) for h in range(nh)], axis=0)
    z = z + jnp.concatenate([bias_ref[...]] * pg, axis=0)
    sp = _softplus(z)
    ls = -sp
    total = jnp.sum(ls, axis=-1, keepdims=True)
    carries = [carry_ref[...]]
    for r in range(pg):
        carries.append(carries[-1] + total[r * nh:(r + 1) * nh])
    carry_ref[...] = carries[pg]
    w = _sb_weights(z, sp, ls, tri, jnp.concatenate(carries[:pg], axis=0))
    for h in range(nh):
        part = w[h:h + 1, :] * v_refs[0][h]
        for r in range(1, pg):
            part = part + w[r * nh + h:r * nh + h + 1, :] * v_refs[r][h]
        acc_ref[h] += part

    @pl.when(g == pl.num_programs(1) - 1)
    def _():
        o_ref[...] = jnp.sum(acc_ref[...], axis=-1)


def _sb_decode(q, cache_k, cache_v, page_table, bias, pg):
    s, nh, hd, _ = q.shape
    n_pages = page_table.shape[1]
    page = cache_k.shape[3]

    def page_spec(r):
        return pl.BlockSpec((None, nh, hd, page), lambda b, g, pt: (pt[b, n_pages - 1 - (g * pg + r)], 0, 0, 0))

    grid_spec = pltpu.PrefetchScalarGridSpec(
        num_scalar_prefetch=1,
        grid=(s, n_pages // pg),
        in_specs=[pl.BlockSpec((None, nh, hd, 1), lambda b, g, pt: (b, 0, 0, 0)),
                  pl.BlockSpec((nh, 1), lambda b, g, pt: (0, 0))] + [page_spec(r) for r in range(pg)] * 2,
        out_specs=pl.BlockSpec((None, nh, hd), lambda b, g, pt: (b, 0, 0)),
        scratch_shapes=[pltpu.VMEM((nh, hd, page), F32), pltpu.VMEM((nh, hd, page), F32), pltpu.VMEM((nh, 1), F32)],
    )
    return pl.pallas_call(
        functools.partial(_sb_decode_kernel, pg=pg),
        grid_spec=grid_spec,
        out_shape=jax.ShapeDtypeStruct((s, nh, hd), F32),
        compiler_params=_cparams("arbitrary", "arbitrary"),
        name="sb_decode",
    )(page_table, q, bias, *([cache_k] * pg), *([cache_v] * pg))


def _mix_residual(h_ref, o_ref, wo_ref, gpost_ref, gffn_ref):
    h1 = h_ref[...] + _rms(_mm(o_ref[...], wo_ref[...]), gpost_ref[...])
    return h1, _rms(h1, gffn_ref[...])


def _top2(logits):
    idx = lax.broadcasted_iota(jnp.int32, logits.shape, 1)
    m1 = jnp.max(logits, axis=-1, keepdims=True)
    i1 = jnp.min(jnp.where(logits == m1, idx, LANES), axis=-1, keepdims=True)
    rest = jnp.where(idx == i1, NEG_BIG, logits)
    m2 = jnp.max(rest, axis=-1, keepdims=True)
    i2 = jnp.min(jnp.where(rest == m2, idx, LANES), axis=-1, keepdims=True)
    e = jnp.exp(m2 - m1)
    w1 = 1.0 / (1.0 + e)
    return idx, i1, i2, w1, e * w1


def _attn_out_kernel(h_ref, o_ref, wo_ref, gpost_ref, gffn_ref, wr_ref, br_ref, h1_ref, z_ref, comb_ref):
    h1, z = _mix_residual(h_ref, o_ref, wo_ref, gpost_ref, gffn_ref)
    h1_ref[...] = h1
    z_ref[...] = z.astype(z_ref.dtype)
    idx, i1, i2, w1, w2 = _top2(_mm(z, wr_ref[...]) + br_ref[...])
    comb_ref[...] = jnp.where(idx == i1, w1, 0.0) + jnp.where(idx == i2, w2, 0.0)


def _attn_out_route_kernel(h_ref, o_ref, wo_ref, gpost_ref, gffn_ref, wr_ref, br_ref, h1_ref, z_ref, cols_ref,
                           rows_ref):
    h1, z = _mix_residual(h_ref, o_ref, wo_ref, gpost_ref, gffn_ref)
    h1_ref[...] = h1
    z_ref[...] = z
    idx, i1, i2, w1, w2 = _top2(_mm(z, wr_ref[...]) + br_ref[...])
    cols = jnp.where(idx == 0, i1.astype(F32), jnp.where(idx == 1, i2.astype(F32),
                                                         jnp.where(idx == 2, w1, jnp.where(idx == 3, w2, 0.0))))
    cols_ref[...] = cols
    pick = (lax.broadcasted_iota(jnp.int32, (8, LANES), 0) == lax.broadcasted_iota(jnp.int32, (8, LANES), 1))
    rows_ref[...] = lax.dot_general(pick.astype(F32), cols, (((1,), (1,)), ((), ())),
                                    precision=lax.Precision.HIGHEST, preferred_element_type=F32)


def _attn_out(h, o, wo, gpost, gffn, wr, br, tm, route=False):
    m, d = h.shape
    row = pl.BlockSpec((tm, d), lambda i: (i, 0))
    comb = pl.BlockSpec((tm, LANES), lambda i: (i, 0))
    in_specs = [row, row, _resident(wo.shape), _resident((1, d)), _resident((1, d)), _resident(wr.shape),
                _resident(br.shape)]
    if route:
        return pl.pallas_call(
            _attn_out_route_kernel,
            grid=(m // tm,),
            in_specs=in_specs,
            out_specs=[row, row, comb, pl.BlockSpec((8, tm), lambda i: (0, i))],
            out_shape=[jax.ShapeDtypeStruct((m, d), F32), jax.ShapeDtypeStruct((m, d), F32),
                       jax.ShapeDtypeStruct((m, LANES), F32), jax.ShapeDtypeStruct((8, m), F32)],
            compiler_params=_cparams("arbitrary"),
            name="attn_out_route",
        )(h, o, wo, gpost, gffn, wr, br)
    return pl.pallas_call(
        _attn_out_kernel,
        grid=(m // tm,),
        in_specs=in_specs,
        out_specs=[row, row, comb],
        out_shape=[jax.ShapeDtypeStruct((m, d), F32), jax.ShapeDtypeStruct((m, d), BF16),
                   jax.ShapeDtypeStruct((m, LANES), F32)],
        compiler_params=_cparams("arbitrary"),
        name="attn_out",
    )(h, o, wo, gpost, gffn, wr, br)


def _moe_kernel(z_ref, comb_ref, wg_ref, wu_ref, wd_ref, f_ref):
    e = pl.program_id(1)
    c = pl.program_id(2)

    @pl.when((e == 0) & (c == 0))
    def _():
        f_ref[...] = jnp.zeros_like(f_ref)

    z = z_ref[...]
    act = (_silu(_dot(z, wg_ref[...])) * _dot(z, wu_ref[...])).astype(BF16)
    lane = lax.broadcasted_iota(jnp.int32, comb_ref.shape, 1)
    ce = jnp.sum(jnp.where(lane == e, comb_ref[...], 0.0), axis=-1, keepdims=True)
    f_ref[...] += ce * _dot(act, wd_ref[...])


def _moe(z, comb, wgu, wd, tm, ck=512):
    m, d = z.shape
    n_e, ff, _ = wd.shape
    fc = ff // ck
    return pl.pallas_call(
        _moe_kernel,
        grid=(m // tm, n_e, fc),
        in_specs=[pl.BlockSpec((tm, d), lambda i, e, c: (i, 0)), pl.BlockSpec((tm, LANES), lambda i, e, c: (i, 0)),
                  pl.BlockSpec((None, d, ck), lambda i, e, c: (e, 0, c)),
                  pl.BlockSpec((None, d, ck), lambda i, e, c: (e, 0, fc + c)),
                  pl.BlockSpec((None, ck, d), lambda i, e, c: (e, c, 0))],
        out_specs=pl.BlockSpec((tm, d), lambda i, e, c: (i, 0)),
        out_shape=jax.ShapeDtypeStruct((m, d), F32),
        compiler_params=_cparams("arbitrary", "arbitrary", "arbitrary"),
        name="moe",
    )(z, comb, wgu, wgu, wd)


def _route_plan(i1, i2, n_e, tm, n_tiles):
    m = i1.shape[0]
    experts = jnp.arange(n_e, dtype=jnp.int32)[None, :]
    hit = (experts == i1[:, None]).astype(jnp.int32) + (experts == i2[:, None]).astype(jnp.int32)
    upto = jnp.cumsum(hit, axis=0)
    before = upto - hit
    padded = ((upto[-1] + tm - 1) // tm) * tm
    ends = jnp.cumsum(padded)
    starts = ends - padded
    slot1 = starts[i1] + jnp.take_along_axis(before, i1[:, None], axis=1)[:, 0]
    slot2 = starts[i2] + jnp.take_along_axis(before, i2[:, None], axis=1)[:, 0]
    slots = jnp.stack([slot1, slot2]).astype(jnp.int32)
    token = jnp.tile(jnp.arange(m, dtype=jnp.int32), 2)
    source = jnp.zeros((n_tiles * tm,), jnp.int32).at[slots.reshape(-1)].set(token).reshape(n_tiles, 1, tm)
    tile_start = jnp.arange(n_tiles, dtype=jnp.int32) * tm
    tile_expert = jnp.minimum(jnp.sum(tile_start[:, None] >= ends[None, :], axis=1), n_e - 1).astype(jnp.int32)
    return slots, source, tile_expert, (ends[-1:] // tm).astype(jnp.int32)


GATHER_UNROLL = 8


def _start_row_gather(src_ref, row_of, dst_ref, sem):
    rows = dst_ref.shape[0]

    def body(r, carry):
        pltpu.make_async_copy(src_ref.at[pl.ds(row_of(r), 1), :], dst_ref.at[pl.ds(r, 1), :], sem).start()
        return carry

    lax.fori_loop(0, rows, body, 0, unroll=GATHER_UNROLL)


def _wait_row_gather(src_ref, dst_ref, sem):
    pltpu.make_async_copy(src_ref.at[pl.ds(0, dst_ref.shape[0]), :], dst_ref, sem).wait()


def _experts_kernel(te_ref, nu_ref, src_ref, nxt_ref, z_ref, wg_ref, wu_ref, wd_ref, y_ref, xg_ref, xb_ref, sem):
    del te_ref
    i = pl.program_id(0)
    c = pl.program_id(1)
    n_used = nu_ref[0]
    used = i < n_used

    @pl.when(used & (c == 0))
    def _():
        buf = i % 2

        @pl.when(i == 0)
        def _():
            _start_row_gather(z_ref, lambda r: src_ref[0, r], xg_ref.at[0], sem.at[0])

        _wait_row_gather(z_ref, xg_ref.at[buf], sem.at[buf])
        xb_ref[...] = xg_ref[buf].astype(BF16)

        @pl.when(i + 1 < n_used)
        def _():
            _start_row_gather(z_ref, lambda r: nxt_ref[0, r], xg_ref.at[1 - buf], sem.at[1 - buf])

    @pl.when(used)
    def _():
        xb = xb_ref[...]
        act = (_silu(_dot(xb, wg_ref[...])) * _dot(xb, wu_ref[...])).astype(BF16)
        y = _dot(act, wd_ref[...])

        @pl.when(c == 0)
        def _():
            y_ref[...] = y

        @pl.when(c > 0)
        def _():
            y_ref[...] += y

    @pl.when(jnp.logical_not(used) & (c == 0))
    def _():
        y_ref[...] = jnp.zeros_like(y_ref)


def _experts(z, source, tile_expert, n_used, wgu, wd, ck=512):
    n_tiles, _, tm = source.shape
    d = z.shape[1]
    ff = wd.shape[1]
    fc = ff // ck
    grid_spec = pltpu.PrefetchScalarGridSpec(
        num_scalar_prefetch=2,
        grid=(n_tiles, fc),
        in_specs=[pl.BlockSpec((None, 1, tm), lambda i, c, te, nu: (i, 0, 0), memory_space=pltpu.SMEM),
                  pl.BlockSpec((None, 1, tm), lambda i, c, te, nu: (jnp.minimum(i + 1, n_tiles - 1), 0, 0),
                               memory_space=pltpu.SMEM),
                  pl.BlockSpec(memory_space=pl.ANY),
                  pl.BlockSpec((None, d, ck), lambda i, c, te, nu: (te[i], 0, c)),
                  pl.BlockSpec((None, d, ck), lambda i, c, te, nu: (te[i], 0, fc + c)),
                  pl.BlockSpec((None, ck, d), lambda i, c, te, nu: (te[i], c, 0))],
        out_specs=pl.BlockSpec((tm, d), lambda i, c, te, nu: (i, 0)),
        scratch_shapes=[pltpu.VMEM((2, tm, d), z.dtype), pltpu.VMEM((tm, d), BF16), pltpu.SemaphoreType.DMA((2,))],
    )
    return pl.pallas_call(
        _experts_kernel,
        grid_spec=grid_spec,
        out_shape=jax.ShapeDtypeStruct((n_tiles * tm, d), F32),
        compiler_params=_cparams("arbitrary", "arbitrary"),
        name="moe_experts",
    )(tile_expert, n_used, source, source, z, wgu, wgu, wd)


def _finish_routed_kernel(slots_ref, nxt_ref, h_ref, cols_ref, p_ref, gpost_ref, gple_ref, wproj_ref, wgate_ref,
                          ys_ref, o_ref, y_ref, sem):
    i = pl.program_id(0)
    buf = i % 2

    def start(rows_ref, b):
        for k in range(TOP_K):
            _start_row_gather(ys_ref, lambda r, k=k: rows_ref[k, r], y_ref.at[b, k], sem.at[b])

    @pl.when(i == 0)
    def _():
        start(slots_ref, 0)

    for k in range(TOP_K):
        _wait_row_gather(ys_ref, y_ref.at[buf, k], sem.at[buf])

    @pl.when(i + 1 < pl.num_programs(0))
    def _():
        start(nxt_ref, 1 - buf)

    cols = cols_ref[...]
    f = cols[:, 2:3] * y_ref[buf, 0] + cols[:, 3:4] * y_ref[buf, 1]
    h2 = h_ref[...] + _rms(f, gpost_ref[...])
    e = _mm(p_ref[...], wproj_ref[...])
    g = jax.nn.sigmoid(_mm(_rms(h2, gple_ref[...]), wgate_ref[...]))
    o_ref[...] = h2 + g * e


def _finish_routed(h, ys, slots, cols, p, gpost, gple, wproj, wgate, tm):
    m, d = h.shape
    n = m // tm
    row = pl.BlockSpec((tm, d), lambda i: (i, 0))
    return pl.pallas_call(
        _finish_routed_kernel,
        grid=(n,),
        in_specs=[pl.BlockSpec((TOP_K, tm), lambda i: (0, i), memory_space=pltpu.SMEM),
                  pl.BlockSpec((TOP_K, tm), lambda i: (0, jnp.minimum(i + 1, n - 1)), memory_space=pltpu.SMEM), row,
                  pl.BlockSpec((tm, LANES), lambda i: (i, 0)), pl.BlockSpec((tm, p.shape[1]), lambda i: (i, 0)),
                  _resident((1, d)), _resident((1, d)), _resident(wproj.shape), _resident(wgate.shape),
                  pl.BlockSpec(memory_space=pl.ANY)],
        out_specs=row,
        out_shape=jax.ShapeDtypeStruct((m, d), F32),
        scratch_shapes=[pltpu.VMEM((2, TOP_K, tm, d), F32), pltpu.SemaphoreType.DMA((2,))],
        compiler_params=_cparams("arbitrary"),
        name="finish_routed",
    )(slots, slots, h, cols, p, gpost, gple, wproj, wgate, ys)


def _tile(m, pref):
    return pref if m % pref == 0 else m


def kernel(x_prompt, x_sample, state_conv, state_lru, cache_k, cache_v, page_table, p_prompt, p_sample, norm_mix_pre, norm_mix_post, norm_ffn_pre, norm_ffn_post, norm_ple, rec_w_in, rec_conv_w, rec_conv_b, rec_w_a, rec_b_a, rec_w_x, rec_b_x, rec_lambda, rec_w_out, att_w_qkv, att_w_o, att_sb_bias, ffn_w_gu, ffn_w_down, moe_w_router, moe_b_router, moe_w_gu, moe_w_down, ple_w_proj, ple_w_gate):
    bsz, t, d = x_prompt.shape
    s = x_sample.shape[0]
    depth = norm_mix_pre.shape[0]
    mp = bsz * t
    tmp = _tile(mp, 512)
    vec = lambda a: a.reshape(1, -1)
    bf = lambda a: a.astype(BF16)
    hp = x_prompt.reshape(mp, d)
    hs = x_sample.reshape(s, d)
    keep = lambda a: a
    outs = {k: [] for k in ("conv_p", "lru_p", "k_p", "v_p", "conv_s", "lru_s", "k_s", "v_s")}
    for i in range(depth):
        j = i // 2
        if i % 2 == 0:
            def rec(cast):
                return (vec(norm_mix_pre[i]), cast(rec_w_in[j]), rec_conv_w[j], vec(rec_conv_b[j]), cast(rec_w_a[j]),
                        vec(rec_b_a[j]), cast(rec_w_x[j]), vec(rec_b_x[j]), vec(rec_lambda[j]), cast(rec_w_out[j]),
                        vec(norm_mix_post[i]))
            hp, conv_new, h_new = _rec_prompt(hp.reshape(bsz, t, d), *rec(bf), tt=_tile(t, 256))
            outs["conv_p"].append(conv_new)
            outs["lru_p"].append(h_new.reshape(bsz, d))
            hs, conv_new, h_new = _rec_step(hs, jnp.swapaxes(state_conv[j], 0, 1), state_lru[j], *rec(keep))
            outs["conv_s"].append(jnp.swapaxes(conv_new, 0, 1))
            outs["lru_s"].append(h_new)
            hp = hp.reshape(mp, d)
            fp = _ffn(hp, vec(norm_ffn_pre[i]), bf(ffn_w_gu[j]), bf(ffn_w_down[j]), tmp)
            fs = _ffn(hs, vec(norm_ffn_pre[i]), ffn_w_gu[j], ffn_w_down[j], s)
        else:
            hd = d // N_HEADS
            kp, vp, qb, kb, vb = _qkv(hp, vec(norm_mix_pre[i]), bf(att_w_qkv[j]), _tile(t, 512), seq=t)
            outs["k_p"].append(kp.reshape(bsz, N_HEADS, hd, t).transpose(0, 3, 1, 2))
            outs["v_p"].append(vp.reshape(bsz, N_HEADS, hd, t).transpose(0, 3, 1, 2))
            tk = _tile(t, 256)
            op = _sb_prompt(qb.reshape(bsz, t, d), kb.reshape(bsz, t, d), vb.reshape(bsz, t, d), att_sb_bias[j],
                            tq=_tile(t, 2 * tk), tk=tk)
            ks, vs, qs, _, _ = _qkv(hs, vec(norm_mix_pre[i]), att_w_qkv[j], s, q_dtype=F32)
            outs["k_s"].append(ks.reshape(s, 1, N_HEADS, hd))
            outs["v_s"].append(vs.reshape(s, 1, N_HEADS, hd))
            n_pages = page_table.shape[1]
            os_ = _sb_decode(qs.reshape(s, N_HEADS, hd, 1), jnp.transpose(cache_k[j], (0, 2, 3, 1)),
                             jnp.transpose(cache_v[j], (0, 2, 3, 1)), page_table, att_sb_bias[j].reshape(N_HEADS, 1),
                             pg=8 if n_pages % 8 == 0 else 1)
            wr = jnp.pad(moe_w_router[j], ((0, 0), (0, LANES - N_EXPERTS)))
            br = jnp.pad(moe_b_router[j], (0, LANES - N_EXPERTS), constant_values=NEG_BIG).reshape(1, LANES)
            norms = (vec(norm_mix_post[i]), vec(norm_ffn_pre[i]))
            hp, zp, cols, rows = _attn_out(hp, op.reshape(mp, d), bf(att_w_o[j]), *norms, bf(wr), br, tmp, route=True)
            hs, zs, cs = _attn_out(hs, os_.reshape(s, d), att_w_o[j], *norms, wr, br, s)
            wgu, wd = bf(moe_w_gu[j]), bf(moe_w_down[j])
            fs = _moe(zs, cs, wgu, wd, s)
            tme = _tile(mp, 512)
            n_tiles = (TOP_K * mp + N_EXPERTS * (tme - 1)) // tme
            slots, source, tile_expert, n_used = _route_plan(rows[0].astype(jnp.int32), rows[1].astype(jnp.int32),
                                                             N_EXPERTS, tme, n_tiles)
            ys = _experts(zp, source, tile_expert, n_used, wgu, wd)
        tail = (vec(norm_ffn_post[i]), vec(norm_ple[i]))
        if i % 2 == 0:
            hp = _finish(hp, fp, p_prompt[i].reshape(mp, -1), *tail, bf(ple_w_proj[i]), bf(ple_w_gate[i]), tmp)
        else:
            hp = _finish_routed(hp, ys, slots, cols, p_prompt[i].reshape(mp, -1), *tail, bf(ple_w_proj[i]),
                                bf(ple_w_gate[i]), tmp)
        hs = _finish(hs, fs, p_sample[i].reshape(s, -1), *tail, ple_w_proj[i], ple_w_gate[i], s)
    st = lambda k: jnp.stack(outs[k])
    return (hp.reshape(bsz, t, d), hs.reshape(s, 1, d), st("conv_p"), st("lru_p"), st("k_p"), st("v_p"),
            st("conv_s"), st("lru_s"), st("k_s"), st("v_s"))
```

```python
import functools
import math

import jax
import jax.numpy as jnp
from jax import lax
from jax.experimental import pallas as pl
from jax.experimental.pallas import tpu as pltpu

F32 = jnp.float32
BF16 = jnp.bfloat16

RMS_EPS = 1e-6
LRU_C = 8.0
N_HEADS = 16
N_LRU_BLOCKS = 8
CONV_W = 4
N_EXPERTS = 8
TOP_K = 2
LANES = 128
SUBLANES = 8
LOG2E = 1.4426950408889634
NEG_BIG = -1e30
VMEM_LIMIT = 56 * 1024 * 1024


def _cparams(*sem):
    return pltpu.CompilerParams(dimension_semantics=sem, vmem_limit_bytes=VMEM_LIMIT)


def _resident(shape):
    return pl.BlockSpec(shape, lambda *_: (0,) * len(shape), pipeline_mode=pl.Buffered(1))


def _dot(a, b):
    return jnp.dot(a, b, preferred_element_type=F32)


def _mm(x, w):
    if w.dtype == F32:
        return jnp.dot(x.astype(F32), w, precision=lax.Precision.HIGHEST, preferred_element_type=F32)
    return jnp.dot(x.astype(w.dtype), w, preferred_element_type=F32)


def _dot_nt(a, b):
    return lax.dot_general(a, b, (((1,), (1,)), ((), ())), preferred_element_type=F32)


def _rms(x, g):
    return x * lax.rsqrt(jnp.mean(x * x, axis=-1, keepdims=True) + RMS_EPS) * g


def _softplus(x):
    return jnp.maximum(x, 0.0) + jnp.log1p(jnp.exp(-jnp.abs(x)))


def _gelu(x):
    c = math.sqrt(2.0 / math.pi)
    return 0.5 * x * (1.0 + jnp.tanh(c * (x + 0.044715 * (x * x * x))))


def _silu(x):
    return x * jax.nn.sigmoid(x)


def _lru_gates(xc, wa_ref, ba, wx_ref, bx, lam):
    blk = xc.shape[1] // N_LRU_BLOCKS
    ra, rx = [], []
    for n in range(N_LRU_BLOCKS):
        xb = xc[:, n * blk:(n + 1) * blk]
        ra.append(_mm(xb, wa_ref[n]))
        rx.append(_mm(xb, wx_ref[n]))
    r = jax.nn.sigmoid(jnp.concatenate(ra, axis=1) + ba)
    ig = jax.nn.sigmoid(jnp.concatenate(rx, axis=1) + bx)
    log_a = -LRU_C * r * _softplus(-lam)
    a = jnp.exp(log_a)
    gx = jnp.sqrt(-jnp.tanh(log_a) * (a * a + 1.0)) * ig * xc
    return a, gx


def _rec_prompt_kernel(x_ref, gpre_ref, win_ref, cw_ref, cb_ref, wa_ref, ba_ref, wx_ref, bx_ref, lam_ref,
                       wout_ref, gpost_ref, h_ref, conv_ref, hlast_ref, tail_ref, hc_ref):
    tt, d = x_ref.shape

    @pl.when(pl.program_id(1) == 0)
    def _():
        tail_ref[...] = jnp.zeros_like(tail_ref)
        hc_ref[...] = jnp.zeros_like(hc_ref)

    x = x_ref[...]
    xn = _rms(x, gpre_ref[...]).astype(win_ref.dtype)
    gate = _gelu(_mm(xn, win_ref[:, :d]))
    xr = _mm(xn, win_ref[:, d:])
    xpad = jnp.concatenate([tail_ref[...], xr], axis=0)
    xc = cb_ref[...] + cw_ref[3:4, :] * xr
    for back in range(1, CONV_W):
        xc = xc + cw_ref[CONV_W - 1 - back:CONV_W - back, :] * xpad[8 - back:8 - back + tt]
    tail_ref[...] = xr[tt - 8:]
    a, b = _lru_gates(xc, wa_ref, ba_ref[...], wx_ref, bx_ref[...], lam_ref[...])
    row = lax.rem(lax.broadcasted_iota(jnp.int32, (tt, 1), 0), SUBLANES)
    s = 1
    while s < SUBLANES:
        keep = row >= s
        a_sh = jnp.where(keep, pltpu.roll(a, s, 0), 1.0)
        b_sh = jnp.where(keep, pltpu.roll(b, s, 0), 0.0)
        b = a * b_sh + b
        a = a * a_sh
        s *= 2
    last = hc_ref[...]
    groups = []
    for g in range(tt // SUBLANES):
        rows = slice(g * SUBLANES, (g + 1) * SUBLANES)
        hg = b[rows] + a[rows] * last
        groups.append(hg)
        last = hg[SUBLANES - 1:]
    h = jnp.concatenate(groups, axis=0)
    hc_ref[...] = last
    y = _mm(h * gate, wout_ref[...])
    h_ref[...] = x + _rms(y, gpost_ref[...])
    conv_ref[...] = xr[tt - (CONV_W - 1):]
    hlast_ref[...] = h[tt - 1:]


def _rec_prompt(x, gpre, win, cw, cb, wa, ba, wx, bx, lam, wout, gpost, tt):
    bsz, t, d = x.shape
    vec = _resident((1, d))
    return pl.pallas_call(
        _rec_prompt_kernel,
        grid=(bsz, t // tt),
        in_specs=[pl.BlockSpec((None, tt, d), lambda b, i: (b, i, 0)), vec, _resident((d, 2 * d)),
                  _resident((CONV_W, d)), vec, _resident(wa.shape), vec, _resident(wx.shape), vec, vec,
                  _resident((d, d)), vec],
        out_specs=[pl.BlockSpec((None, tt, d), lambda b, i: (b, i, 0)),
                   pl.BlockSpec((None, CONV_W - 1, d), lambda b, i: (b, 0, 0)),
                   pl.BlockSpec((None, 1, d), lambda b, i: (b, 0, 0))],
        out_shape=[jax.ShapeDtypeStruct((bsz, t, d), F32), jax.ShapeDtypeStruct((bsz, CONV_W - 1, d), F32),
                   jax.ShapeDtypeStruct((bsz, 1, d), F32)],
        scratch_shapes=[pltpu.VMEM((8, d), F32), pltpu.VMEM((1, d), F32)],
        compiler_params=_cparams("arbitrary", "arbitrary"),
        name="rec_prompt",
    )(x, gpre, win, cw, cb, wa, ba, wx, bx, lam, wout, gpost)


def _rec_step_kernel(x_ref, sc_ref, h0_ref, gpre_ref, win_ref, cw_ref, cb_ref, wa_ref, ba_ref, wx_ref, bx_ref,
                     lam_ref, wout_ref, gpost_ref, h_ref, conv_ref, hnew_ref):
    d = x_ref.shape[1]
    x = x_ref[...]
    xn = _rms(x, gpre_ref[...]).astype(win_ref.dtype)
    gate = _gelu(_mm(xn, win_ref[:, :d]))
    xr = _mm(xn, win_ref[:, d:])
    xc = cb_ref[...] + cw_ref[CONV_W - 1:CONV_W, :] * xr
    for k in range(CONV_W - 1):
        xc = xc + cw_ref[k:k + 1, :] * sc_ref[k]
    a, b = _lru_gates(xc, wa_ref, ba_ref[...], wx_ref, bx_ref[...], lam_ref[...])
    h = a * h0_ref[...] + b
    y = _mm(h * gate, wout_ref[...])
    h_ref[...] = x + _rms(y, gpost_ref[...])
    for k in range(CONV_W - 2):
        conv_ref[k] = sc_ref[k + 1]
    conv_ref[CONV_W - 2] = xr
    hnew_ref[...] = h


def _rec_step(x, sc, h0, gpre, win, cw, cb, wa, ba, wx, bx, lam, wout, gpost):
    rows, d = x.shape
    return pl.pallas_call(
        _rec_step_kernel,
        out_shape=[jax.ShapeDtypeStruct((rows, d), F32), jax.ShapeDtypeStruct((CONV_W - 1, rows, d), F32),
                   jax.ShapeDtypeStruct((rows, d), F32)],
        compiler_params=pltpu.CompilerParams(vmem_limit_bytes=VMEM_LIMIT),
        name="rec_step",
    )(x, sc, h0, gpre, win, cw, cb, wa, ba, wx, bx, lam, wout, gpost)


def _ffn_kernel(h_ref, g_ref, wgu_ref, wd_ref, f_ref, *, ck):
    ff = wd_ref.shape[0]
    z = _rms(h_ref[...], g_ref[...]).astype(wgu_ref.dtype)
    acc = jnp.zeros(f_ref.shape, F32)
    for c in range(ff // ck):
        g = _mm(z, wgu_ref[:, c * ck:(c + 1) * ck])
        u = _mm(z, wgu_ref[:, ff + c * ck:ff + (c + 1) * ck])
        acc = acc + _mm(_silu(g) * u, wd_ref[c * ck:(c + 1) * ck, :])
    f_ref[...] = acc


def _ffn(h, g, wgu, wd, tm, ck=512):
    m, d = h.shape
    row = pl.BlockSpec((tm, d), lambda i: (i, 0))
    return pl.pallas_call(
        functools.partial(_ffn_kernel, ck=ck),
        grid=(m // tm,),
        in_specs=[row, _resident((1, d)), _resident(wgu.shape), _resident(wd.shape)],
        out_specs=row,
        out_shape=jax.ShapeDtypeStruct((m, d), F32),
        compiler_params=_cparams("arbitrary"),
        name="ffn_dense",
    )(h, g, wgu, wd)


def _finish_kernel(h_ref, f_ref, p_ref, gpost_ref, gple_ref, wproj_ref, wgate_ref, o_ref):
    h2 = h_ref[...] + _rms(f_ref[...], gpost_ref[...])
    e = _mm(p_ref[...], wproj_ref[...])
    g = jax.nn.sigmoid(_mm(_rms(h2, gple_ref[...]), wgate_ref[...]))
    o_ref[...] = h2 + g * e


def _finish(h, f, p, gpost, gple, wproj, wgate, tm):
    m, d = h.shape
    row = pl.BlockSpec((tm, d), lambda i: (i, 0))
    return pl.pallas_call(
        _finish_kernel,
        grid=(m // tm,),
        in_specs=[row, row, pl.BlockSpec((tm, p.shape[1]), lambda i: (i, 0)), _resident((1, d)), _resident((1, d)),
                  _resident(wproj.shape), _resident(wgate.shape)],
        out_specs=row,
        out_shape=jax.ShapeDtypeStruct((m, d), F32),
        compiler_params=_cparams("arbitrary"),
        name="finish",
    )(h, f, p, gpost, gple, wproj, wgate)


def _qkv_kernel(h_ref, g_ref, w_ref, k_ref, v_ref, qb_ref, kb_ref, vb_ref):
    d = h_ref.shape[1]
    xn = _rms(h_ref[...], g_ref[...]).astype(w_ref.dtype)
    scale = (d // N_HEADS) ** -0.5
    qb_ref[...] = (_mm(xn, w_ref[:, :d]) * scale).astype(qb_ref.dtype)
    k = _mm(xn, w_ref[:, d:2 * d])
    v = _mm(xn, w_ref[:, 2 * d:])
    if k_ref.shape == k.shape:
        k_ref[...] = k
        v_ref[...] = v
    else:
        k_ref[...] = k.T
        v_ref[...] = v.T
    kb_ref[...] = k.astype(BF16)
    vb_ref[...] = v.astype(BF16)


def _qkv(h, g, w, tm, q_dtype=BF16, seq=None):
    m, d = h.shape
    row = pl.BlockSpec((tm, d), lambda i: (i, 0))
    if seq is None:
        kv_spec, kv_shape = row, jax.ShapeDtypeStruct((m, d), F32)
    else:
        per = seq // tm
        kv_spec = pl.BlockSpec((None, d, tm), lambda i: (i // per, 0, i % per))
        kv_shape = jax.ShapeDtypeStruct((m // seq, d, seq), F32)
    return pl.pallas_call(
        _qkv_kernel,
        grid=(m // tm,),
        in_specs=[row, _resident((1, d)), _resident(w.shape)],
        out_specs=[kv_spec, kv_spec, row, row, row],
        out_shape=[kv_shape, kv_shape, jax.ShapeDtypeStruct((m, d), q_dtype)]
        + [jax.ShapeDtypeStruct((m, d), BF16)] * 2,
        compiler_params=_cparams("arbitrary"),
        name="qkv",
    )(h, g, w)


def _sb_weights(z, sp, ls, tri, carry):
    hi = ls.astype(BF16)
    lo = (ls - hi.astype(F32)).astype(BF16)
    later = _dot(hi, tri) + _dot(lo, tri)
    return jnp.exp((z - sp) + later + carry)


def _sb_prompt_kernel(bias_ref, q_ref, k_ref, v_ref, o_ref, qs_ref, w_ref, acc_ref, carry_ref, *, hps, tk):
    tq, width = q_ref.shape
    hd = width // hps
    rows = hps * tq
    per = tq // tk
    grp = pl.program_id(1)
    i = pl.program_id(2)
    lane = lax.broadcasted_iota(jnp.int32, (tq, width), 1)
    q = q_ref[...]
    for h in range(hps):
        qs_ref[h * tq:(h + 1) * tq, :] = jnp.where((lane >= h * hd) & (lane < (h + 1) * hd), q, jnp.zeros_like(q))
    acc_ref[...] = jnp.zeros_like(acc_ref)
    carry_ref[...] = jnp.zeros_like(carry_ref)
    r2 = lax.broadcasted_iota(jnp.int32, (tk, tk), 0)
    c2 = lax.broadcasted_iota(jnp.int32, (tk, tk), 1)
    neg_tri = jnp.where(r2 > c2, -1.0, 0.0).astype(BF16)
    ahead = (lax.broadcasted_iota(jnp.int32, (rows, tk), 1)
             - lax.rem(lax.broadcasted_iota(jnp.int32, (rows, tk), 0), tq))
    bias = jnp.concatenate([jnp.full((tq, 1), bias_ref[hps * grp + h], F32) for h in range(hps)], axis=0)

    def weights(j, diag):
        z = _dot_nt(qs_ref[...], k_ref[pl.ds(pl.multiple_of(j * tk, tk), tk), :]) + bias
        sp = jnp.maximum(z, 0.0) + jnp.log(1.0 + jnp.exp2(jnp.abs(z) * -LOG2E))
        if diag:
            causal = ahead < i * tq - j * tk
            sp = jnp.where(causal, sp, 0.0)
        later = _dot(sp.astype(BF16), neg_tri)
        w = jnp.exp((z - sp) + later + carry_ref[...])
        if diag:
            w = jnp.where(causal, w, 0.0)
        w_ref[...] = w.astype(BF16)
        carry_ref[...] -= jnp.sum(sp, axis=-1, keepdims=True)

    def values(j):
        acc_ref[...] += _dot(w_ref[...], v_ref[pl.ds(pl.multiple_of(j * tk, tk), tk), :])

    top = (i + 1) * per - 1
    weights(top, True)
    for back in range(1, per):
        values(top - back + 1)
        weights(top - back, True)

    def body(jj, c):
        j = i * per - 1 - jj
        values(j + 1)
        weights(j, False)
        return c

    lax.fori_loop(0, i * per, body, 0)
    values(0)
    out = acc_ref[0:tq, :]
    for h in range(1, hps):
        out = jnp.where(lane >= h * hd, acc_ref[h * tq:(h + 1) * tq, :], out)
    o_ref[...] = out.astype(o_ref.dtype)


def _sb_prompt(qb, kb, vb, bias, tq, tk, hps=4):
    bsz, t, d = qb.shape
    width = hps * (d // N_HEADS)
    qspec = pl.BlockSpec((None, tq, width), lambda b, g, i: (b, i, g))
    kvspec = pl.BlockSpec((None, t, width), lambda b, g, i: (b, 0, g))
    return pl.pallas_call(
        functools.partial(_sb_prompt_kernel, hps=hps, tk=tk),
        grid=(bsz, d // width, t // tq),
        in_specs=[pl.BlockSpec(memory_space=pltpu.SMEM), qspec, kvspec, kvspec],
        out_specs=qspec,
        out_shape=jax.ShapeDtypeStruct((bsz, t, d), BF16),
        scratch_shapes=[pltpu.VMEM((hps * tq, width), BF16), pltpu.VMEM((hps * tq, tk), BF16),
                        pltpu.VMEM((hps * tq, width), F32),
                        pltpu.VMEM((hps * tq, 1), F32)],
        compiler_params=_cparams("arbitrary", "arbitrary", "arbitrary"),
        name="sb_prompt",
    )(bias, qb, kb, vb)


def _sb_decode_kernel(pt_ref, q_ref, bias_ref, *refs, pg):
    del pt_ref
    k_refs, v_refs = refs[:pg], refs[pg:2 * pg]
    o_ref, qb_ref, acc_ref, carry_ref = refs[2 * pg:]
    nh, _, page = k_refs[0].shape
    g = pl.program_id(1)

    @pl.when(g == 0)
    def _():
        acc_ref[...] = jnp.zeros_like(acc_ref)
        carry_ref[...] = jnp.zeros_like(carry_ref)
        qb_ref[...] = jnp.broadcast_to(q_ref[...], qb_ref.shape)

    row = lax.broadcasted_iota(jnp.int32, (page, page), 0)
    col = lax.broadcasted_iota(jnp.int32, (page, page), 1)
    tri = jnp.where(row > col, 1.0, 0.0).astype(BF16)
    z = jnp.concatenate([jnp.sum(qb_ref[h] * k_refs[r][h], axis=0, keepdims=True)
                         for r in range(pg) for h in range(nh)], axis=0)
    z = z + jnp.concatenate([bias_ref[...]] * pg, axis=0)
    sp = _softplus(z)
    ls = -sp
    total = jnp.sum(ls, axis=-1, keepdims=True)
    carries = [carry_ref[...]]
    for r in range(pg):
        carries.append(carries[-1] + total[r * nh:(r + 1) * nh])
    carry_ref[...] = carries[pg]
    w = _sb_weights(z, sp, ls, tri, jnp.concatenate(carries[:pg], axis=0))
    for h in range(nh):
        part = w[h:h + 1, :] * v_refs[0][h]
        for r in range(1, pg):
            part = part + w[r * nh + h:r * nh + h + 1, :] * v_refs[r][h]
        acc_ref[h] += part

    @pl.when(g == pl.num_programs(1) - 1)
    def _():
        o_ref[...] = jnp.sum(acc_ref[...], axis=-1)


def _sb_decode(q, cache_k, cache_v, page_table, bias, pg):
    s, nh, hd, _ = q.shape
    n_pages = page_table.shape[1]
    page = cache_k.shape[3]

    def page_spec(r):
        return pl.BlockSpec((None, nh, hd, page), lambda b, g, pt: (pt[b, n_pages - 1 - (g * pg + r)], 0, 0, 0))

    grid_spec = pltpu.PrefetchScalarGridSpec(
        num_scalar_prefetch=1,
        grid=(s, n_pages // pg),
        in_specs=[pl.BlockSpec((None, nh, hd, 1), lambda b, g, pt: (b, 0, 0, 0)),
                  pl.BlockSpec((nh, 1), lambda b, g, pt: (0, 0))] + [page_spec(r) for r in range(pg)] * 2,
        out_specs=pl.BlockSpec((None, nh, hd), lambda b, g, pt: (b, 0, 0)),
        scratch_shapes=[pltpu.VMEM((nh, hd, page), F32), pltpu.VMEM((nh, hd, page), F32), pltpu.VMEM((nh, 1), F32)],
    )
    return pl.pallas_call(
        functools.partial(_sb_decode_kernel, pg=pg),
        grid_spec=grid_spec,
        out_shape=jax.ShapeDtypeStruct((s, nh, hd), F32),
        compiler_params=_cparams("arbitrary", "arbitrary"),
        name="sb_decode",
    )(page_table, q, bias, *([cache_k] * pg), *([cache_v] * pg))


def _mix_residual(h_ref, o_ref, wo_ref, gpost_ref, gffn_ref):
    h1 = h_ref[...] + _rms(_mm(o_ref[...], wo_ref[...]), gpost_ref[...])
    return h1, _rms(h1, gffn_ref[...])


def _top2(logits):
    idx = lax.broadcasted_iota(jnp.int32, logits.shape, 1)
    m1 = jnp.max(logits, axis=-1, keepdims=True)
    i1 = jnp.min(jnp.where(logits == m1, idx, LANES), axis=-1, keepdims=True)
    rest = jnp.where(idx == i1, NEG_BIG, logits)
    m2 = jnp.max(rest, axis=-1, keepdims=True)
    i2 = jnp.min(jnp.where(rest == m2, idx, LANES), axis=-1, keepdims=True)
    e = jnp.exp(m2 - m1)
    w1 = 1.0 / (1.0 + e)
    return idx, i1, i2, w1, e * w1


def _attn_out_kernel(h_ref, o_ref, wo_ref, gpost_ref, gffn_ref, wr_ref, br_ref, h1_ref, z_ref, comb_ref):
    h1, z = _mix_residual(h_ref, o_ref, wo_ref, gpost_ref, gffn_ref)
    h1_ref[...] = h1
    z_ref[...] = z.astype(z_ref.dtype)
    idx, i1, i2, w1, w2 = _top2(_mm(z, wr_ref[...]) + br_ref[...])
    comb_ref[...] = jnp.where(idx == i1, w1, 0.0) + jnp.where(idx == i2, w2, 0.0)


def _attn_out_route_kernel(h_ref, o_ref, wo_ref, gpost_ref, gffn_ref, wr_ref, br_ref, h1_ref, z_ref, cols_ref,
                           rows_ref):
    h1, z = _mix_residual(h_ref, o_ref, wo_ref, gpost_ref, gffn_ref)
    h1_ref[...] = h1
    z_ref[...] = z
    idx, i1, i2, w1, w2 = _top2(_mm(z, wr_ref[...]) + br_ref[...])
    cols = jnp.where(idx == 0, i1.astype(F32), jnp.where(idx == 1, i2.astype(F32),
                                                         jnp.where(idx == 2, w1, jnp.where(idx == 3, w2, 0.0))))
    cols_ref[...] = cols
    pick = (lax.broadcasted_iota(jnp.int32, (8, LANES), 0) == lax.broadcasted_iota(jnp.int32, (8, LANES), 1))
    rows_ref[...] = lax.dot_general(pick.astype(F32), cols, (((1,), (1,)), ((), ())),
                                    precision=lax.Precision.HIGHEST, preferred_element_type=F32)


def _attn_out(h, o, wo, gpost, gffn, wr, br, tm, route=False):
    m, d = h.shape
    row = pl.BlockSpec((tm, d), lambda i: (i, 0))
    comb = pl.BlockSpec((tm, LANES), lambda i: (i, 0))
    in_specs = [row, row, _resident(wo.shape), _resident((1, d)), _resident((1, d)), _resident(wr.shape),
                _resident(br.shape)]
    if route:
        return pl.pallas_call(
            _attn_out_route_kernel,
            grid=(m // tm,),
            in_specs=in_specs,
            out_specs=[row, row, comb, pl.BlockSpec((8, tm), lambda i: (0, i))],
            out_shape=[jax.ShapeDtypeStruct((m, d), F32), jax.ShapeDtypeStruct((m, d), F32),
                       jax.ShapeDtypeStruct((m, LANES), F32), jax.ShapeDtypeStruct((8, m), F32)],
            compiler_params=_cparams("arbitrary"),
            name="attn_out_route",
        )(h, o, wo, gpost, gffn, wr, br)
    return pl.pallas_call(
        _attn_out_kernel,
        grid=(m // tm,),
        in_specs=in_specs,
        out_specs=[row, row, comb],
        out_shape=[jax.ShapeDtypeStruct((m, d), F32), jax.ShapeDtypeStruct((m, d), BF16),
                   jax.ShapeDtypeStruct((m, LANES), F32)],
        compiler_params=_cparams("arbitrary"),
        name="attn_out",
    )(h, o, wo, gpost, gffn, wr, br)


def _moe_kernel(z_ref, comb_ref, wg_ref, wu_ref, wd_ref, f_ref):
    e = pl.program_id(1)
    c = pl.program_id(2)

    @pl.when((e == 0) & (c == 0))
    def _():
        f_ref[...] = jnp.zeros_like(f_ref)

    z = z_ref[...]
    act = (_silu(_dot(z, wg_ref[...])) * _dot(z, wu_ref[...])).astype(BF16)
    lane = lax.broadcasted_iota(jnp.int32, comb_ref.shape, 1)
    ce = jnp.sum(jnp.where(lane == e, comb_ref[...], 0.0), axis=-1, keepdims=True)
    f_ref[...] += ce * _dot(act, wd_ref[...])


def _moe(z, comb, wgu, wd, tm, ck=512):
    m, d = z.shape
    n_e, ff, _ = wd.shape
    fc = ff // ck
    return pl.pallas_call(
        _moe_kernel,
        grid=(m // tm, n_e, fc),
        in_specs=[pl.BlockSpec((tm, d), lambda i, e, c: (i, 0)), pl.BlockSpec((tm, LANES), lambda i, e, c: (i, 0)),
                  pl.BlockSpec((None, d, ck), lambda i, e, c: (e, 0, c)),
                  pl.BlockSpec((None, d, ck), lambda i, e, c: (e, 0, fc + c)),
                  pl.BlockSpec((None, ck, d), lambda i, e, c: (e, c, 0))],
        out_specs=pl.BlockSpec((tm, d), lambda i, e, c: (i, 0)),
        out_shape=jax.ShapeDtypeStruct((m, d), F32),
        compiler_params=_cparams("arbitrary", "arbitrary", "arbitrary"),
        name="moe",
    )(z, comb, wgu, wgu, wd)


def _route_plan(i1, i2, n_e, tm, n_tiles):
    m = i1.shape[0]
    experts = jnp.arange(n_e, dtype=jnp.int32)[None, :]
    hit = (experts == i1[:, None]).astype(jnp.int32) + (experts == i2[:, None]).astype(jnp.int32)
    upto = jnp.cumsum(hit, axis=0)
    before = upto - hit
    padded = ((upto[-1] + tm - 1) // tm) * tm
    ends = jnp.cumsum(padded)
    starts = ends - padded
    slot1 = starts[i1] + jnp.take_along_axis(before, i1[:, None], axis=1)[:, 0]
    slot2 = starts[i2] + jnp.take_along_axis(before, i2[:, None], axis=1)[:, 0]
    slots = jnp.stack([slot1, slot2]).astype(jnp.int32)
    token = jnp.tile(jnp.arange(m, dtype=jnp.int32), 2)
    source = jnp.zeros((n_tiles * tm,), jnp.int32).at[slots.reshape(-1)].set(token).reshape(n_tiles, 1, tm)
    tile_start = jnp.arange(n_tiles, dtype=jnp.int32) * tm
    tile_expert = jnp.minimum(jnp.sum(tile_start[:, None] >= ends[None, :], axis=1), n_e - 1).astype(jnp.int32)
    return slots, source, tile_expert, (ends[-1:] // tm).astype(jnp.int32)


GATHER_UNROLL = 8


def _start_row_gather(src_ref, row_of, dst_ref, sem):
    rows = dst_ref.shape[0]

    def body(r, carry):
        pltpu.make_async_copy(src_ref.at[pl.ds(row_of(r), 1), :], dst_ref.at[pl.ds(r, 1), :], sem).start()
        return carry

    lax.fori_loop(0, rows, body, 0, unroll=GATHER_UNROLL)


def _wait_row_gather(src_ref, dst_ref, sem):
    pltpu.make_async_copy(src_ref.at[pl.ds(0, dst_ref.shape[0]), :], dst_ref, sem).wait()


def _experts_kernel(te_ref, nu_ref, src_ref, nxt_ref, z_ref, wg_ref, wu_ref, wd_ref, y_ref, xg_ref, xb_ref, sem):
    del te_ref
    i = pl.program_id(0)
    c = pl.program_id(1)
    n_used = nu_ref[0]
    used = i < n_used

    @pl.when(used & (c == 0))
    def _():
        buf = i % 2

        @pl.when(i == 0)
        def _():
            _start_row_gather(z_ref, lambda r: src_ref[0, r], xg_ref.at[0], sem.at[0])

        _wait_row_gather(z_ref, xg_ref.at[buf], sem.at[buf])
        xb_ref[...] = xg_ref[buf].astype(BF16)

        @pl.when(i + 1 < n_used)
        def _():
            _start_row_gather(z_ref, lambda r: nxt_ref[0, r], xg_ref.at[1 - buf], sem.at[1 - buf])

    @pl.when(used)
    def _():
        xb = xb_ref[...]
        act = (_silu(_dot(xb, wg_ref[...])) * _dot(xb, wu_ref[...])).astype(BF16)
        y = _dot(act, wd_ref[...])

        @pl.when(c == 0)
        def _():
            y_ref[...] = y

        @pl.when(c > 0)
        def _():
            y_ref[...] += y

    @pl.when(jnp.logical_not(used) & (c == 0))
    def _():
        y_ref[...] = jnp.zeros_like(y_ref)


def _experts(z, source, tile_expert, n_used, wgu, wd, ck=1792):
    n_tiles, _, tm = source.shape
    d = z.shape[1]
    ff = wd.shape[1]
    assert ff % ck == 0, (ff, ck)
    fc = ff // ck
    grid_spec = pltpu.PrefetchScalarGridSpec(
        num_scalar_prefetch=2,
        grid=(n_tiles, fc),
        in_specs=[pl.BlockSpec((None, 1, tm), lambda i, c, te, nu: (i, 0, 0), memory_space=pltpu.SMEM),
                  pl.BlockSpec((None, 1, tm), lambda i, c, te, nu: (jnp.minimum(i + 1, n_tiles - 1), 0, 0),
                               memory_space=pltpu.SMEM),
                  pl.BlockSpec(memory_space=pl.ANY),
                  pl.BlockSpec((None, d, ck), lambda i, c, te, nu: (te[i], 0, c)),
                  pl.BlockSpec((None, d, ck), lambda i, c, te, nu: (te[i], 0, fc + c)),
                  pl.BlockSpec((None, ck, d), lambda i, c, te, nu: (te[i], c, 0))],
        out_specs=pl.BlockSpec((tm, d), lambda i, c, te, nu: (i, 0)),
        scratch_shapes=[pltpu.VMEM((2, tm, d), z.dtype), pltpu.VMEM((tm, d), BF16), pltpu.SemaphoreType.DMA((2,))],
    )
    return pl.pallas_call(
        _experts_kernel,
        grid_spec=grid_spec,
        out_shape=jax.ShapeDtypeStruct((n_tiles * tm, d), F32),
        compiler_params=_cparams("arbitrary", "arbitrary"),
        name="moe_experts",
    )(tile_expert, n_used, source, source, z, wgu, wgu, wd)


def _finish_routed_kernel(slots_ref, nxt_ref, h_ref, cols_ref, p_ref, gpost_ref, gple_ref, wproj_ref, wgate_ref,
                          ys_ref, o_ref, y_ref, sem):
    i = pl.program_id(0)
    buf = i % 2

    def start(rows_ref, b):
        for k in range(TOP_K):
            _start_row_gather(ys_ref, lambda r, k=k: rows_ref[k, r], y_ref.at[b, k], sem.at[b])

    @pl.when(i == 0)
    def _():
        start(slots_ref, 0)

    for k in range(TOP_K):
        _wait_row_gather(ys_ref, y_ref.at[buf, k], sem.at[buf])

    @pl.when(i + 1 < pl.num_programs(0))
    def _():
        start(nxt_ref, 1 - buf)

    cols = cols_ref[...]
    f = cols[:, 2:3] * y_ref[buf, 0] + cols[:, 3:4] * y_ref[buf, 1]
    h2 = h_ref[...] + _rms(f, gpost_ref[...])
    e = _mm(p_ref[...], wproj_ref[...])
    g = jax.nn.sigmoid(_mm(_rms(h2, gple_ref[...]), wgate_ref[...]))
    o_ref[...] = h2 + g * e


def _finish_routed(h, ys, slots, cols, p, gpost, gple, wproj, wgate, tm):
    m, d = h.shape
    n = m // tm
    row = pl.BlockSpec((tm, d), lambda i: (i, 0))
    return pl.pallas_call(
        _finish_routed_kernel,
        grid=(n,),
        in_specs=[pl.BlockSpec((TOP_K, tm), lambda i: (0, i), memory_space=pltpu.SMEM),
                  pl.BlockSpec((TOP_K, tm), lambda i: (0, jnp.minimum(i + 1, n - 1)), memory_space=pltpu.SMEM), row,
                  pl.BlockSpec((tm, LANES), lambda i: (i, 0)), pl.BlockSpec((tm, p.shape[1]), lambda i: (i, 0)),
                  _resident((1, d)), _resident((1, d)), _resident(wproj.shape), _resident(wgate.shape),
                  pl.BlockSpec(memory_space=pl.ANY)],
        out_specs=row,
        out_shape=jax.ShapeDtypeStruct((m, d), F32),
        scratch_shapes=[pltpu.VMEM((2, TOP_K, tm, d), F32), pltpu.SemaphoreType.DMA((2,))],
        compiler_params=_cparams("arbitrary"),
        name="finish_routed",
    )(slots, slots, h, cols, p, gpost, gple, wproj, wgate, ys)


def _tile(m, pref):
    return pref if m % pref == 0 else m


def kernel(x_prompt, x_sample, state_conv, state_lru, cache_k, cache_v, page_table, p_prompt, p_sample, norm_mix_pre, norm_mix_post, norm_ffn_pre, norm_ffn_post, norm_ple, rec_w_in, rec_conv_w, rec_conv_b, rec_w_a, rec_b_a, rec_w_x, rec_b_x, rec_lambda, rec_w_out, att_w_qkv, att_w_o, att_sb_bias, ffn_w_gu, ffn_w_down, moe_w_router, moe_b_router, moe_w_gu, moe_w_down, ple_w_proj, ple_w_gate):
    bsz, t, d = x_prompt.shape
    s = x_sample.shape[0]
    depth = norm_mix_pre.shape[0]
    mp = bsz * t
    tmp = _tile(mp, 512)
    vec = lambda a: a.reshape(1, -1)
    bf = lambda a: a.astype(BF16)
    hp = x_prompt.reshape(mp, d)
    hs = x_sample.reshape(s, d)
    keep = lambda a: a
    outs = {k: [] for k in ("conv_p", "lru_p", "k_p", "v_p", "conv_s", "lru_s", "k_s", "v_s")}
    for i in range(depth):
        j = i // 2
        if i % 2 == 0:
            def rec(cast):
                return (vec(norm_mix_pre[i]), cast(rec_w_in[j]), rec_conv_w[j], vec(rec_conv_b[j]), cast(rec_w_a[j]),
                        vec(rec_b_a[j]), cast(rec_w_x[j]), vec(rec_b_x[j]), vec(rec_lambda[j]), cast(rec_w_out[j]),
                        vec(norm_mix_post[i]))
            hp, conv_new, h_new = _rec_prompt(hp.reshape(bsz, t, d), *rec(bf), tt=_tile(t, 256))
            outs["conv_p"].append(conv_new)
            outs["lru_p"].append(h_new.reshape(bsz, d))
            hs, conv_new, h_new = _rec_step(hs, jnp.swapaxes(state_conv[j], 0, 1), state_lru[j], *rec(keep))
            outs["conv_s"].append(jnp.swapaxes(conv_new, 0, 1))
            outs["lru_s"].append(h_new)
            hp = hp.reshape(mp, d)
            fp = _ffn(hp, vec(norm_ffn_pre[i]), bf(ffn_w_gu[j]), bf(ffn_w_down[j]), tmp)
            fs = _ffn(hs, vec(norm_ffn_pre[i]), ffn_w_gu[j], ffn_w_down[j], s)
        else:
            hd = d // N_HEADS
            kp, vp, qb, kb, vb = _qkv(hp, vec(norm_mix_pre[i]), bf(att_w_qkv[j]), _tile(t, 512), seq=t)
            outs["k_p"].append(kp.reshape(bsz, N_HEADS, hd, t).transpose(0, 3, 1, 2))
            outs["v_p"].append(vp.reshape(bsz, N_HEADS, hd, t).transpose(0, 3, 1, 2))
            tk = _tile(t, 256)
            op = _sb_prompt(qb.reshape(bsz, t, d), kb.reshape(bsz, t, d), vb.reshape(bsz, t, d), att_sb_bias[j],
                            tq=_tile(t, 2 * tk), tk=tk)
            ks, vs, qs, _, _ = _qkv(hs, vec(norm_mix_pre[i]), att_w_qkv[j], s, q_dtype=F32)
            outs["k_s"].append(ks.reshape(s, 1, N_HEADS, hd))
            outs["v_s"].append(vs.reshape(s, 1, N_HEADS, hd))
            n_pages = page_table.shape[1]
            os_ = _sb_decode(qs.reshape(s, N_HEADS, hd, 1), jnp.transpose(cache_k[j], (0, 2, 3, 1)),
                             jnp.transpose(cache_v[j], (0, 2, 3, 1)), page_table, att_sb_bias[j].reshape(N_HEADS, 1),
                             pg=8 if n_pages % 8 == 0 else 1)
            wr = jnp.pad(moe_w_router[j], ((0, 0), (0, LANES - N_EXPERTS)))
            br = jnp.pad(moe_b_router[j], (0, LANES - N_EXPERTS), constant_values=NEG_BIG).reshape(1, LANES)
            norms = (vec(norm_mix_post[i]), vec(norm_ffn_pre[i]))
            hp, zp, cols, rows = _attn_out(hp, op.reshape(mp, d), bf(att_w_o[j]), *norms, bf(wr), br, tmp, route=True)
            hs, zs, cs = _attn_out(hs, os_.reshape(s, d), att_w_o[j], *norms, wr, br, s)
            wgu, wd = bf(moe_w_gu[j]), bf(moe_w_down[j])
            fs = _moe(zs, cs, wgu, wd, s)
            tme = _tile(mp, 512)
            n_tiles = (TOP_K * mp + N_EXPERTS * (tme - 1)) // tme
            slots, source, tile_expert, n_used = _route_plan(rows[0].astype(jnp.int32), rows[1].astype(jnp.int32),
                                                             N_EXPERTS, tme, n_tiles)
            ys = _experts(zp, source, tile_expert, n_used, wgu, wd)
        tail = (vec(norm_ffn_post[i]), vec(norm_ple[i]))
        if i % 2 == 0:
            hp = _finish(hp, fp, p_prompt[i].reshape(mp, -1), *tail, bf(ple_w_proj[i]), bf(ple_w_gate[i]), tmp)
        else:
            hp = _finish_routed(hp, ys, slots, cols, p_prompt[i].reshape(mp, -1), *tail, bf(ple_w_proj[i]),
                                bf(ple_w_gate[i]), tmp)
        hs = _finish(hs, fs, p_sample[i].reshape(s, -1), *tail, ple_w_proj[i], ple_w_gate[i], s)
    st = lambda k: jnp.stack(outs[k])
    return (hp.reshape(bsz, t, d), hs.reshape(s, 1, d), st("conv_p"), st("lru_p"), st("k_p"), st("v_p"),
            st("conv_s"), st("lru_s"), st("k_s"), st("v_s"))
```

```python
import functools
import math

import jax
import jax.numpy as jnp
from jax import lax
from jax.experimental import pallas as pl
from jax.experimental.pallas import tpu as pltpu

F32 = jnp.float32
BF16 = jnp.bfloat16

RMS_EPS = 1e-6
LRU_C = 8.0
N_HEADS = 16
N_LRU_BLOCKS = 8
CONV_W = 4
N_EXPERTS = 8
TOP_K = 2
LANES = 128
SUBLANES = 8
LOG2E = 1.4426950408889634
NEG_BIG = -1e30
VMEM_LIMIT = 56 * 1024 * 1024


def _cparams(*sem):
    return pltpu.CompilerParams(dimension_semantics=sem, vmem_limit_bytes=VMEM_LIMIT)


def _resident(shape):
    return pl.BlockSpec(shape, lambda *_: (0,) * len(shape), pipeline_mode=pl.Buffered(1))


def _dot(a, b):
    return jnp.dot(a, b, preferred_element_type=F32)


def _mm(x, w):
    if w.dtype == F32:
        return jnp.dot(x.astype(F32), w, precision=lax.Precision.HIGHEST, preferred_element_type=F32)
    return jnp.dot(x.astype(w.dtype), w, preferred_element_type=F32)


def _dot_nt(a, b):
    return lax.dot_general(a, b, (((1,), (1,)), ((), ())), preferred_element_type=F32)


def _rms(x, g):
    return x * lax.rsqrt(jnp.mean(x * x, axis=-1, keepdims=True) + RMS_EPS) * g


def _softplus(x):
    return jnp.maximum(x, 0.0) + jnp.log1p(jnp.exp(-jnp.abs(x)))


def _gelu(x):
    c = math.sqrt(2.0 / math.pi)
    return 0.5 * x * (1.0 + jnp.tanh(c * (x + 0.044715 * (x * x * x))))


def _silu(x):
    return x * jax.nn.sigmoid(x)


def _lru_gates(xc, wa_ref, ba, wx_ref, bx, lam):
    blk = xc.shape[1] // N_LRU_BLOCKS
    ra, rx = [], []
    for n in range(N_LRU_BLOCKS):
        xb = xc[:, n * blk:(n + 1) * blk]
        ra.append(_mm(xb, wa_ref[n]))
        rx.append(_mm(xb, wx_ref[n]))
    r = jax.nn.sigmoid(jnp.concatenate(ra, axis=1) + ba)
    ig = jax.nn.sigmoid(jnp.concatenate(rx, axis=1) + bx)
    log_a = -LRU_C * r * _softplus(-lam)
    a = jnp.exp(log_a)
    gx = jnp.sqrt(-jnp.tanh(log_a) * (a * a + 1.0)) * ig * xc
    return a, gx


def _rec_prompt_kernel(x_ref, gpre_ref, win_ref, cw_ref, cb_ref, wa_ref, ba_ref, wx_ref, bx_ref, lam_ref,
                       wout_ref, gpost_ref, h_ref, conv_ref, hlast_ref, tail_ref, hc_ref):
    tt, d = x_ref.shape

    @pl.when(pl.program_id(1) == 0)
    def _():
        tail_ref[...] = jnp.zeros_like(tail_ref)
        hc_ref[...] = jnp.zeros_like(hc_ref)

    x = x_ref[...]
    xn = _rms(x, gpre_ref[...]).astype(win_ref.dtype)
    gate = _gelu(_mm(xn, win_ref[:, :d]))
    xr = _mm(xn, win_ref[:, d:])
    xpad = jnp.concatenate([tail_ref[...], xr], axis=0)
    xc = cb_ref[...] + cw_ref[3:4, :] * xr
    for back in range(1, CONV_W):
        xc = xc + cw_ref[CONV_W - 1 - back:CONV_W - back, :] * xpad[8 - back:8 - back + tt]
    tail_ref[...] = xr[tt - 8:]
    a, b = _lru_gates(xc, wa_ref, ba_ref[...], wx_ref, bx_ref[...], lam_ref[...])
    row = lax.rem(lax.broadcasted_iota(jnp.int32, (tt, 1), 0), SUBLANES)
    s = 1
    while s < SUBLANES:
        keep = row >= s
        a_sh = jnp.where(keep, pltpu.roll(a, s, 0), 1.0)
        b_sh = jnp.where(keep, pltpu.roll(b, s, 0), 0.0)
        b = a * b_sh + b
        a = a * a_sh
        s *= 2
    last = hc_ref[...]
    groups = []
    for g in range(tt // SUBLANES):
        rows = slice(g * SUBLANES, (g + 1) * SUBLANES)
        hg = b[rows] + a[rows] * last
        groups.append(hg)
        last = hg[SUBLANES - 1:]
    h = jnp.concatenate(groups, axis=0)
    hc_ref[...] = last
    y = _mm(h * gate, wout_ref[...])
    h_ref[...] = x + _rms(y, gpost_ref[...])
    conv_ref[...] = xr[tt - (CONV_W - 1):]
    hlast_ref[...] = h[tt - 1:]


def _rec_prompt(x, gpre, win, cw, cb, wa, ba, wx, bx, lam, wout, gpost, tt):
    bsz, t, d = x.shape
    vec = _resident((1, d))
    return pl.pallas_call(
        _rec_prompt_kernel,
        grid=(bsz, t // tt),
        in_specs=[pl.BlockSpec((None, tt, d), lambda b, i: (b, i, 0)), vec, _resident((d, 2 * d)),
                  _resident((CONV_W, d)), vec, _resident(wa.shape), vec, _resident(wx.shape), vec, vec,
                  _resident((d, d)), vec],
        out_specs=[pl.BlockSpec((None, tt, d), lambda b, i: (b, i, 0)),
                   pl.BlockSpec((None, CONV_W - 1, d), lambda b, i: (b, 0, 0)),
                   pl.BlockSpec((None, 1, d), lambda b, i: (b, 0, 0))],
        out_shape=[jax.ShapeDtypeStruct((bsz, t, d), F32), jax.ShapeDtypeStruct((bsz, CONV_W - 1, d), F32),
                   jax.ShapeDtypeStruct((bsz, 1, d), F32)],
        scratch_shapes=[pltpu.VMEM((8, d), F32), pltpu.VMEM((1, d), F32)],
        compiler_params=_cparams("arbitrary", "arbitrary"),
        name="rec_prompt",
    )(x, gpre, win, cw, cb, wa, ba, wx, bx, lam, wout, gpost)


def _rec_step_kernel(x_ref, sc_ref, h0_ref, gpre_ref, win_ref, cw_ref, cb_ref, wa_ref, ba_ref, wx_ref, bx_ref,
                     lam_ref, wout_ref, gpost_ref, h_ref, conv_ref, hnew_ref):
    d = x_ref.shape[1]
    x = x_ref[...]
    xn = _rms(x, gpre_ref[...]).astype(win_ref.dtype)
    gate = _gelu(_mm(xn, win_ref[:, :d]))
    xr = _mm(xn, win_ref[:, d:])
    xc = cb_ref[...] + cw_ref[CONV_W - 1:CONV_W, :] * xr
    for k in range(CONV_W - 1):
        xc = xc + cw_ref[k:k + 1, :] * sc_ref[k]
    a, b = _lru_gates(xc, wa_ref, ba_ref[...], wx_ref, bx_ref[...], lam_ref[...])
    h = a * h0_ref[...] + b
    y = _mm(h * gate, wout_ref[...])
    h_ref[...] = x + _rms(y, gpost_ref[...])
    for k in range(CONV_W - 2):
        conv_ref[k] = sc_ref[k + 1]
    conv_ref[CONV_W - 2] = xr
    hnew_ref[...] = h


def _rec_step(x, sc, h0, gpre, win, cw, cb, wa, ba, wx, bx, lam, wout, gpost):
    rows, d = x.shape
    return pl.pallas_call(
        _rec_step_kernel,
        out_shape=[jax.ShapeDtypeStruct((rows, d), F32), jax.ShapeDtypeStruct((CONV_W - 1, rows, d), F32),
                   jax.ShapeDtypeStruct((rows, d), F32)],
        compiler_params=pltpu.CompilerParams(vmem_limit_bytes=VMEM_LIMIT),
        name="rec_step",
    )(x, sc, h0, gpre, win, cw, cb, wa, ba, wx, bx, lam, wout, gpost)


def _layer_tail(h, f, p_ref, gpost_ref, gple_ref, wproj_ref, wgate_ref):
    h2 = h + _rms(f, gpost_ref[...])
    e = _mm(p_ref[...], wproj_ref[...])
    g = jax.nn.sigmoid(_mm(_rms(h2, gple_ref[...]), wgate_ref[...]))
    return h2 + g * e


def _ffn_layer_kernel(h_ref, g_ref, wgu_ref, wd_ref, p_ref, gpost_ref, gple_ref, wproj_ref, wgate_ref, o_ref, *, ck):
    ff = wd_ref.shape[0]
    h = h_ref[...]
    z = _rms(h, g_ref[...]).astype(wgu_ref.dtype)
    acc = jnp.zeros(h.shape, F32)
    for c in range(ff // ck):
        g = _mm(z, wgu_ref[:, c * ck:(c + 1) * ck])
        u = _mm(z, wgu_ref[:, ff + c * ck:ff + (c + 1) * ck])
        acc = acc + _mm(_silu(g) * u, wd_ref[c * ck:(c + 1) * ck, :])
    o_ref[...] = _layer_tail(h, acc, p_ref, gpost_ref, gple_ref, wproj_ref, wgate_ref)


def _p_spec(p, layer, tm):
    return pl.BlockSpec((None, tm, p.shape[2]), lambda i: (layer, i, 0))


def _ffn_layer(h, g, wgu, wd, p, layer, gpost, gple, wproj, wgate, tm, ck=512):
    m, d = h.shape
    row = pl.BlockSpec((tm, d), lambda i: (i, 0))
    vec = _resident((1, d))
    return pl.pallas_call(
        functools.partial(_ffn_layer_kernel, ck=ck),
        grid=(m // tm,),
        in_specs=[row, vec, _resident(wgu.shape), _resident(wd.shape), _p_spec(p, layer, tm), vec, vec,
                  _resident(wproj.shape), _resident(wgate.shape)],
        out_specs=row,
        out_shape=jax.ShapeDtypeStruct((m, d), F32),
        compiler_params=_cparams("arbitrary"),
        name="ffn_layer",
    )(h, g, wgu, wd, p, gpost, gple, wproj, wgate)


def _finish_kernel(h_ref, f_ref, p_ref, gpost_ref, gple_ref, wproj_ref, wgate_ref, o_ref):
    o_ref[...] = _layer_tail(h_ref[...], f_ref[...], p_ref, gpost_ref, gple_ref, wproj_ref, wgate_ref)


def _finish(h, f, p, layer, gpost, gple, wproj, wgate, tm):
    m, d = h.shape
    row = pl.BlockSpec((tm, d), lambda i: (i, 0))
    return pl.pallas_call(
        _finish_kernel,
        grid=(m // tm,),
        in_specs=[row, row, _p_spec(p, layer, tm), _resident((1, d)), _resident((1, d)),
                  _resident(wproj.shape), _resident(wgate.shape)],
        out_specs=row,
        out_shape=jax.ShapeDtypeStruct((m, d), F32),
        compiler_params=_cparams("arbitrary"),
        name="finish",
    )(h, f, p, gpost, gple, wproj, wgate)


def _qkv_kernel(h_ref, g_ref, w_ref, k_ref, v_ref, qb_ref, kb_ref, vb_ref):
    d = h_ref.shape[1]
    xn = _rms(h_ref[...], g_ref[...]).astype(w_ref.dtype)
    scale = (d // N_HEADS) ** -0.5
    qb_ref[...] = (_mm(xn, w_ref[:, :d]) * scale).astype(qb_ref.dtype)
    k = _mm(xn, w_ref[:, d:2 * d])
    v = _mm(xn, w_ref[:, 2 * d:])
    if k_ref.shape == k.shape:
        k_ref[...] = k
        v_ref[...] = v
    else:
        k_ref[...] = k.T
        v_ref[...] = v.T
    kb_ref[...] = k.astype(BF16)
    vb_ref[...] = v.astype(BF16)


def _qkv(h, g, w, tm, q_dtype=BF16, seq=None):
    m, d = h.shape
    row = pl.BlockSpec((tm, d), lambda i: (i, 0))
    if seq is None:
        kv_spec, kv_shape = row, jax.ShapeDtypeStruct((m, d), F32)
    else:
        per = seq // tm
        kv_spec = pl.BlockSpec((None, d, tm), lambda i: (i // per, 0, i % per))
        kv_shape = jax.ShapeDtypeStruct((m // seq, d, seq), F32)
    return pl.pallas_call(
        _qkv_kernel,
        grid=(m // tm,),
        in_specs=[row, _resident((1, d)), _resident(w.shape)],
        out_specs=[kv_spec, kv_spec, row, row, row],
        out_shape=[kv_shape, kv_shape, jax.ShapeDtypeStruct((m, d), q_dtype)]
        + [jax.ShapeDtypeStruct((m, d), BF16)] * 2,
        compiler_params=_cparams("arbitrary"),
        name="qkv",
    )(h, g, w)


def _sb_weights(z, sp, ls, tri, carry):
    hi = ls.astype(BF16)
    lo = (ls - hi.astype(F32)).astype(BF16)
    later = _dot(hi, tri) + _dot(lo, tri)
    return jnp.exp((z - sp) + later + carry)


def _sb_prompt_kernel(bias_ref, q_ref, k_ref, v_ref, o_ref, qs_ref, w_ref, acc_ref, carry_ref, *, hps, tk):
    tq, width = q_ref.shape
    hd = width // hps
    rows = hps * tq
    per = tq // tk
    grp = pl.program_id(1)
    i = pl.program_id(2)
    lane = lax.broadcasted_iota(jnp.int32, (tq, width), 1)
    q = q_ref[...]
    for h in range(hps):
        qs_ref[h * tq:(h + 1) * tq, :] = jnp.where((lane >= h * hd) & (lane < (h + 1) * hd), q, jnp.zeros_like(q))
    acc_ref[...] = jnp.zeros_like(acc_ref)
    carry_ref[...] = jnp.zeros_like(carry_ref)
    r2 = lax.broadcasted_iota(jnp.int32, (tk, tk), 0)
    c2 = lax.broadcasted_iota(jnp.int32, (tk, tk), 1)
    neg_tri = jnp.where(r2 > c2, -1.0, 0.0).astype(BF16)
    ahead = (lax.broadcasted_iota(jnp.int32, (rows, tk), 1)
             - lax.rem(lax.broadcasted_iota(jnp.int32, (rows, tk), 0), tq))
    bias = jnp.concatenate([jnp.full((tq, 1), bias_ref[hps * grp + h], F32) for h in range(hps)], axis=0)

    def weights(j, diag):
        z = _dot_nt(qs_ref[...], k_ref[pl.ds(pl.multiple_of(j * tk, tk), tk), :]) + bias
        sp = jnp.maximum(z, 0.0) + jnp.log(1.0 + jnp.exp2(jnp.abs(z) * -LOG2E))
        if diag:
            causal = ahead < i * tq - j * tk
            sp = jnp.where(causal, sp, 0.0)
        later = _dot(sp.astype(BF16), neg_tri)
        w = jnp.exp((z - sp) + later + carry_ref[...])
        if diag:
            w = jnp.where(causal, w, 0.0)
        w_ref[...] = w.astype(BF16)
        carry_ref[...] -= jnp.sum(sp, axis=-1, keepdims=True)

    def values(j):
        acc_ref[...] += _dot(w_ref[...], v_ref[pl.ds(pl.multiple_of(j * tk, tk), tk), :])

    top = (i + 1) * per - 1
    weights(top, True)
    for back in range(1, per):
        values(top - back + 1)
        weights(top - back, True)

    def body(jj, c):
        j = i * per - 1 - jj
        values(j + 1)
        weights(j, False)
        return c

    lax.fori_loop(0, i * per, body, 0)
    values(0)
    out = acc_ref[0:tq, :]
    for h in range(1, hps):
        out = jnp.where(lane >= h * hd, acc_ref[h * tq:(h + 1) * tq, :], out)
    o_ref[...] = out.astype(o_ref.dtype)


def _sb_prompt(qb, kb, vb, bias, tq, tk, hps=4):
    bsz, t, d = qb.shape
    width = hps * (d // N_HEADS)
    qspec = pl.BlockSpec((None, tq, width), lambda b, g, i: (b, i, g))
    kvspec = pl.BlockSpec((None, t, width), lambda b, g, i: (b, 0, g))
    return pl.pallas_call(
        functools.partial(_sb_prompt_kernel, hps=hps, tk=tk),
        grid=(bsz, d // width, t // tq),
        in_specs=[pl.BlockSpec(memory_space=pltpu.SMEM), qspec, kvspec, kvspec],
        out_specs=qspec,
        out_shape=jax.ShapeDtypeStruct((bsz, t, d), BF16),
        scratch_shapes=[pltpu.VMEM((hps * tq, width), BF16), pltpu.VMEM((hps * tq, tk), BF16),
                        pltpu.VMEM((hps * tq, width), F32),
                        pltpu.VMEM((hps * tq, 1), F32)],
        compiler_params=_cparams("arbitrary", "arbitrary", "arbitrary"),
        name="sb_prompt",
    )(bias, qb, kb, vb)


def _sb_decode_kernel(pt_ref, q_ref, bias_ref, *refs, pg):
    del pt_ref
    k_refs, v_refs = refs[:pg], refs[pg:2 * pg]
    o_ref, qb_ref, acc_ref, carry_ref = refs[2 * pg:]
    nh, _, page = k_refs[0].shape
    g = pl.program_id(1)

    @pl.when(g == 0)
    def _():
        acc_ref[...] = jnp.zeros_like(acc_ref)
        carry_ref[...] = jnp.zeros_like(carry_ref)
        qb_ref[...] = jnp.broadcast_to(q_ref[...], qb_ref.shape)

    row = lax.broadcasted_iota(jnp.int32, (page, page), 0)
    col = lax.broadcasted_iota(jnp.int32, (page, page), 1)
    tri = jnp.where(row > col, 1.0, 0.0).astype(BF16)
    z = jnp.concatenate([jnp.sum(qb_ref[h] * k_refs[r][h], axis=0, keepdims=True)
                         for r in range(pg) for h in range(nh)], axis=0)
    z = z + jnp.concatenate([bias_ref[...]] * pg, axis=0)
    sp = _softplus(z)
    ls = -sp
    total = jnp.sum(ls, axis=-1, keepdims=True)
    carries = [carry_ref[...]]
    for r in range(pg):
        carries.append(carries[-1] + total[r * nh:(r + 1) * nh])
    carry_ref[...] = carries[pg]
    w = _sb_weights(z, sp, ls, tri, jnp.concatenate(carries[:pg], axis=0))
    for h in range(nh):
        part = w[h:h + 1, :] * v_refs[0][h]
        for r in range(1, pg):
            part = part + w[r * nh + h:r * nh + h + 1, :] * v_refs[r][h]
        acc_ref[h] += part

    @pl.when(g == pl.num_programs(1) - 1)
    def _():
        o_ref[...] = jnp.sum(acc_ref[...], axis=-1)


def _sb_decode(q, cache_k, cache_v, page_table, bias, pg):
    s, nh, hd, _ = q.shape
    n_pages = page_table.shape[1]
    page = cache_k.shape[3]

    def page_spec(r):
        return pl.BlockSpec((None, nh, hd, page), lambda b, g, pt: (pt[b, n_pages - 1 - (g * pg + r)], 0, 0, 0))

    grid_spec = pltpu.PrefetchScalarGridSpec(
        num_scalar_prefetch=1,
        grid=(s, n_pages // pg),
        in_specs=[pl.BlockSpec((None, nh, hd, 1), lambda b, g, pt: (b, 0, 0, 0)),
                  pl.BlockSpec((nh, 1), lambda b, g, pt: (0, 0))] + [page_spec(r) for r in range(pg)] * 2,
        out_specs=pl.BlockSpec((None, nh, hd), lambda b, g, pt: (b, 0, 0)),
        scratch_shapes=[pltpu.VMEM((nh, hd, page), F32), pltpu.VMEM((nh, hd, page), F32), pltpu.VMEM((nh, 1), F32)],
    )
    return pl.pallas_call(
        functools.partial(_sb_decode_kernel, pg=pg),
        grid_spec=grid_spec,
        out_shape=jax.ShapeDtypeStruct((s, nh, hd), F32),
        compiler_params=_cparams("arbitrary", "arbitrary"),
        name="sb_decode",
    )(page_table, q, bias, *([cache_k] * pg), *([cache_v] * pg))


def _mix_residual(h_ref, o_ref, wo_ref, gpost_ref, gffn_ref):
    h1 = h_ref[...] + _rms(_mm(o_ref[...], wo_ref[...]), gpost_ref[...])
    return h1, _rms(h1, gffn_ref[...])


def _top2(logits):
    idx = lax.broadcasted_iota(jnp.int32, logits.shape, 1)
    m1 = jnp.max(logits, axis=-1, keepdims=True)
    i1 = jnp.min(jnp.where(logits == m1, idx, LANES), axis=-1, keepdims=True)
    rest = jnp.where(idx == i1, NEG_BIG, logits)
    m2 = jnp.max(rest, axis=-1, keepdims=True)
    i2 = jnp.min(jnp.where(rest == m2, idx, LANES), axis=-1, keepdims=True)
    e = jnp.exp(m2 - m1)
    w1 = 1.0 / (1.0 + e)
    return idx, i1, i2, w1, e * w1


def _attn_out_kernel(h_ref, o_ref, wo_ref, gpost_ref, gffn_ref, wr_ref, br_ref, h1_ref, z_ref, comb_ref):
    h1, z = _mix_residual(h_ref, o_ref, wo_ref, gpost_ref, gffn_ref)
    h1_ref[...] = h1
    z_ref[...] = z.astype(z_ref.dtype)
    idx, i1, i2, w1, w2 = _top2(_mm(z, wr_ref[...]) + br_ref[...])
    comb_ref[...] = jnp.where(idx == i1, w1, 0.0) + jnp.where(idx == i2, w2, 0.0)


def _attn_out_route_kernel(h_ref, o_ref, wo_ref, gpost_ref, gffn_ref, wr_ref, br_ref, h1_ref, z_ref, cols_ref,
                           rows_ref):
    h1, z = _mix_residual(h_ref, o_ref, wo_ref, gpost_ref, gffn_ref)
    h1_ref[...] = h1
    z_ref[...] = z
    idx, i1, i2, w1, w2 = _top2(_mm(z, wr_ref[...]) + br_ref[...])
    cols = jnp.where(idx == 0, i1.astype(F32), jnp.where(idx == 1, i2.astype(F32),
                                                         jnp.where(idx == 2, w1, jnp.where(idx == 3, w2, 0.0))))
    cols_ref[...] = cols
    pick = (lax.broadcasted_iota(jnp.int32, (8, LANES), 0) == lax.broadcasted_iota(jnp.int32, (8, LANES), 1))
    rows_ref[...] = lax.dot_general(pick.astype(F32), cols, (((1,), (1,)), ((), ())),
                                    precision=lax.Precision.HIGHEST, preferred_element_type=F32)


def _attn_out(h, o, wo, gpost, gffn, wr, br, tm, route=False):
    m, d = h.shape
    row = pl.BlockSpec((tm, d), lambda i: (i, 0))
    comb = pl.BlockSpec((tm, LANES), lambda i: (i, 0))
    in_specs = [row, row, _resident(wo.shape), _resident((1, d)), _resident((1, d)), _resident(wr.shape),
                _resident(br.shape)]
    if route:
        return pl.pallas_call(
            _attn_out_route_kernel,
            grid=(m // tm,),
            in_specs=in_specs,
            out_specs=[row, row, comb, pl.BlockSpec((8, tm), lambda i: (0, i))],
            out_shape=[jax.ShapeDtypeStruct((m, d), F32), jax.ShapeDtypeStruct((m, d), F32),
                       jax.ShapeDtypeStruct((m, LANES), F32), jax.ShapeDtypeStruct((8, m), F32)],
            compiler_params=_cparams("arbitrary"),
            name="attn_out_route",
        )(h, o, wo, gpost, gffn, wr, br)
    return pl.pallas_call(
        _attn_out_kernel,
        grid=(m // tm,),
        in_specs=in_specs,
        out_specs=[row, row, comb],
        out_shape=[jax.ShapeDtypeStruct((m, d), F32), jax.ShapeDtypeStruct((m, d), BF16),
                   jax.ShapeDtypeStruct((m, LANES), F32)],
        compiler_params=_cparams("arbitrary"),
        name="attn_out",
    )(h, o, wo, gpost, gffn, wr, br)


def _moe_kernel(z_ref, comb_ref, wg_ref, wu_ref, wd_ref, f_ref):
    e = pl.program_id(1)
    c = pl.program_id(2)

    @pl.when((e == 0) & (c == 0))
    def _():
        f_ref[...] = jnp.zeros_like(f_ref)

    z = z_ref[...]
    act = (_silu(_dot(z, wg_ref[...])) * _dot(z, wu_ref[...])).astype(BF16)
    lane = lax.broadcasted_iota(jnp.int32, comb_ref.shape, 1)
    ce = jnp.sum(jnp.where(lane == e, comb_ref[...], 0.0), axis=-1, keepdims=True)
    f_ref[...] += ce * _dot(act, wd_ref[...])


def _moe(z, comb, wgu, wd, tm, ck=512):
    m, d = z.shape
    n_e, ff, _ = wd.shape
    fc = ff // ck
    return pl.pallas_call(
        _moe_kernel,
        grid=(m // tm, n_e, fc),
        in_specs=[pl.BlockSpec((tm, d), lambda i, e, c: (i, 0)), pl.BlockSpec((tm, LANES), lambda i, e, c: (i, 0)),
                  pl.BlockSpec((None, d, ck), lambda i, e, c: (e, 0, c)),
                  pl.BlockSpec((None, d, ck), lambda i, e, c: (e, 0, fc + c)),
                  pl.BlockSpec((None, ck, d), lambda i, e, c: (e, c, 0))],
        out_specs=pl.BlockSpec((tm, d), lambda i, e, c: (i, 0)),
        out_shape=jax.ShapeDtypeStruct((m, d), F32),
        compiler_params=_cparams("arbitrary", "arbitrary", "arbitrary"),
        name="moe",
    )(z, comb, wgu, wgu, wd)


def _route_plan(i1, i2, n_e, tm, n_tiles):
    m = i1.shape[0]
    experts = jnp.arange(n_e, dtype=jnp.int32)[None, :]
    hit = (experts == i1[:, None]).astype(jnp.int32) + (experts == i2[:, None]).astype(jnp.int32)
    upto = jnp.cumsum(hit, axis=0)
    before = upto - hit
    padded = ((upto[-1] + tm - 1) // tm) * tm
    ends = jnp.cumsum(padded)
    starts = ends - padded
    slot1 = starts[i1] + jnp.take_along_axis(before, i1[:, None], axis=1)[:, 0]
    slot2 = starts[i2] + jnp.take_along_axis(before, i2[:, None], axis=1)[:, 0]
    slots = jnp.stack([slot1, slot2]).astype(jnp.int32)
    token = jnp.tile(jnp.arange(m, dtype=jnp.int32), 2)
    source = jnp.zeros((n_tiles * tm,), jnp.int32).at[slots.reshape(-1)].set(token).reshape(n_tiles, 1, tm)
    tile_start = jnp.arange(n_tiles, dtype=jnp.int32) * tm
    tile_expert = jnp.minimum(jnp.sum(tile_start[:, None] >= ends[None, :], axis=1), n_e - 1).astype(jnp.int32)
    return slots, source, tile_expert, (ends[-1:] // tm).astype(jnp.int32)


GATHER_UNROLL = 8


def _start_row_gather(src_ref, row_of, dst_ref, sem):
    rows = dst_ref.shape[0]

    def body(r, carry):
        pltpu.make_async_copy(src_ref.at[pl.ds(row_of(r), 1), :], dst_ref.at[pl.ds(r, 1), :], sem).start()
        return carry

    lax.fori_loop(0, rows, body, 0, unroll=GATHER_UNROLL)


def _wait_row_gather(src_ref, dst_ref, sem):
    pltpu.make_async_copy(src_ref.at[pl.ds(0, dst_ref.shape[0]), :], dst_ref, sem).wait()


def _experts_kernel(te_ref, nu_ref, src_ref, nxt_ref, z_ref, wg_ref, wu_ref, wd_ref, y_ref, xg_ref, xb_ref, sem):
    del te_ref
    i = pl.program_id(0)
    c = pl.program_id(1)
    n_used = nu_ref[0]
    used = i < n_used

    @pl.when(used & (c == 0))
    def _():
        buf = i % 2

        @pl.when(i == 0)
        def _():
            _start_row_gather(z_ref, lambda r: src_ref[0, r], xg_ref.at[0], sem.at[0])

        _wait_row_gather(z_ref, xg_ref.at[buf], sem.at[buf])
        xb_ref[...] = xg_ref[buf].astype(BF16)

        @pl.when(i + 1 < n_used)
        def _():
            _start_row_gather(z_ref, lambda r: nxt_ref[0, r], xg_ref.at[1 - buf], sem.at[1 - buf])

    @pl.when(used)
    def _():
        xb = xb_ref[...]
        act = (_silu(_dot(xb, wg_ref[...])) * _dot(xb, wu_ref[...])).astype(BF16)
        y = _dot(act, wd_ref[...])

        @pl.when(c == 0)
        def _():
            y_ref[...] = y

        @pl.when(c > 0)
        def _():
            y_ref[...] += y

    @pl.when(jnp.logical_not(used) & (c == 0))
    def _():
        y_ref[...] = jnp.zeros_like(y_ref)


def _experts(z, source, tile_expert, n_used, wgu, wd, ck=1792):
    n_tiles, _, tm = source.shape
    d = z.shape[1]
    ff = wd.shape[1]
    assert ff % ck == 0, (ff, ck)
    fc = ff // ck
    grid_spec = pltpu.PrefetchScalarGridSpec(
        num_scalar_prefetch=2,
        grid=(n_tiles, fc),
        in_specs=[pl.BlockSpec((None, 1, tm), lambda i, c, te, nu: (i, 0, 0), memory_space=pltpu.SMEM),
                  pl.BlockSpec((None, 1, tm), lambda i, c, te, nu: (jnp.minimum(i + 1, n_tiles - 1), 0, 0),
                               memory_space=pltpu.SMEM),
                  pl.BlockSpec(memory_space=pl.ANY),
                  pl.BlockSpec((None, d, ck), lambda i, c, te, nu: (te[i], 0, c)),
                  pl.BlockSpec((None, d, ck), lambda i, c, te, nu: (te[i], 0, fc + c)),
                  pl.BlockSpec((None, ck, d), lambda i, c, te, nu: (te[i], c, 0))],
        out_specs=pl.BlockSpec((tm, d), lambda i, c, te, nu: (i, 0)),
        scratch_shapes=[pltpu.VMEM((2, tm, d), z.dtype), pltpu.VMEM((tm, d), BF16), pltpu.SemaphoreType.DMA((2,))],
    )
    return pl.pallas_call(
        _experts_kernel,
        grid_spec=grid_spec,
        out_shape=jax.ShapeDtypeStruct((n_tiles * tm, d), F32),
        compiler_params=_cparams("arbitrary", "arbitrary"),
        name="moe_experts",
    )(tile_expert, n_used, source, source, z, wgu, wgu, wd)


def _finish_routed_kernel(slots_ref, nxt_ref, h_ref, cols_ref, p_ref, gpost_ref, gple_ref, wproj_ref, wgate_ref,
                          ys_ref, o_ref, y_ref, sem):
    i = pl.program_id(0)
    buf = i % 2

    def start(rows_ref, b):
        for k in range(TOP_K):
            _start_row_gather(ys_ref, lambda r, k=k: rows_ref[k, r], y_ref.at[b, k], sem.at[b])

    @pl.when(i == 0)
    def _():
        start(slots_ref, 0)

    for k in range(TOP_K):
        _wait_row_gather(ys_ref, y_ref.at[buf, k], sem.at[buf])

    @pl.when(i + 1 < pl.num_programs(0))
    def _():
        start(nxt_ref, 1 - buf)

    cols = cols_ref[...]
    f = cols[:, 2:3] * y_ref[buf, 0] + cols[:, 3:4] * y_ref[buf, 1]
    o_ref[...] = _layer_tail(h_ref[...], f, p_ref, gpost_ref, gple_ref, wproj_ref, wgate_ref)


def _finish_routed(h, ys, slots, cols, p, layer, gpost, gple, wproj, wgate, tm):
    m, d = h.shape
    n = m // tm
    row = pl.BlockSpec((tm, d), lambda i: (i, 0))
    return pl.pallas_call(
        _finish_routed_kernel,
        grid=(n,),
        in_specs=[pl.BlockSpec((TOP_K, tm), lambda i: (0, i), memory_space=pltpu.SMEM),
                  pl.BlockSpec((TOP_K, tm), lambda i: (0, jnp.minimum(i + 1, n - 1)), memory_space=pltpu.SMEM), row,
                  pl.BlockSpec((tm, LANES), lambda i: (i, 0)), _p_spec(p, layer, tm),
                  _resident((1, d)), _resident((1, d)), _resident(wproj.shape), _resident(wgate.shape),
                  pl.BlockSpec(memory_space=pl.ANY)],
        out_specs=row,
        out_shape=jax.ShapeDtypeStruct((m, d), F32),
        scratch_shapes=[pltpu.VMEM((2, TOP_K, tm, d), F32), pltpu.SemaphoreType.DMA((2,))],
        compiler_params=_cparams("arbitrary"),
        name="finish_routed",
    )(slots, slots, h, cols, p, gpost, gple, wproj, wgate, ys)


def _tile(m, pref):
    return pref if m % pref == 0 else m


def kernel(x_prompt, x_sample, state_conv, state_lru, cache_k, cache_v, page_table, p_prompt, p_sample, norm_mix_pre, norm_mix_post, norm_ffn_pre, norm_ffn_post, norm_ple, rec_w_in, rec_conv_w, rec_conv_b, rec_w_a, rec_b_a, rec_w_x, rec_b_x, rec_lambda, rec_w_out, att_w_qkv, att_w_o, att_sb_bias, ffn_w_gu, ffn_w_down, moe_w_router, moe_b_router, moe_w_gu, moe_w_down, ple_w_proj, ple_w_gate):
    bsz, t, d = x_prompt.shape
    s = x_sample.shape[0]
    depth = norm_mix_pre.shape[0]
    mp = bsz * t
    tmp = _tile(mp, 512)
    vec = lambda a: a.reshape(1, -1)
    bf = lambda a: a.astype(BF16)
    hp = x_prompt.reshape(mp, d)
    hs = x_sample.reshape(s, d)
    keep = lambda a: a
    outs = {k: [] for k in ("conv_p", "lru_p", "k_p", "v_p", "conv_s", "lru_s", "k_s", "v_s")}
    pp = p_prompt.reshape(depth, mp, -1)
    ps = p_sample.reshape(depth, s, -1)
    for i in range(depth):
        j = i // 2
        tail = (vec(norm_ffn_post[i]), vec(norm_ple[i]))
        if i % 2 == 0:
            def rec(cast):
                return (vec(norm_mix_pre[i]), cast(rec_w_in[j]), rec_conv_w[j], vec(rec_conv_b[j]), cast(rec_w_a[j]),
                        vec(rec_b_a[j]), cast(rec_w_x[j]), vec(rec_b_x[j]), vec(rec_lambda[j]), cast(rec_w_out[j]),
                        vec(norm_mix_post[i]))
            hp, conv_new, h_new = _rec_prompt(hp.reshape(bsz, t, d), *rec(bf), tt=_tile(t, 256))
            outs["conv_p"].append(conv_new)
            outs["lru_p"].append(h_new.reshape(bsz, d))
            hs, conv_new, h_new = _rec_step(hs, jnp.swapaxes(state_conv[j], 0, 1), state_lru[j], *rec(keep))
            outs["conv_s"].append(jnp.swapaxes(conv_new, 0, 1))
            outs["lru_s"].append(h_new)
            hp = _ffn_layer(hp.reshape(mp, d), vec(norm_ffn_pre[i]), bf(ffn_w_gu[j]), bf(ffn_w_down[j]), pp, i, *tail,
                            bf(ple_w_proj[i]), bf(ple_w_gate[i]), tmp)
            hs = _ffn_layer(hs, vec(norm_ffn_pre[i]), ffn_w_gu[j], ffn_w_down[j], ps, i, *tail, ple_w_proj[i],
                            ple_w_gate[i], s)
        else:
            hd = d // N_HEADS
            kp, vp, qb, kb, vb = _qkv(hp, vec(norm_mix_pre[i]), bf(att_w_qkv[j]), _tile(t, 512), seq=t)
            outs["k_p"].append(kp.reshape(bsz, N_HEADS, hd, t).transpose(0, 3, 1, 2))
            outs["v_p"].append(vp.reshape(bsz, N_HEADS, hd, t).transpose(0, 3, 1, 2))
            tk = _tile(t, 256)
            op = _sb_prompt(qb.reshape(bsz, t, d), kb.reshape(bsz, t, d), vb.reshape(bsz, t, d), att_sb_bias[j],
                            tq=_tile(t, 2 * tk), tk=tk)
            ks, vs, qs, _, _ = _qkv(hs, vec(norm_mix_pre[i]), att_w_qkv[j], s, q_dtype=F32)
            outs["k_s"].append(ks.reshape(s, 1, N_HEADS, hd))
            outs["v_s"].append(vs.reshape(s, 1, N_HEADS, hd))
            n_pages = page_table.shape[1]
            os_ = _sb_decode(qs.reshape(s, N_HEADS, hd, 1), jnp.transpose(cache_k[j], (0, 2, 3, 1)),
                             jnp.transpose(cache_v[j], (0, 2, 3, 1)), page_table, att_sb_bias[j].reshape(N_HEADS, 1),
                             pg=16 if n_pages % 16 == 0 else 1)
            wr = jnp.pad(moe_w_router[j], ((0, 0), (0, LANES - N_EXPERTS)))
            br = jnp.pad(moe_b_router[j], (0, LANES - N_EXPERTS), constant_values=NEG_BIG).reshape(1, LANES)
            norms = (vec(norm_mix_post[i]), vec(norm_ffn_pre[i]))
            hp, zp, cols, rows = _attn_out(hp, op.reshape(mp, d), bf(att_w_o[j]), *norms, bf(wr), br, tmp, route=True)
            hs, zs, cs = _attn_out(hs, os_.reshape(s, d), att_w_o[j], *norms, wr, br, s)
            wgu, wd = bf(moe_w_gu[j]), bf(moe_w_down[j])
            fs = _moe(zs, cs, wgu, wd, s)
            tme = _tile(mp, 512)
            n_tiles = (TOP_K * mp + N_EXPERTS * (tme - 1)) // tme
            slots, source, tile_expert, n_used = _route_plan(rows[0].astype(jnp.int32), rows[1].astype(jnp.int32),
                                                             N_EXPERTS, tme, n_tiles)
            ys = _experts(zp, source, tile_expert, n_used, wgu, wd)
            hp = _finish_routed(hp, ys, slots, cols, pp, i, *tail, bf(ple_w_proj[i]), bf(ple_w_gate[i]), tmp)
            hs = _finish(hs, fs, ps, i, *tail, ple_w_proj[i], ple_w_gate[i], s)
    st = lambda k: jnp.stack(outs[k])
    return (hp.reshape(bsz, t, d), hs.reshape(s, 1, d), st("conv_p"), st("lru_p"), st("k_p"), st("v_p"),
            st("conv_s"), st("lru_s"), st("k_s"), st("v_s"))
```

```python
import functools
import math

import jax
import jax.numpy as jnp
from jax import lax
from jax.experimental import pallas as pl
from jax.experimental.pallas import tpu as pltpu

F32 = jnp.float32
BF16 = jnp.bfloat16

RMS_EPS = 1e-6
LRU_C = 8.0
N_HEADS = 16
N_LRU_BLOCKS = 8
CONV_W = 4
N_EXPERTS = 8
TOP_K = 2
LANES = 128
SUBLANES = 8
LOG2E = 1.4426950408889634
NEG_BIG = -1e30
VMEM_LIMIT = 56 * 1024 * 1024


def _cparams(*sem):
    return pltpu.CompilerParams(dimension_semantics=sem, vmem_limit_bytes=VMEM_LIMIT)


def _resident(shape):
    return pl.BlockSpec(shape, lambda *_: (0,) * len(shape), pipeline_mode=pl.Buffered(1))


def _dot(a, b):
    return jnp.dot(a, b, preferred_element_type=F32)


def _mm(x, w):
    if w.dtype == F32:
        return jnp.dot(x.astype(F32), w, precision=lax.Precision.HIGHEST, preferred_element_type=F32)
    return jnp.dot(x.astype(w.dtype), w, preferred_element_type=F32)


def _dot_nt(a, b):
    return lax.dot_general(a, b, (((1,), (1,)), ((), ())), preferred_element_type=F32)


def _rms(x, g):
    return x * lax.rsqrt(jnp.mean(x * x, axis=-1, keepdims=True) + RMS_EPS) * g


def _softplus(x):
    return jnp.maximum(x, 0.0) + jnp.log1p(jnp.exp(-jnp.abs(x)))


def _gelu(x):
    c = math.sqrt(2.0 / math.pi)
    return 0.5 * x * (1.0 + jnp.tanh(c * (x + 0.044715 * (x * x * x))))


def _silu(x):
    return x * jax.nn.sigmoid(x)


def _lru_gates(xc, wa_ref, ba, wx_ref, bx, lam):
    blk = xc.shape[1] // N_LRU_BLOCKS
    ra, rx = [], []
    for n in range(N_LRU_BLOCKS):
        xb = xc[:, n * blk:(n + 1) * blk]
        ra.append(_mm(xb, wa_ref[n]))
        rx.append(_mm(xb, wx_ref[n]))
    r = jax.nn.sigmoid(jnp.concatenate(ra, axis=1) + ba)
    ig = jax.nn.sigmoid(jnp.concatenate(rx, axis=1) + bx)
    log_a = -LRU_C * r * _softplus(-lam)
    a = jnp.exp(log_a)
    gx = jnp.sqrt(-jnp.tanh(log_a) * (a * a + 1.0)) * ig * xc
    return a, gx


def _rec_prompt_kernel(x_ref, gpre_ref, win_ref, cw_ref, cb_ref, wa_ref, ba_ref, wx_ref, bx_ref, lam_ref,
                       wout_ref, gpost_ref, h_ref, conv_ref, hlast_ref, tail_ref, hc_ref):
    tt, d = x_ref.shape

    @pl.when(pl.program_id(1) == 0)
    def _():
        tail_ref[...] = jnp.zeros_like(tail_ref)
        hc_ref[...] = jnp.zeros_like(hc_ref)

    x = x_ref[...]
    xn = _rms(x, gpre_ref[...]).astype(win_ref.dtype)
    gate = _gelu(_mm(xn, win_ref[:, :d]))
    xr = _mm(xn, win_ref[:, d:])
    xpad = jnp.concatenate([tail_ref[...], xr], axis=0)
    xc = cb_ref[...] + cw_ref[3:4, :] * xr
    for back in range(1, CONV_W):
        xc = xc + cw_ref[CONV_W - 1 - back:CONV_W - back, :] * xpad[8 - back:8 - back + tt]
    tail_ref[...] = xr[tt - 8:]
    a, b = _lru_gates(xc, wa_ref, ba_ref[...], wx_ref, bx_ref[...], lam_ref[...])
    row = lax.rem(lax.broadcasted_iota(jnp.int32, (tt, 1), 0), SUBLANES)
    s = 1
    while s < SUBLANES:
        keep = row >= s
        a_sh = jnp.where(keep, pltpu.roll(a, s, 0), 1.0)
        b_sh = jnp.where(keep, pltpu.roll(b, s, 0), 0.0)
        b = a * b_sh + b
        a = a * a_sh
        s *= 2
    last = hc_ref[...]
    groups = []
    for g in range(tt // SUBLANES):
        rows = slice(g * SUBLANES, (g + 1) * SUBLANES)
        hg = b[rows] + a[rows] * last
        groups.append(hg)
        last = hg[SUBLANES - 1:]
    h = jnp.concatenate(groups, axis=0)
    hc_ref[...] = last
    y = _mm(h * gate, wout_ref[...])
    h_ref[...] = x + _rms(y, gpost_ref[...])
    conv_ref[...] = xr[tt - (CONV_W - 1):]
    hlast_ref[...] = h[tt - 1:]


def _rec_prompt(x, gpre, win, cw, cb, wa, ba, wx, bx, lam, wout, gpost, tt):
    bsz, t, d = x.shape
    vec = _resident((1, d))
    return pl.pallas_call(
        _rec_prompt_kernel,
        grid=(bsz, t // tt),
        in_specs=[pl.BlockSpec((None, tt, d), lambda b, i: (b, i, 0)), vec, _resident((d, 2 * d)),
                  _resident((CONV_W, d)), vec, _resident(wa.shape), vec, _resident(wx.shape), vec, vec,
                  _resident((d, d)), vec],
        out_specs=[pl.BlockSpec((None, tt, d), lambda b, i: (b, i, 0)),
                   pl.BlockSpec((None, CONV_W - 1, d), lambda b, i: (b, 0, 0)),
                   pl.BlockSpec((None, 1, d), lambda b, i: (b, 0, 0))],
        out_shape=[jax.ShapeDtypeStruct((bsz, t, d), F32), jax.ShapeDtypeStruct((bsz, CONV_W - 1, d), F32),
                   jax.ShapeDtypeStruct((bsz, 1, d), F32)],
        scratch_shapes=[pltpu.VMEM((8, d), F32), pltpu.VMEM((1, d), F32)],
        compiler_params=_cparams("arbitrary", "arbitrary"),
        name="rec_prompt",
    )(x, gpre, win, cw, cb, wa, ba, wx, bx, lam, wout, gpost)


def _rec_step_kernel(x_ref, sc_ref, h0_ref, gpre_ref, win_ref, cw_ref, cb_ref, wa_ref, ba_ref, wx_ref, bx_ref,
                     lam_ref, wout_ref, gpost_ref, h_ref, conv_ref, hnew_ref):
    d = x_ref.shape[1]
    x = x_ref[...]
    xn = _rms(x, gpre_ref[...]).astype(win_ref.dtype)
    gate = _gelu(_mm(xn, win_ref[:, :d]))
    xr = _mm(xn, win_ref[:, d:])
    xc = cb_ref[...] + cw_ref[CONV_W - 1:CONV_W, :] * xr
    for k in range(CONV_W - 1):
        xc = xc + cw_ref[k:k + 1, :] * sc_ref[k]
    a, b = _lru_gates(xc, wa_ref, ba_ref[...], wx_ref, bx_ref[...], lam_ref[...])
    h = a * h0_ref[...] + b
    y = _mm(h * gate, wout_ref[...])
    h_ref[...] = x + _rms(y, gpost_ref[...])
    for k in range(CONV_W - 2):
        conv_ref[k] = sc_ref[k + 1]
    conv_ref[CONV_W - 2] = xr
    hnew_ref[...] = h


def _rec_step(x, sc, h0, gpre, win, cw, cb, wa, ba, wx, bx, lam, wout, gpost):
    rows, d = x.shape
    return pl.pallas_call(
        _rec_step_kernel,
        out_shape=[jax.ShapeDtypeStruct((rows, d), F32), jax.ShapeDtypeStruct((CONV_W - 1, rows, d), F32),
                   jax.ShapeDtypeStruct((rows, d), F32)],
        compiler_params=pltpu.CompilerParams(vmem_limit_bytes=VMEM_LIMIT),
        name="rec_step",
    )(x, sc, h0, gpre, win, cw, cb, wa, ba, wx, bx, lam, wout, gpost)


def _layer_tail(h, f, p_ref, gpost_ref, gple_ref, wproj_ref, wgate_ref):
    h2 = h + _rms(f, gpost_ref[...])
    e = _mm(p_ref[...], wproj_ref[...])
    g = jax.nn.sigmoid(_mm(_rms(h2, gple_ref[...]), wgate_ref[...]))
    return h2 + g * e


def _ffn_layer_kernel(h_ref, g_ref, wgu_ref, wd_ref, p_ref, gpost_ref, gple_ref, wproj_ref, wgate_ref, o_ref, *, ck):
    ff = wd_ref.shape[0]
    h = h_ref[...]
    z = _rms(h, g_ref[...]).astype(wgu_ref.dtype)
    acc = jnp.zeros(h.shape, F32)
    for c in range(ff // ck):
        g = _mm(z, wgu_ref[:, c * ck:(c + 1) * ck])
        u = _mm(z, wgu_ref[:, ff + c * ck:ff + (c + 1) * ck])
        acc = acc + _mm(_silu(g) * u, wd_ref[c * ck:(c + 1) * ck, :])
    o_ref[...] = _layer_tail(h, acc, p_ref, gpost_ref, gple_ref, wproj_ref, wgate_ref)


def _p_spec(p, layer, tm):
    return pl.BlockSpec((None, tm, p.shape[2]), lambda i: (layer, i, 0))


def _ffn_layer(h, g, wgu, wd, p, layer, gpost, gple, wproj, wgate, tm, ck=512):
    m, d = h.shape
    row = pl.BlockSpec((tm, d), lambda i: (i, 0))
    vec = _resident((1, d))
    return pl.pallas_call(
        functools.partial(_ffn_layer_kernel, ck=ck),
        grid=(m // tm,),
        in_specs=[row, vec, _resident(wgu.shape), _resident(wd.shape), _p_spec(p, layer, tm), vec, vec,
                  _resident(wproj.shape), _resident(wgate.shape)],
        out_specs=row,
        out_shape=jax.ShapeDtypeStruct((m, d), F32),
        compiler_params=_cparams("arbitrary"),
        name="ffn_layer",
    )(h, g, wgu, wd, p, gpost, gple, wproj, wgate)


def _finish_kernel(h_ref, f_ref, p_ref, gpost_ref, gple_ref, wproj_ref, wgate_ref, o_ref):
    o_ref[...] = _layer_tail(h_ref[...], f_ref[...], p_ref, gpost_ref, gple_ref, wproj_ref, wgate_ref)


def _finish(h, f, p, layer, gpost, gple, wproj, wgate, tm):
    m, d = h.shape
    row = pl.BlockSpec((tm, d), lambda i: (i, 0))
    return pl.pallas_call(
        _finish_kernel,
        grid=(m // tm,),
        in_specs=[row, row, _p_spec(p, layer, tm), _resident((1, d)), _resident((1, d)),
                  _resident(wproj.shape), _resident(wgate.shape)],
        out_specs=row,
        out_shape=jax.ShapeDtypeStruct((m, d), F32),
        compiler_params=_cparams("arbitrary"),
        name="finish",
    )(h, f, p, gpost, gple, wproj, wgate)


def _qkv_kernel(h_ref, g_ref, w_ref, k_ref, v_ref, qb_ref, kb_ref, vb_ref):
    d = h_ref.shape[1]
    xn = _rms(h_ref[...], g_ref[...]).astype(w_ref.dtype)
    scale = (d // N_HEADS) ** -0.5
    qb_ref[...] = (_mm(xn, w_ref[:, :d]) * scale).astype(qb_ref.dtype)
    k = _mm(xn, w_ref[:, d:2 * d])
    v = _mm(xn, w_ref[:, 2 * d:])
    if k_ref.shape == k.shape:
        k_ref[...] = k
        v_ref[...] = v
    else:
        k_ref[...] = k.T
        v_ref[...] = v.T
    kb_ref[...] = k.astype(BF16)
    vb_ref[...] = v.astype(BF16)


def _qkv(h, g, w, tm, q_dtype=BF16, seq=None):
    m, d = h.shape
    row = pl.BlockSpec((tm, d), lambda i: (i, 0))
    if seq is None:
        kv_spec, kv_shape = row, jax.ShapeDtypeStruct((m, d), F32)
    else:
        per = seq // tm
        kv_spec = pl.BlockSpec((None, d, tm), lambda i: (i // per, 0, i % per))
        kv_shape = jax.ShapeDtypeStruct((m // seq, d, seq), F32)
    return pl.pallas_call(
        _qkv_kernel,
        grid=(m // tm,),
        in_specs=[row, _resident((1, d)), _resident(w.shape)],
        out_specs=[kv_spec, kv_spec, row, row, row],
        out_shape=[kv_shape, kv_shape, jax.ShapeDtypeStruct((m, d), q_dtype)]
        + [jax.ShapeDtypeStruct((m, d), BF16)] * 2,
        compiler_params=_cparams("arbitrary"),
        name="qkv",
    )(h, g, w)


def _sb_weights(z, sp, ls, tri, carry):
    hi = ls.astype(BF16)
    lo = (ls - hi.astype(F32)).astype(BF16)
    later = _dot(hi, tri) + _dot(lo, tri)
    return jnp.exp((z - sp) + later + carry)


def _sb_prompt_kernel(bias_ref, q_ref, k_ref, v_ref, o_ref, qs_ref, w_ref, acc_ref, carry_ref, *, hps, tk):
    tq, width = q_ref.shape
    hd = width // hps
    rows = hps * tq
    per = tq // tk
    grp = pl.program_id(1)
    i = pl.program_id(2)
    lane = lax.broadcasted_iota(jnp.int32, (tq, width), 1)
    q = q_ref[...]
    for h in range(hps):
        qs_ref[h * tq:(h + 1) * tq, :] = jnp.where((lane >= h * hd) & (lane < (h + 1) * hd), q, jnp.zeros_like(q))
    acc_ref[...] = jnp.zeros_like(acc_ref)
    carry_ref[...] = jnp.zeros_like(carry_ref)
    r2 = lax.broadcasted_iota(jnp.int32, (tk, tk), 0)
    c2 = lax.broadcasted_iota(jnp.int32, (tk, tk), 1)
    neg_tri = jnp.where(r2 > c2, -1.0, 0.0).astype(BF16)
    ahead = (lax.broadcasted_iota(jnp.int32, (rows, tk), 1)
             - lax.rem(lax.broadcasted_iota(jnp.int32, (rows, tk), 0), tq))
    bias = jnp.concatenate([jnp.full((tq, 1), bias_ref[hps * grp + h], F32) for h in range(hps)], axis=0)

    def weights(j, diag):
        z = _dot_nt(qs_ref[...], k_ref[pl.ds(pl.multiple_of(j * tk, tk), tk), :]) + bias
        sp = jnp.maximum(z, 0.0) + jnp.log(1.0 + jnp.exp2(jnp.abs(z) * -LOG2E))
        if diag:
            causal = ahead < i * tq - j * tk
            sp = jnp.where(causal, sp, 0.0)
        later = _dot(sp.astype(BF16), neg_tri)
        w = jnp.exp((z - sp) + later + carry_ref[...])
        if diag:
            w = jnp.where(causal, w, 0.0)
        w_ref[...] = w.astype(BF16)
        carry_ref[...] -= jnp.sum(sp, axis=-1, keepdims=True)

    def values(j):
        acc_ref[...] += _dot(w_ref[...], v_ref[pl.ds(pl.multiple_of(j * tk, tk), tk), :])

    top = (i + 1) * per - 1
    weights(top, True)
    for back in range(1, per):
        values(top - back + 1)
        weights(top - back, True)

    def body(jj, c):
        j = i * per - 1 - jj
        values(j + 1)
        weights(j, False)
        return c

    lax.fori_loop(0, i * per, body, 0)
    values(0)
    out = acc_ref[0:tq, :]
    for h in range(1, hps):
        out = jnp.where(lane >= h * hd, acc_ref[h * tq:(h + 1) * tq, :], out)
    o_ref[...] = out.astype(o_ref.dtype)


def _sb_prompt(qb, kb, vb, bias, tq, tk, hps=4):
    bsz, t, d = qb.shape
    width = hps * (d // N_HEADS)
    qspec = pl.BlockSpec((None, tq, width), lambda b, g, i: (b, i, g))
    kvspec = pl.BlockSpec((None, t, width), lambda b, g, i: (b, 0, g))
    return pl.pallas_call(
        functools.partial(_sb_prompt_kernel, hps=hps, tk=tk),
        grid=(bsz, d // width, t // tq),
        in_specs=[pl.BlockSpec(memory_space=pltpu.SMEM), qspec, kvspec, kvspec],
        out_specs=qspec,
        out_shape=jax.ShapeDtypeStruct((bsz, t, d), BF16),
        scratch_shapes=[pltpu.VMEM((hps * tq, width), BF16), pltpu.VMEM((hps * tq, tk), BF16),
                        pltpu.VMEM((hps * tq, width), F32),
                        pltpu.VMEM((hps * tq, 1), F32)],
        compiler_params=_cparams("arbitrary", "arbitrary", "arbitrary"),
        name="sb_prompt",
    )(bias, qb, kb, vb)


def _sb_decode_kernel(pt_ref, q_ref, bias_ref, *refs, pg):
    del pt_ref
    k_refs, v_refs = refs[:pg], refs[pg:2 * pg]
    o_ref, qb_ref, acc_ref, carry_ref = refs[2 * pg:]
    nh, _, page = k_refs[0].shape
    g = pl.program_id(1)

    @pl.when(g == 0)
    def _():
        acc_ref[...] = jnp.zeros_like(acc_ref)
        carry_ref[...] = jnp.zeros_like(carry_ref)
        qb_ref[...] = jnp.broadcast_to(q_ref[...], qb_ref.shape)

    row = lax.broadcasted_iota(jnp.int32, (page, page), 0)
    col = lax.broadcasted_iota(jnp.int32, (page, page), 1)
    tri = jnp.where(row > col, 1.0, 0.0).astype(BF16)
    z = jnp.concatenate([jnp.sum(qb_ref[h] * k_refs[r][h], axis=0, keepdims=True)
                         for r in range(pg) for h in range(nh)], axis=0)
    z = z + jnp.concatenate([bias_ref[...]] * pg, axis=0)
    sp = _softplus(z)
    ls = -sp
    total = jnp.sum(ls, axis=-1, keepdims=True)
    carries = [carry_ref[...]]
    for r in range(pg):
        carries.append(carries[-1] + total[r * nh:(r + 1) * nh])
    carry_ref[...] = carries[pg]
    w = _sb_weights(z, sp, ls, tri, jnp.concatenate(carries[:pg], axis=0))
    for h in range(nh):
        part = w[h:h + 1, :] * v_refs[0][h]
        for r in range(1, pg):
            part = part + w[r * nh + h:r * nh + h + 1, :] * v_refs[r][h]
        acc_ref[h] += part

    @pl.when(g == pl.num_programs(1) - 1)
    def _():
        o_ref[...] = jnp.sum(acc_ref[...], axis=-1)


def _sb_decode(q, cache_k, cache_v, page_table, bias, pg):
    s, nh, hd, _ = q.shape
    n_pages = page_table.shape[1]
    page = cache_k.shape[3]

    def page_spec(r):
        return pl.BlockSpec((None, nh, hd, page), lambda b, g, pt: (pt[b, n_pages - 1 - (g * pg + r)], 0, 0, 0))

    grid_spec = pltpu.PrefetchScalarGridSpec(
        num_scalar_prefetch=1,
        grid=(s, n_pages // pg),
        in_specs=[pl.BlockSpec((None, nh, hd, 1), lambda b, g, pt: (b, 0, 0, 0)),
                  pl.BlockSpec((nh, 1), lambda b, g, pt: (0, 0))] + [page_spec(r) for r in range(pg)] * 2,
        out_specs=pl.BlockSpec((None, nh, hd), lambda b, g, pt: (b, 0, 0)),
        scratch_shapes=[pltpu.VMEM((nh, hd, page), F32), pltpu.VMEM((nh, hd, page), F32), pltpu.VMEM((nh, 1), F32)],
    )
    return pl.pallas_call(
        functools.partial(_sb_decode_kernel, pg=pg),
        grid_spec=grid_spec,
        out_shape=jax.ShapeDtypeStruct((s, nh, hd), F32),
        compiler_params=_cparams("arbitrary", "arbitrary"),
        name="sb_decode",
    )(page_table, q, bias, *([cache_k] * pg), *([cache_v] * pg))


def _mix_residual(h_ref, o_ref, wo_ref, gpost_ref, gffn_ref):
    h1 = h_ref[...] + _rms(_mm(o_ref[...], wo_ref[...]), gpost_ref[...])
    return h1, _rms(h1, gffn_ref[...])


def _top2(logits):
    idx = lax.broadcasted_iota(jnp.int32, logits.shape, 1)
    m1 = jnp.max(logits, axis=-1, keepdims=True)
    i1 = jnp.min(jnp.where(logits == m1, idx, LANES), axis=-1, keepdims=True)
    rest = jnp.where(idx == i1, NEG_BIG, logits)
    m2 = jnp.max(rest, axis=-1, keepdims=True)
    i2 = jnp.min(jnp.where(rest == m2, idx, LANES), axis=-1, keepdims=True)
    e = jnp.exp(m2 - m1)
    w1 = 1.0 / (1.0 + e)
    return idx, i1, i2, w1, e * w1


def _attn_out_kernel(h_ref, o_ref, wo_ref, gpost_ref, gffn_ref, wr_ref, br_ref, h1_ref, z_ref, comb_ref):
    h1, z = _mix_residual(h_ref, o_ref, wo_ref, gpost_ref, gffn_ref)
    h1_ref[...] = h1
    z_ref[...] = z.astype(z_ref.dtype)
    idx, i1, i2, w1, w2 = _top2(_mm(z, wr_ref[...]) + br_ref[...])
    comb_ref[...] = jnp.where(idx == i1, w1, 0.0) + jnp.where(idx == i2, w2, 0.0)


def _attn_out_route_kernel(h_ref, o_ref, wo_ref, gpost_ref, gffn_ref, wr_ref, br_ref, h1_ref, z_ref, cols_ref,
                           rows_ref):
    h1, z = _mix_residual(h_ref, o_ref, wo_ref, gpost_ref, gffn_ref)
    h1_ref[...] = h1
    z_ref[...] = z
    idx, i1, i2, w1, w2 = _top2(_mm(z, wr_ref[...]) + br_ref[...])
    cols = jnp.where(idx == 0, i1.astype(F32), jnp.where(idx == 1, i2.astype(F32),
                                                         jnp.where(idx == 2, w1, jnp.where(idx == 3, w2, 0.0))))
    cols_ref[...] = cols
    pick = (lax.broadcasted_iota(jnp.int32, (8, LANES), 0) == lax.broadcasted_iota(jnp.int32, (8, LANES), 1))
    rows_ref[...] = lax.dot_general(pick.astype(F32), cols, (((1,), (1,)), ((), ())),
                                    precision=lax.Precision.HIGHEST, preferred_element_type=F32)


def _attn_out(h, o, wo, gpost, gffn, wr, br, tm, route=False):
    m, d = h.shape
    row = pl.BlockSpec((tm, d), lambda i: (i, 0))
    comb = pl.BlockSpec((tm, LANES), lambda i: (i, 0))
    in_specs = [row, row, _resident(wo.shape), _resident((1, d)), _resident((1, d)), _resident(wr.shape),
                _resident(br.shape)]
    if route:
        return pl.pallas_call(
            _attn_out_route_kernel,
            grid=(m // tm,),
            in_specs=in_specs,
            out_specs=[row, row, comb, pl.BlockSpec((8, tm), lambda i: (0, i))],
            out_shape=[jax.ShapeDtypeStruct((m, d), F32), jax.ShapeDtypeStruct((m, d), F32),
                       jax.ShapeDtypeStruct((m, LANES), F32), jax.ShapeDtypeStruct((8, m), F32)],
            compiler_params=_cparams("arbitrary"),
            name="attn_out_route",
        )(h, o, wo, gpost, gffn, wr, br)
    return pl.pallas_call(
        _attn_out_kernel,
        grid=(m // tm,),
        in_specs=in_specs,
        out_specs=[row, row, comb],
        out_shape=[jax.ShapeDtypeStruct((m, d), F32), jax.ShapeDtypeStruct((m, d), BF16),
                   jax.ShapeDtypeStruct((m, LANES), F32)],
        compiler_params=_cparams("arbitrary"),
        name="attn_out",
    )(h, o, wo, gpost, gffn, wr, br)


def _moe_kernel(z_ref, comb_ref, wg_ref, wu_ref, wd_ref, f_ref, wgb_ref, wub_ref, wdb_ref):
    e = pl.program_id(0)
    c = pl.program_id(1)

    @pl.when((e == 0) & (c == 0))
    def _():
        f_ref[...] = jnp.zeros_like(f_ref)

    wg = wg_ref[...].astype(BF16)
    wu = wu_ref[...].astype(BF16)
    wd = wd_ref[...].astype(BF16)
    wgb_ref[...] = wg
    wub_ref[...] = wu
    wdb_ref[...] = wd
    z = z_ref[...]
    act = (_silu(_dot(z, wg)) * _dot(z, wu)).astype(BF16)
    lane = lax.broadcasted_iota(jnp.int32, comb_ref.shape, 1)
    ce = jnp.sum(jnp.where(lane == e, comb_ref[...], 0.0), axis=-1, keepdims=True)
    f_ref[...] += ce * _dot(act, wd)


def _moe(z, comb, wgu, wd, ck=512):
    m, d = z.shape
    n_e, ff, _ = wd.shape
    fc = ff // ck
    rows = pl.BlockSpec((m, d), lambda e, c: (0, 0))
    up_down = [pl.BlockSpec((None, d, ck), lambda e, c: (e, 0, c)), pl.BlockSpec((None, ck, d), lambda e, c: (e, c, 0))]
    return pl.pallas_call(
        _moe_kernel,
        grid=(n_e, fc),
        in_specs=[rows, pl.BlockSpec((m, LANES), lambda e, c: (0, 0)), up_down[0],
                  pl.BlockSpec((None, d, ck), lambda e, c: (e, 0, fc + c)), up_down[1]],
        out_specs=[rows, up_down[0], up_down[0], up_down[1]],
        out_shape=[jax.ShapeDtypeStruct((m, d), F32), jax.ShapeDtypeStruct((n_e, d, ff), BF16),
                   jax.ShapeDtypeStruct((n_e, d, ff), BF16), jax.ShapeDtypeStruct((n_e, ff, d), BF16)],
        compiler_params=_cparams("arbitrary", "arbitrary"),
        name="moe",
    )(z, comb, wgu, wgu, wd)


def _route_plan(i1, i2, n_e, tm, n_tiles):
    m = i1.shape[0]
    experts = jnp.arange(n_e, dtype=jnp.int32)[None, :]
    hit = (experts == i1[:, None]).astype(jnp.int32) + (experts == i2[:, None]).astype(jnp.int32)
    upto = jnp.cumsum(hit, axis=0)
    before = upto - hit
    padded = ((upto[-1] + tm - 1) // tm) * tm
    ends = jnp.cumsum(padded)
    starts = ends - padded
    slot1 = starts[i1] + jnp.take_along_axis(before, i1[:, None], axis=1)[:, 0]
    slot2 = starts[i2] + jnp.take_along_axis(before, i2[:, None], axis=1)[:, 0]
    slots = jnp.stack([slot1, slot2]).astype(jnp.int32)
    token = jnp.tile(jnp.arange(m, dtype=jnp.int32), 2)
    source = jnp.zeros((n_tiles * tm,), jnp.int32).at[slots.reshape(-1)].set(token).reshape(n_tiles, 1, tm)
    tile_start = jnp.arange(n_tiles, dtype=jnp.int32) * tm
    tile_expert = jnp.minimum(jnp.sum(tile_start[:, None] >= ends[None, :], axis=1), n_e - 1).astype(jnp.int32)
    return slots, source, tile_expert, (ends[-1:] // tm).astype(jnp.int32)


GATHER_UNROLL = 8


def _start_row_gather(src_ref, row_of, dst_ref, sem):
    rows = dst_ref.shape[0]

    def body(r, carry):
        pltpu.make_async_copy(src_ref.at[pl.ds(row_of(r), 1), :], dst_ref.at[pl.ds(r, 1), :], sem).start()
        return carry

    lax.fori_loop(0, rows, body, 0, unroll=GATHER_UNROLL)


def _wait_row_gather(src_ref, dst_ref, sem):
    pltpu.make_async_copy(src_ref.at[pl.ds(0, dst_ref.shape[0]), :], dst_ref, sem).wait()


def _experts_kernel(te_ref, nu_ref, src_ref, nxt_ref, z_ref, wg_ref, wu_ref, wd_ref, y_ref, xg_ref, xb_ref, sem):
    del te_ref
    i = pl.program_id(0)
    c = pl.program_id(1)
    n_used = nu_ref[0]
    used = i < n_used

    @pl.when(used & (c == 0))
    def _():
        buf = i % 2

        @pl.when(i == 0)
        def _():
            _start_row_gather(z_ref, lambda r: src_ref[0, r], xg_ref.at[0], sem.at[0])

        _wait_row_gather(z_ref, xg_ref.at[buf], sem.at[buf])
        xb_ref[...] = xg_ref[buf].astype(BF16)

        @pl.when(i + 1 < n_used)
        def _():
            _start_row_gather(z_ref, lambda r: nxt_ref[0, r], xg_ref.at[1 - buf], sem.at[1 - buf])

    @pl.when(used)
    def _():
        xb = xb_ref[...]
        act = (_silu(_dot(xb, wg_ref[...])) * _dot(xb, wu_ref[...])).astype(BF16)
        y = _dot(act, wd_ref[...])

        @pl.when(c == 0)
        def _():
            y_ref[...] = y

        @pl.when(c > 0)
        def _():
            y_ref[...] += y

    @pl.when(jnp.logical_not(used) & (c == 0))
    def _():
        y_ref[...] = jnp.zeros_like(y_ref)


def _experts(z, source, tile_expert, n_used, wg, wu, wd, ck=1792):
    n_tiles, _, tm = source.shape
    d = z.shape[1]
    ff = wd.shape[1]
    assert ff % ck == 0, (ff, ck)
    fc = ff // ck
    grid_spec = pltpu.PrefetchScalarGridSpec(
        num_scalar_prefetch=2,
        grid=(n_tiles, fc),
        in_specs=[pl.BlockSpec((None, 1, tm), lambda i, c, te, nu: (i, 0, 0), memory_space=pltpu.SMEM),
                  pl.BlockSpec((None, 1, tm), lambda i, c, te, nu: (jnp.minimum(i + 1, n_tiles - 1), 0, 0),
                               memory_space=pltpu.SMEM),
                  pl.BlockSpec(memory_space=pl.ANY),
                  pl.BlockSpec((None, d, ck), lambda i, c, te, nu: (te[i], 0, c)),
                  pl.BlockSpec((None, d, ck), lambda i, c, te, nu: (te[i], 0, c)),
                  pl.BlockSpec((None, ck, d), lambda i, c, te, nu: (te[i], c, 0))],
        out_specs=pl.BlockSpec((tm, d), lambda i, c, te, nu: (i, 0)),
        scratch_shapes=[pltpu.VMEM((2, tm, d), z.dtype), pltpu.VMEM((tm, d), BF16), pltpu.SemaphoreType.DMA((2,))],
    )
    return pl.pallas_call(
        _experts_kernel,
        grid_spec=grid_spec,
        out_shape=jax.ShapeDtypeStruct((n_tiles * tm, d), F32),
        compiler_params=_cparams("arbitrary", "arbitrary"),
        name="moe_experts",
    )(tile_expert, n_used, source, source, z, wg, wu, wd)


def _finish_routed_kernel(slots_ref, nxt_ref, h_ref, cols_ref, p_ref, gpost_ref, gple_ref, wproj_ref, wgate_ref,
                          ys_ref, o_ref, y_ref, sem):
    i = pl.program_id(0)
    buf = i % 2

    def start(rows_ref, b):
        for k in range(TOP_K):
            _start_row_gather(ys_ref, lambda r, k=k: rows_ref[k, r], y_ref.at[b, k], sem.at[b])

    @pl.when(i == 0)
    def _():
        start(slots_ref, 0)

    for k in range(TOP_K):
        _wait_row_gather(ys_ref, y_ref.at[buf, k], sem.at[buf])

    @pl.when(i + 1 < pl.num_programs(0))
    def _():
        start(nxt_ref, 1 - buf)

    cols = cols_ref[...]
    f = cols[:, 2:3] * y_ref[buf, 0] + cols[:, 3:4] * y_ref[buf, 1]
    o_ref[...] = _layer_tail(h_ref[...], f, p_ref, gpost_ref, gple_ref, wproj_ref, wgate_ref)


def _finish_routed(h, ys, slots, cols, p, layer, gpost, gple, wproj, wgate, tm):
    m, d = h.shape
    n = m // tm
    row = pl.BlockSpec((tm, d), lambda i: (i, 0))
    return pl.pallas_call(
        _finish_routed_kernel,
        grid=(n,),
        in_specs=[pl.BlockSpec((TOP_K, tm), lambda i: (0, i), memory_space=pltpu.SMEM),
                  pl.BlockSpec((TOP_K, tm), lambda i: (0, jnp.minimum(i + 1, n - 1)), memory_space=pltpu.SMEM), row,
                  pl.BlockSpec((tm, LANES), lambda i: (i, 0)), _p_spec(p, layer, tm),
                  _resident((1, d)), _resident((1, d)), _resident(wproj.shape), _resident(wgate.shape),
                  pl.BlockSpec(memory_space=pl.ANY)],
        out_specs=row,
        out_shape=jax.ShapeDtypeStruct((m, d), F32),
        scratch_shapes=[pltpu.VMEM((2, TOP_K, tm, d), F32), pltpu.SemaphoreType.DMA((2,))],
        compiler_params=_cparams("arbitrary"),
        name="finish_routed",
    )(slots, slots, h, cols, p, gpost, gple, wproj, wgate, ys)


def _tile(m, pref):
    return pref if m % pref == 0 else m


def kernel(x_prompt, x_sample, state_conv, state_lru, cache_k, cache_v, page_table, p_prompt, p_sample, norm_mix_pre, norm_mix_post, norm_ffn_pre, norm_ffn_post, norm_ple, rec_w_in, rec_conv_w, rec_conv_b, rec_w_a, rec_b_a, rec_w_x, rec_b_x, rec_lambda, rec_w_out, att_w_qkv, att_w_o, att_sb_bias, ffn_w_gu, ffn_w_down, moe_w_router, moe_b_router, moe_w_gu, moe_w_down, ple_w_proj, ple_w_gate):
    bsz, t, d = x_prompt.shape
    s = x_sample.shape[0]
    depth = norm_mix_pre.shape[0]
    mp = bsz * t
    tmp = _tile(mp, 512)
    vec = lambda a: a.reshape(1, -1)
    bf = lambda a: a.astype(BF16)
    hp = x_prompt.reshape(mp, d)
    hs = x_sample.reshape(s, d)
    keep = lambda a: a
    outs = {k: [] for k in ("conv_p", "lru_p", "k_p", "v_p", "conv_s", "lru_s", "k_s", "v_s")}
    pp = p_prompt.reshape(depth, mp, -1)
    ps = p_sample.reshape(depth, s, -1)
    for i in range(depth):
        j = i // 2
        tail = (vec(norm_ffn_post[i]), vec(norm_ple[i]))
        if i % 2 == 0:
            def rec(cast):
                return (vec(norm_mix_pre[i]), cast(rec_w_in[j]), rec_conv_w[j], vec(rec_conv_b[j]), cast(rec_w_a[j]),
                        vec(rec_b_a[j]), cast(rec_w_x[j]), vec(rec_b_x[j]), vec(rec_lambda[j]), cast(rec_w_out[j]),
                        vec(norm_mix_post[i]))
            hp, conv_new, h_new = _rec_prompt(hp.reshape(bsz, t, d), *rec(bf), tt=_tile(t, 256))
            outs["conv_p"].append(conv_new)
            outs["lru_p"].append(h_new.reshape(bsz, d))
            hs, conv_new, h_new = _rec_step(hs, jnp.swapaxes(state_conv[j], 0, 1), state_lru[j], *rec(keep))
            outs["conv_s"].append(jnp.swapaxes(conv_new, 0, 1))
            outs["lru_s"].append(h_new)
            hp = _ffn_layer(hp.reshape(mp, d), vec(norm_ffn_pre[i]), bf(ffn_w_gu[j]), bf(ffn_w_down[j]), pp, i, *tail,
                            bf(ple_w_proj[i]), bf(ple_w_gate[i]), tmp)
            hs = _ffn_layer(hs, vec(norm_ffn_pre[i]), ffn_w_gu[j], ffn_w_down[j], ps, i, *tail, ple_w_proj[i],
                            ple_w_gate[i], s)
        else:
            hd = d // N_HEADS
            kp, vp, qb, kb, vb = _qkv(hp, vec(norm_mix_pre[i]), bf(att_w_qkv[j]), _tile(t, 512), seq=t)
            outs["k_p"].append(kp.reshape(bsz, N_HEADS, hd, t).transpose(0, 3, 1, 2))
            outs["v_p"].append(vp.reshape(bsz, N_HEADS, hd, t).transpose(0, 3, 1, 2))
            tk = _tile(t, 256)
            op = _sb_prompt(qb.reshape(bsz, t, d), kb.reshape(bsz, t, d), vb.reshape(bsz, t, d), att_sb_bias[j],
                            tq=_tile(t, 2 * tk), tk=tk)
            ks, vs, qs, _, _ = _qkv(hs, vec(norm_mix_pre[i]), att_w_qkv[j], s, q_dtype=F32)
            outs["k_s"].append(ks.reshape(s, 1, N_HEADS, hd))
            outs["v_s"].append(vs.reshape(s, 1, N_HEADS, hd))
            n_pages = page_table.shape[1]
            os_ = _sb_decode(qs.reshape(s, N_HEADS, hd, 1), jnp.transpose(cache_k[j], (0, 2, 3, 1)),
                             jnp.transpose(cache_v[j], (0, 2, 3, 1)), page_table, att_sb_bias[j].reshape(N_HEADS, 1),
                             pg=16 if n_pages % 16 == 0 else 1)
            wr = jnp.pad(moe_w_router[j], ((0, 0), (0, LANES - N_EXPERTS)))
            br = jnp.pad(moe_b_router[j], (0, LANES - N_EXPERTS), constant_values=NEG_BIG).reshape(1, LANES)
            norms = (vec(norm_mix_post[i]), vec(norm_ffn_pre[i]))
            hp, zp, cols, rows = _attn_out(hp, op.reshape(mp, d), bf(att_w_o[j]), *norms, bf(wr), br, tmp, route=True)
            hs, zs, cs = _attn_out(hs, os_.reshape(s, d), att_w_o[j], *norms, wr, br, s)
            fs, wg, wu, wd = _moe(zs, cs, moe_w_gu[j], moe_w_down[j])
            tme = _tile(mp, 512)
            n_tiles = (TOP_K * mp + N_EXPERTS * (tme - 1)) // tme
            slots, source, tile_expert, n_used = _route_plan(rows[0].astype(jnp.int32), rows[1].astype(jnp.int32),
                                                             N_EXPERTS, tme, n_tiles)
            ys = _experts(zp, source, tile_expert, n_used, wg, wu, wd)
            hp = _finish_routed(hp, ys, slots, cols, pp, i, *tail, bf(ple_w_proj[i]), bf(ple_w_gate[i]), tmp)
            hs = _finish(hs, fs, ps, i, *tail, ple_w_proj[i], ple_w_gate[i], s)
    st = lambda k: jnp.stack(outs[k])
    return (hp.reshape(bsz, t, d), hs.reshape(s, 1, d), st("conv_p"), st("lru_p"), st("k_p"), st("v_p"),
            st("conv_s"), st("lru_s"), st("k_s"), st("v_s"))
```

```python
import functools
import math

import jax
import jax.numpy as jnp
from jax import lax
from jax.experimental import pallas as pl
from jax.experimental.pallas import tpu as pltpu

F32 = jnp.float32
BF16 = jnp.bfloat16

RMS_EPS = 1e-6
LRU_C = 8.0
N_HEADS = 16
N_LRU_BLOCKS = 8
CONV_W = 4
N_EXPERTS = 8
TOP_K = 2
LANES = 128
SUBLANES = 8
LOG2E = 1.4426950408889634
NEG_BIG = -1e30
VMEM_LIMIT = 56 * 1024 * 1024

ROW_TILE = 512
SCAN_TILE = 256
KEY_TILE = 256
QUERY_TILE = 2 * KEY_TILE
HEADS_PER_STEP = 4
PAGES_PER_STEP = 16
FF_CHUNK = 512
EXPERT_FF_CHUNK = 1792


def _cparams(*sem):
    return pltpu.CompilerParams(dimension_semantics=sem, vmem_limit_bytes=VMEM_LIMIT)


def _resident(shape):
    return pl.BlockSpec(shape, lambda *_: (0,) * len(shape), pipeline_mode=pl.Buffered(1))


def _dot(a, b):
    return jnp.dot(a, b, preferred_element_type=F32)


def _mm(x, w):
    if w.dtype == F32:
        return jnp.dot(x.astype(F32), w, precision=lax.Precision.HIGHEST, preferred_element_type=F32)
    return jnp.dot(x.astype(w.dtype), w, preferred_element_type=F32)


def _dot_nt(a, b):
    return lax.dot_general(a, b, (((1,), (1,)), ((), ())), preferred_element_type=F32)


def _rms(x, g):
    return x * lax.rsqrt(jnp.mean(x * x, axis=-1, keepdims=True) + RMS_EPS) * g


def _softplus(x):
    return jnp.maximum(x, 0.0) + jnp.log1p(jnp.exp(-jnp.abs(x)))


def _gelu(x):
    c = math.sqrt(2.0 / math.pi)
    return 0.5 * x * (1.0 + jnp.tanh(c * (x + 0.044715 * (x * x * x))))


def _silu(x):
    return x * jax.nn.sigmoid(x)


def _lru_gates(xc, wa_ref, ba, wx_ref, bx, lam):
    blk = xc.shape[1] // N_LRU_BLOCKS
    ra, rx = [], []
    for n in range(N_LRU_BLOCKS):
        xb = xc[:, n * blk:(n + 1) * blk]
        ra.append(_mm(xb, wa_ref[n]))
        rx.append(_mm(xb, wx_ref[n]))
    r = jax.nn.sigmoid(jnp.concatenate(ra, axis=1) + ba)
    ig = jax.nn.sigmoid(jnp.concatenate(rx, axis=1) + bx)
    log_a = -LRU_C * r * _softplus(-lam)
    a = jnp.exp(log_a)
    gx = jnp.sqrt(-jnp.tanh(log_a) * (a * a + 1.0)) * ig * xc
    return a, gx


def _rec_prompt_kernel(x_ref, gpre_ref, win_ref, cw_ref, cb_ref, wa_ref, ba_ref, wx_ref, bx_ref, lam_ref,
                       wout_ref, gpost_ref, h_ref, conv_ref, hlast_ref, tail_ref, hc_ref):
    tt, d = x_ref.shape

    @pl.when(pl.program_id(1) == 0)
    def _():
        tail_ref[...] = jnp.zeros_like(tail_ref)
        hc_ref[...] = jnp.zeros_like(hc_ref)

    x = x_ref[...]
    xn = _rms(x, gpre_ref[...]).astype(win_ref.dtype)
    gate = _gelu(_mm(xn, win_ref[:, :d]))
    xr = _mm(xn, win_ref[:, d:])
    xpad = jnp.concatenate([tail_ref[...], xr], axis=0)
    xc = cb_ref[...] + cw_ref[3:4, :] * xr
    for back in range(1, CONV_W):
        xc = xc + cw_ref[CONV_W - 1 - back:CONV_W - back, :] * xpad[8 - back:8 - back + tt]
    tail_ref[...] = xr[tt - 8:]
    a, b = _lru_gates(xc, wa_ref, ba_ref[...], wx_ref, bx_ref[...], lam_ref[...])
    row = lax.rem(lax.broadcasted_iota(jnp.int32, (tt, 1), 0), SUBLANES)
    s = 1
    while s < SUBLANES:
        keep = row >= s
        a_sh = jnp.where(keep, pltpu.roll(a, s, 0), 1.0)
        b_sh = jnp.where(keep, pltpu.roll(b, s, 0), 0.0)
        b = a * b_sh + b
        a = a * a_sh
        s *= 2
    last = hc_ref[...]
    groups = []
    for g in range(tt // SUBLANES):
        rows = slice(g * SUBLANES, (g + 1) * SUBLANES)
        hg = b[rows] + a[rows] * last
        groups.append(hg)
        last = hg[SUBLANES - 1:]
    h = jnp.concatenate(groups, axis=0)
    hc_ref[...] = last
    y = _mm(h * gate, wout_ref[...])
    h_ref[...] = x + _rms(y, gpost_ref[...])
    conv_ref[...] = xr[tt - (CONV_W - 1):]
    hlast_ref[...] = h[tt - 1:]


def _rec_prompt(x, gpre, win, cw, cb, wa, ba, wx, bx, lam, wout, gpost, tt):
    bsz, t, d = x.shape
    vec = _resident((1, d))
    return pl.pallas_call(
        _rec_prompt_kernel,
        grid=(bsz, t // tt),
        in_specs=[pl.BlockSpec((None, tt, d), lambda b, i: (b, i, 0)), vec, _resident((d, 2 * d)),
                  _resident((CONV_W, d)), vec, _resident(wa.shape), vec, _resident(wx.shape), vec, vec,
                  _resident((d, d)), vec],
        out_specs=[pl.BlockSpec((None, tt, d), lambda b, i: (b, i, 0)),
                   pl.BlockSpec((None, CONV_W - 1, d), lambda b, i: (b, 0, 0)),
                   pl.BlockSpec((None, 1, d), lambda b, i: (b, 0, 0))],
        out_shape=[jax.ShapeDtypeStruct((bsz, t, d), F32), jax.ShapeDtypeStruct((bsz, CONV_W - 1, d), F32),
                   jax.ShapeDtypeStruct((bsz, 1, d), F32)],
        scratch_shapes=[pltpu.VMEM((8, d), F32), pltpu.VMEM((1, d), F32)],
        compiler_params=_cparams("arbitrary", "arbitrary"),
        name="rec_prompt",
    )(x, gpre, win, cw, cb, wa, ba, wx, bx, lam, wout, gpost)


def _rec_step_kernel(x_ref, sc_ref, h0_ref, gpre_ref, win_ref, cw_ref, cb_ref, wa_ref, ba_ref, wx_ref, bx_ref,
                     lam_ref, wout_ref, gpost_ref, h_ref, conv_ref, hnew_ref):
    d = x_ref.shape[1]
    x = x_ref[...]
    xn = _rms(x, gpre_ref[...]).astype(win_ref.dtype)
    gate = _gelu(_mm(xn, win_ref[:, :d]))
    xr = _mm(xn, win_ref[:, d:])
    xc = cb_ref[...] + cw_ref[CONV_W - 1:CONV_W, :] * xr
    for k in range(CONV_W - 1):
        xc = xc + cw_ref[k:k + 1, :] * sc_ref[k]
    a, b = _lru_gates(xc, wa_ref, ba_ref[...], wx_ref, bx_ref[...], lam_ref[...])
    h = a * h0_ref[...] + b
    y = _mm(h * gate, wout_ref[...])
    h_ref[...] = x + _rms(y, gpost_ref[...])
    for k in range(CONV_W - 2):
        conv_ref[k] = sc_ref[k + 1]
    conv_ref[CONV_W - 2] = xr
    hnew_ref[...] = h


def _rec_step(x, sc, h0, gpre, win, cw, cb, wa, ba, wx, bx, lam, wout, gpost):
    rows, d = x.shape
    return pl.pallas_call(
        _rec_step_kernel,
        out_shape=[jax.ShapeDtypeStruct((rows, d), F32), jax.ShapeDtypeStruct((CONV_W - 1, rows, d), F32),
                   jax.ShapeDtypeStruct((rows, d), F32)],
        compiler_params=pltpu.CompilerParams(vmem_limit_bytes=VMEM_LIMIT),
        name="rec_step",
    )(x, sc, h0, gpre, win, cw, cb, wa, ba, wx, bx, lam, wout, gpost)


def _layer_tail(h, f, p_ref, gpost_ref, gple_ref, wproj_ref, wgate_ref):
    h2 = h + _rms(f, gpost_ref[...])
    e = _mm(p_ref[...], wproj_ref[...])
    g = jax.nn.sigmoid(_mm(_rms(h2, gple_ref[...]), wgate_ref[...]))
    return h2 + g * e


def _ffn_layer_kernel(h_ref, g_ref, wgu_ref, wd_ref, p_ref, gpost_ref, gple_ref, wproj_ref, wgate_ref, o_ref, *, ck):
    ff = wd_ref.shape[0]
    h = h_ref[...]
    z = _rms(h, g_ref[...]).astype(wgu_ref.dtype)
    acc = jnp.zeros(h.shape, F32)
    for c in range(ff // ck):
        g = _mm(z, wgu_ref[:, c * ck:(c + 1) * ck])
        u = _mm(z, wgu_ref[:, ff + c * ck:ff + (c + 1) * ck])
        acc = acc + _mm(_silu(g) * u, wd_ref[c * ck:(c + 1) * ck, :])
    o_ref[...] = _layer_tail(h, acc, p_ref, gpost_ref, gple_ref, wproj_ref, wgate_ref)


def _p_spec(p, layer, tm):
    return pl.BlockSpec((None, tm, p.shape[2]), lambda i: (layer, i, 0))


def _ffn_layer(h, g, wgu, wd, p, layer, gpost, gple, wproj, wgate, tm, ck=FF_CHUNK):
    m, d = h.shape
    row = pl.BlockSpec((tm, d), lambda i: (i, 0))
    vec = _resident((1, d))
    return pl.pallas_call(
        functools.partial(_ffn_layer_kernel, ck=ck),
        grid=(m // tm,),
        in_specs=[row, vec, _resident(wgu.shape), _resident(wd.shape), _p_spec(p, layer, tm), vec, vec,
                  _resident(wproj.shape), _resident(wgate.shape)],
        out_specs=row,
        out_shape=jax.ShapeDtypeStruct((m, d), F32),
        compiler_params=_cparams("arbitrary"),
        name="ffn_layer",
    )(h, g, wgu, wd, p, gpost, gple, wproj, wgate)


def _finish_kernel(h_ref, f_ref, p_ref, gpost_ref, gple_ref, wproj_ref, wgate_ref, o_ref):
    o_ref[...] = _layer_tail(h_ref[...], f_ref[...], p_ref, gpost_ref, gple_ref, wproj_ref, wgate_ref)


def _finish(h, f, p, layer, gpost, gple, wproj, wgate, tm):
    m, d = h.shape
    row = pl.BlockSpec((tm, d), lambda i: (i, 0))
    return pl.pallas_call(
        _finish_kernel,
        grid=(m // tm,),
        in_specs=[row, row, _p_spec(p, layer, tm), _resident((1, d)), _resident((1, d)),
                  _resident(wproj.shape), _resident(wgate.shape)],
        out_specs=row,
        out_shape=jax.ShapeDtypeStruct((m, d), F32),
        compiler_params=_cparams("arbitrary"),
        name="finish",
    )(h, f, p, gpost, gple, wproj, wgate)


def _qkv_kernel(h_ref, g_ref, w_ref, k_ref, v_ref, qb_ref, kb_ref, vb_ref):
    d = h_ref.shape[1]
    xn = _rms(h_ref[...], g_ref[...]).astype(w_ref.dtype)
    scale = (d // N_HEADS) ** -0.5
    qb_ref[...] = (_mm(xn, w_ref[:, :d]) * scale).astype(qb_ref.dtype)
    k = _mm(xn, w_ref[:, d:2 * d])
    v = _mm(xn, w_ref[:, 2 * d:])
    if k_ref.shape == k.shape:
        k_ref[...] = k
        v_ref[...] = v
    else:
        k_ref[...] = k.T
        v_ref[...] = v.T
    kb_ref[...] = k.astype(BF16)
    vb_ref[...] = v.astype(BF16)


def _qkv(h, g, w, tm, q_dtype=BF16, seq=None):
    m, d = h.shape
    row = pl.BlockSpec((tm, d), lambda i: (i, 0))
    if seq is None:
        kv_spec, kv_shape = row, jax.ShapeDtypeStruct((m, d), F32)
    else:
        per = seq // tm
        kv_spec = pl.BlockSpec((None, d, tm), lambda i: (i // per, 0, i % per))
        kv_shape = jax.ShapeDtypeStruct((m // seq, d, seq), F32)
    return pl.pallas_call(
        _qkv_kernel,
        grid=(m // tm,),
        in_specs=[row, _resident((1, d)), _resident(w.shape)],
        out_specs=[kv_spec, kv_spec, row, row, row],
        out_shape=[kv_shape, kv_shape, jax.ShapeDtypeStruct((m, d), q_dtype)]
        + [jax.ShapeDtypeStruct((m, d), BF16)] * 2,
        compiler_params=_cparams("arbitrary"),
        name="qkv",
    )(h, g, w)


def _sb_weights(z, sp, ls, tri, carry):
    hi = ls.astype(BF16)
    lo = (ls - hi.astype(F32)).astype(BF16)
    later = _dot(hi, tri) + _dot(lo, tri)
    return jnp.exp((z - sp) + later + carry)


def _sb_prompt_kernel(bias_ref, q_ref, k_ref, v_ref, o_ref, qs_ref, w_ref, acc_ref, carry_ref, *, hps, tk):
    tq, width = q_ref.shape
    hd = width // hps
    rows = hps * tq
    per = tq // tk
    grp = pl.program_id(1)
    i = pl.program_id(2)
    lane = lax.broadcasted_iota(jnp.int32, (tq, width), 1)
    q = q_ref[...]
    for h in range(hps):
        qs_ref[h * tq:(h + 1) * tq, :] = jnp.where((lane >= h * hd) & (lane < (h + 1) * hd), q, jnp.zeros_like(q))
    acc_ref[...] = jnp.zeros_like(acc_ref)
    carry_ref[...] = jnp.zeros_like(carry_ref)
    r2 = lax.broadcasted_iota(jnp.int32, (tk, tk), 0)
    c2 = lax.broadcasted_iota(jnp.int32, (tk, tk), 1)
    neg_tri = jnp.where(r2 > c2, -1.0, 0.0).astype(BF16)
    ahead = (lax.broadcasted_iota(jnp.int32, (rows, tk), 1)
             - lax.rem(lax.broadcasted_iota(jnp.int32, (rows, tk), 0), tq))
    bias = jnp.concatenate([jnp.full((tq, 1), bias_ref[hps * grp + h], F32) for h in range(hps)], axis=0)

    def weights(j, diag):
        z = _dot_nt(qs_ref[...], k_ref[pl.ds(pl.multiple_of(j * tk, tk), tk), :]) + bias
        sp = jnp.maximum(z, 0.0) + jnp.log(1.0 + jnp.exp2(jnp.abs(z) * -LOG2E))
        if diag:
            causal = ahead < i * tq - j * tk
            sp = jnp.where(causal, sp, 0.0)
        later = _dot(sp.astype(BF16), neg_tri)
        w = jnp.exp((z - sp) + later + carry_ref[...])
        if diag:
            w = jnp.where(causal, w, 0.0)
        w_ref[...] = w.astype(BF16)
        carry_ref[...] -= jnp.sum(sp, axis=-1, keepdims=True)

    def values(j):
        acc_ref[...] += _dot(w_ref[...], v_ref[pl.ds(pl.multiple_of(j * tk, tk), tk), :])

    top = (i + 1) * per - 1
    weights(top, True)
    for back in range(1, per):
        values(top - back + 1)
        weights(top - back, True)

    def body(jj, c):
        j = i * per - 1 - jj
        values(j + 1)
        weights(j, False)
        return c

    lax.fori_loop(0, i * per, body, 0)
    values(0)
    out = acc_ref[0:tq, :]
    for h in range(1, hps):
        out = jnp.where(lane >= h * hd, acc_ref[h * tq:(h + 1) * tq, :], out)
    o_ref[...] = out.astype(o_ref.dtype)


def _sb_prompt(qb, kb, vb, bias, tq, tk, hps=HEADS_PER_STEP):
    bsz, t, d = qb.shape
    width = hps * (d // N_HEADS)
    qspec = pl.BlockSpec((None, tq, width), lambda b, g, i: (b, i, g))
    kvspec = pl.BlockSpec((None, t, width), lambda b, g, i: (b, 0, g))
    return pl.pallas_call(
        functools.partial(_sb_prompt_kernel, hps=hps, tk=tk),
        grid=(bsz, d // width, t // tq),
        in_specs=[pl.BlockSpec(memory_space=pltpu.SMEM), qspec, kvspec, kvspec],
        out_specs=qspec,
        out_shape=jax.ShapeDtypeStruct((bsz, t, d), BF16),
        scratch_shapes=[pltpu.VMEM((hps * tq, width), BF16), pltpu.VMEM((hps * tq, tk), BF16),
                        pltpu.VMEM((hps * tq, width), F32),
                        pltpu.VMEM((hps * tq, 1), F32)],
        compiler_params=_cparams("arbitrary", "arbitrary", "arbitrary"),
        name="sb_prompt",
    )(bias, qb, kb, vb)


def _sb_decode_kernel(pt_ref, q_ref, bias_ref, *refs, pg):
    del pt_ref
    k_refs, v_refs = refs[:pg], refs[pg:2 * pg]
    o_ref, qb_ref, acc_ref, carry_ref = refs[2 * pg:]
    nh, _, page = k_refs[0].shape
    g = pl.program_id(1)

    @pl.when(g == 0)
    def _():
        acc_ref[...] = jnp.zeros_like(acc_ref)
        carry_ref[...] = jnp.zeros_like(carry_ref)
        qb_ref[...] = jnp.broadcast_to(q_ref[...], qb_ref.shape)

    row = lax.broadcasted_iota(jnp.int32, (page, page), 0)
    col = lax.broadcasted_iota(jnp.int32, (page, page), 1)
    tri = jnp.where(row > col, 1.0, 0.0).astype(BF16)
    z = jnp.concatenate([jnp.sum(qb_ref[h] * k_refs[r][h], axis=0, keepdims=True)
                         for r in range(pg) for h in range(nh)], axis=0)
    z = z + jnp.concatenate([bias_ref[...]] * pg, axis=0)
    sp = _softplus(z)
    ls = -sp
    total = jnp.sum(ls, axis=-1, keepdims=True)
    carries = [carry_ref[...]]
    for r in range(pg):
        carries.append(carries[-1] + total[r * nh:(r + 1) * nh])
    carry_ref[...] = carries[pg]
    w = _sb_weights(z, sp, ls, tri, jnp.concatenate(carries[:pg], axis=0))
    for h in range(nh):
        part = w[h:h + 1, :] * v_refs[0][h]
        for r in range(1, pg):
            part = part + w[r * nh + h:r * nh + h + 1, :] * v_refs[r][h]
        acc_ref[h] += part

    @pl.when(g == pl.num_programs(1) - 1)
    def _():
        o_ref[...] = jnp.sum(acc_ref[...], axis=-1)


def _sb_decode(q, cache_k, cache_v, page_table, bias, pg):
    s, nh, hd, _ = q.shape
    n_pages = page_table.shape[1]
    page = cache_k.shape[3]

    def page_spec(r):
        return pl.BlockSpec((None, nh, hd, page), lambda b, g, pt: (pt[b, n_pages - 1 - (g * pg + r)], 0, 0, 0))

    grid_spec = pltpu.PrefetchScalarGridSpec(
        num_scalar_prefetch=1,
        grid=(s, n_pages // pg),
        in_specs=[pl.BlockSpec((None, nh, hd, 1), lambda b, g, pt: (b, 0, 0, 0)),
                  pl.BlockSpec((nh, 1), lambda b, g, pt: (0, 0))] + [page_spec(r) for r in range(pg)] * 2,
        out_specs=pl.BlockSpec((None, nh, hd), lambda b, g, pt: (b, 0, 0)),
        scratch_shapes=[pltpu.VMEM((nh, hd, page), F32), pltpu.VMEM((nh, hd, page), F32), pltpu.VMEM((nh, 1), F32)],
    )
    return pl.pallas_call(
        functools.partial(_sb_decode_kernel, pg=pg),
        grid_spec=grid_spec,
        out_shape=jax.ShapeDtypeStruct((s, nh, hd), F32),
        compiler_params=_cparams("arbitrary", "arbitrary"),
        name="sb_decode",
    )(page_table, q, bias, *([cache_k] * pg), *([cache_v] * pg))


def _mix_residual(h_ref, o_ref, wo_ref, gpost_ref, gffn_ref):
    h1 = h_ref[...] + _rms(_mm(o_ref[...], wo_ref[...]), gpost_ref[...])
    return h1, _rms(h1, gffn_ref[...])


def _top2(logits):
    idx = lax.broadcasted_iota(jnp.int32, logits.shape, 1)
    m1 = jnp.max(logits, axis=-1, keepdims=True)
    i1 = jnp.min(jnp.where(logits == m1, idx, LANES), axis=-1, keepdims=True)
    rest = jnp.where(idx == i1, NEG_BIG, logits)
    m2 = jnp.max(rest, axis=-1, keepdims=True)
    i2 = jnp.min(jnp.where(rest == m2, idx, LANES), axis=-1, keepdims=True)
    e = jnp.exp(m2 - m1)
    w1 = 1.0 / (1.0 + e)
    return idx, i1, i2, w1, e * w1


def _attn_out_kernel(h_ref, o_ref, wo_ref, gpost_ref, gffn_ref, wr_ref, br_ref, h1_ref, z_ref, comb_ref):
    h1, z = _mix_residual(h_ref, o_ref, wo_ref, gpost_ref, gffn_ref)
    h1_ref[...] = h1
    z_ref[...] = z.astype(z_ref.dtype)
    idx, i1, i2, w1, w2 = _top2(_mm(z, wr_ref[...]) + br_ref[...])
    comb_ref[...] = jnp.where(idx == i1, w1, 0.0) + jnp.where(idx == i2, w2, 0.0)


def _attn_out_route_kernel(h_ref, o_ref, wo_ref, gpost_ref, gffn_ref, wr_ref, br_ref, h1_ref, z_ref, cols_ref,
                           rows_ref):
    h1, z = _mix_residual(h_ref, o_ref, wo_ref, gpost_ref, gffn_ref)
    h1_ref[...] = h1
    z_ref[...] = z
    idx, i1, i2, w1, w2 = _top2(_mm(z, wr_ref[...]) + br_ref[...])
    cols = jnp.where(idx == 0, i1.astype(F32), jnp.where(idx == 1, i2.astype(F32),
                                                         jnp.where(idx == 2, w1, jnp.where(idx == 3, w2, 0.0))))
    cols_ref[...] = cols
    pick = (lax.broadcasted_iota(jnp.int32, (8, LANES), 0) == lax.broadcasted_iota(jnp.int32, (8, LANES), 1))
    rows_ref[...] = lax.dot_general(pick.astype(F32), cols, (((1,), (1,)), ((), ())),
                                    precision=lax.Precision.HIGHEST, preferred_element_type=F32)


def _attn_out(h, o, wo, gpost, gffn, wr, br, tm, route=False):
    m, d = h.shape
    row = pl.BlockSpec((tm, d), lambda i: (i, 0))
    comb = pl.BlockSpec((tm, LANES), lambda i: (i, 0))
    in_specs = [row, row, _resident(wo.shape), _resident((1, d)), _resident((1, d)), _resident(wr.shape),
                _resident(br.shape)]
    if route:
        return pl.pallas_call(
            _attn_out_route_kernel,
            grid=(m // tm,),
            in_specs=in_specs,
            out_specs=[row, row, comb, pl.BlockSpec((8, tm), lambda i: (0, i))],
            out_shape=[jax.ShapeDtypeStruct((m, d), F32), jax.ShapeDtypeStruct((m, d), F32),
                       jax.ShapeDtypeStruct((m, LANES), F32), jax.ShapeDtypeStruct((8, m), F32)],
            compiler_params=_cparams("arbitrary"),
            name="attn_out_route",
        )(h, o, wo, gpost, gffn, wr, br)
    return pl.pallas_call(
        _attn_out_kernel,
        grid=(m // tm,),
        in_specs=in_specs,
        out_specs=[row, row, comb],
        out_shape=[jax.ShapeDtypeStruct((m, d), F32), jax.ShapeDtypeStruct((m, d), BF16),
                   jax.ShapeDtypeStruct((m, LANES), F32)],
        compiler_params=_cparams("arbitrary"),
        name="attn_out",
    )(h, o, wo, gpost, gffn, wr, br)


def _moe_kernel(z_ref, comb_ref, wg_ref, wu_ref, wd_ref, f_ref, wgb_ref, wub_ref, wdb_ref):
    e = pl.program_id(0)
    c = pl.program_id(1)

    @pl.when((e == 0) & (c == 0))
    def _():
        f_ref[...] = jnp.zeros_like(f_ref)

    wg = wg_ref[...].astype(BF16)
    wu = wu_ref[...].astype(BF16)
    wd = wd_ref[...].astype(BF16)
    wgb_ref[...] = wg
    wub_ref[...] = wu
    wdb_ref[...] = wd
    z = z_ref[...]
    act = (_silu(_dot(z, wg)) * _dot(z, wu)).astype(BF16)
    lane = lax.broadcasted_iota(jnp.int32, comb_ref.shape, 1)
    ce = jnp.sum(jnp.where(lane == e, comb_ref[...], 0.0), axis=-1, keepdims=True)
    f_ref[...] += ce * _dot(act, wd)


def _moe(z, comb, wgu, wd, ck=FF_CHUNK):
    m, d = z.shape
    n_e, ff, _ = wd.shape
    fc = ff // ck
    rows = pl.BlockSpec((m, d), lambda e, c: (0, 0))
    up_down = [pl.BlockSpec((None, d, ck), lambda e, c: (e, 0, c)), pl.BlockSpec((None, ck, d), lambda e, c: (e, c, 0))]
    return pl.pallas_call(
        _moe_kernel,
        grid=(n_e, fc),
        in_specs=[rows, pl.BlockSpec((m, LANES), lambda e, c: (0, 0)), up_down[0],
                  pl.BlockSpec((None, d, ck), lambda e, c: (e, 0, fc + c)), up_down[1]],
        out_specs=[rows, up_down[0], up_down[0], up_down[1]],
        out_shape=[jax.ShapeDtypeStruct((m, d), F32), jax.ShapeDtypeStruct((n_e, d, ff), BF16),
                   jax.ShapeDtypeStruct((n_e, d, ff), BF16), jax.ShapeDtypeStruct((n_e, ff, d), BF16)],
        compiler_params=_cparams("arbitrary", "arbitrary"),
        name="moe",
    )(z, comb, wgu, wgu, wd)


def _route_plan(i1, i2, n_e, tm, n_tiles):
    m = i1.shape[0]
    experts = jnp.arange(n_e, dtype=jnp.int32)[None, :]
    hit = (experts == i1[:, None]).astype(jnp.int32) + (experts == i2[:, None]).astype(jnp.int32)
    upto = jnp.cumsum(hit, axis=0)
    before = upto - hit
    padded = ((upto[-1] + tm - 1) // tm) * tm
    ends = jnp.cumsum(padded)
    starts = ends - padded
    slot1 = starts[i1] + jnp.take_along_axis(before, i1[:, None], axis=1)[:, 0]
    slot2 = starts[i2] + jnp.take_along_axis(before, i2[:, None], axis=1)[:, 0]
    slots = jnp.stack([slot1, slot2]).astype(jnp.int32)
    token = jnp.tile(jnp.arange(m, dtype=jnp.int32), 2)
    source = jnp.zeros((n_tiles * tm,), jnp.int32).at[slots.reshape(-1)].set(token).reshape(n_tiles, 1, tm)
    tile_start = jnp.arange(n_tiles, dtype=jnp.int32) * tm
    tile_expert = jnp.minimum(jnp.sum(tile_start[:, None] >= ends[None, :], axis=1), n_e - 1).astype(jnp.int32)
    return slots, source, tile_expert, (ends[-1:] // tm).astype(jnp.int32)


GATHER_UNROLL = 8


def _start_row_gather(src_ref, row_of, dst_ref, sem):
    rows = dst_ref.shape[0]

    def body(r, carry):
        pltpu.make_async_copy(src_ref.at[pl.ds(row_of(r), 1), :], dst_ref.at[pl.ds(r, 1), :], sem).start()
        return carry

    lax.fori_loop(0, rows, body, 0, unroll=GATHER_UNROLL)


def _wait_row_gather(src_ref, dst_ref, sem):
    pltpu.make_async_copy(src_ref.at[pl.ds(0, dst_ref.shape[0]), :], dst_ref, sem).wait()


def _experts_kernel(te_ref, nu_ref, src_ref, nxt_ref, z_ref, wg_ref, wu_ref, wd_ref, y_ref, xg_ref, xb_ref, sem):
    del te_ref
    i = pl.program_id(0)
    c = pl.program_id(1)
    n_used = nu_ref[0]
    used = i < n_used

    @pl.when(used & (c == 0))
    def _():
        buf = i % 2

        @pl.when(i == 0)
        def _():
            _start_row_gather(z_ref, lambda r: src_ref[0, r], xg_ref.at[0], sem.at[0])

        _wait_row_gather(z_ref, xg_ref.at[buf], sem.at[buf])
        xb_ref[...] = xg_ref[buf].astype(BF16)

        @pl.when(i + 1 < n_used)
        def _():
            _start_row_gather(z_ref, lambda r: nxt_ref[0, r], xg_ref.at[1 - buf], sem.at[1 - buf])

    @pl.when(used)
    def _():
        xb = xb_ref[...]
        act = (_silu(_dot(xb, wg_ref[...])) * _dot(xb, wu_ref[...])).astype(BF16)
        y = _dot(act, wd_ref[...])

        @pl.when(c == 0)
        def _():
            y_ref[...] = y

        @pl.when(c > 0)
        def _():
            y_ref[...] += y

    @pl.when(jnp.logical_not(used) & (c == 0))
    def _():
        y_ref[...] = jnp.zeros_like(y_ref)


def _experts(z, source, tile_expert, n_used, wg, wu, wd, ck=EXPERT_FF_CHUNK):
    n_tiles, _, tm = source.shape
    d = z.shape[1]
    ff = wd.shape[1]
    assert ff % ck == 0, (ff, ck)
    fc = ff // ck
    grid_spec = pltpu.PrefetchScalarGridSpec(
        num_scalar_prefetch=2,
        grid=(n_tiles, fc),
        in_specs=[pl.BlockSpec((None, 1, tm), lambda i, c, te, nu: (i, 0, 0), memory_space=pltpu.SMEM),
                  pl.BlockSpec((None, 1, tm), lambda i, c, te, nu: (jnp.minimum(i + 1, n_tiles - 1), 0, 0),
                               memory_space=pltpu.SMEM),
                  pl.BlockSpec(memory_space=pl.ANY),
                  pl.BlockSpec((None, d, ck), lambda i, c, te, nu: (te[i], 0, c)),
                  pl.BlockSpec((None, d, ck), lambda i, c, te, nu: (te[i], 0, c)),
                  pl.BlockSpec((None, ck, d), lambda i, c, te, nu: (te[i], c, 0))],
        out_specs=pl.BlockSpec((tm, d), lambda i, c, te, nu: (i, 0)),
        scratch_shapes=[pltpu.VMEM((2, tm, d), z.dtype), pltpu.VMEM((tm, d), BF16), pltpu.SemaphoreType.DMA((2,))],
    )
    return pl.pallas_call(
        _experts_kernel,
        grid_spec=grid_spec,
        out_shape=jax.ShapeDtypeStruct((n_tiles * tm, d), F32),
        compiler_params=_cparams("arbitrary", "arbitrary"),
        name="moe_experts",
    )(tile_expert, n_used, source, source, z, wg, wu, wd)


def _finish_routed_kernel(slots_ref, nxt_ref, h_ref, cols_ref, p_ref, gpost_ref, gple_ref, wproj_ref, wgate_ref,
                          ys_ref, o_ref, y_ref, sem):
    i = pl.program_id(0)
    buf = i % 2

    def start(rows_ref, b):
        for k in range(TOP_K):
            _start_row_gather(ys_ref, lambda r, k=k: rows_ref[k, r], y_ref.at[b, k], sem.at[b])

    @pl.when(i == 0)
    def _():
        start(slots_ref, 0)

    for k in range(TOP_K):
        _wait_row_gather(ys_ref, y_ref.at[buf, k], sem.at[buf])

    @pl.when(i + 1 < pl.num_programs(0))
    def _():
        start(nxt_ref, 1 - buf)

    cols = cols_ref[...]
    f = cols[:, 2:3] * y_ref[buf, 0] + cols[:, 3:4] * y_ref[buf, 1]
    o_ref[...] = _layer_tail(h_ref[...], f, p_ref, gpost_ref, gple_ref, wproj_ref, wgate_ref)


def _finish_routed(h, ys, slots, cols, p, layer, gpost, gple, wproj, wgate, tm):
    m, d = h.shape
    n = m // tm
    row = pl.BlockSpec((tm, d), lambda i: (i, 0))
    return pl.pallas_call(
        _finish_routed_kernel,
        grid=(n,),
        in_specs=[pl.BlockSpec((TOP_K, tm), lambda i: (0, i), memory_space=pltpu.SMEM),
                  pl.BlockSpec((TOP_K, tm), lambda i: (0, jnp.minimum(i + 1, n - 1)), memory_space=pltpu.SMEM), row,
                  pl.BlockSpec((tm, LANES), lambda i: (i, 0)), _p_spec(p, layer, tm),
                  _resident((1, d)), _resident((1, d)), _resident(wproj.shape), _resident(wgate.shape),
                  pl.BlockSpec(memory_space=pl.ANY)],
        out_specs=row,
        out_shape=jax.ShapeDtypeStruct((m, d), F32),
        scratch_shapes=[pltpu.VMEM((2, TOP_K, tm, d), F32), pltpu.SemaphoreType.DMA((2,))],
        compiler_params=_cparams("arbitrary"),
        name="finish_routed",
    )(slots, slots, h, cols, p, gpost, gple, wproj, wgate, ys)


def _tile(m, pref):
    return pref if m % pref == 0 else m


def kernel(x_prompt, x_sample, state_conv, state_lru, cache_k, cache_v, page_table, p_prompt, p_sample, norm_mix_pre, norm_mix_post, norm_ffn_pre, norm_ffn_post, norm_ple, rec_w_in, rec_conv_w, rec_conv_b, rec_w_a, rec_b_a, rec_w_x, rec_b_x, rec_lambda, rec_w_out, att_w_qkv, att_w_o, att_sb_bias, ffn_w_gu, ffn_w_down, moe_w_router, moe_b_router, moe_w_gu, moe_w_down, ple_w_proj, ple_w_gate):
    bsz, t, d = x_prompt.shape
    s = x_sample.shape[0]
    depth = norm_mix_pre.shape[0]
    mp = bsz * t
    tmp = _tile(mp, ROW_TILE)
    vec = lambda a: a.reshape(1, -1)
    bf = lambda a: a.astype(BF16)
    hp = x_prompt.reshape(mp, d)
    hs = x_sample.reshape(s, d)
    keep = lambda a: a
    outs = {k: [] for k in ("conv_p", "lru_p", "k_p", "v_p", "conv_s", "lru_s", "k_s", "v_s")}
    pp = p_prompt.reshape(depth, mp, -1)
    ps = p_sample.reshape(depth, s, -1)
    for i in range(depth):
        j = i // 2
        tail = (vec(norm_ffn_post[i]), vec(norm_ple[i]))
        if i % 2 == 0:
            def rec(cast):
                return (vec(norm_mix_pre[i]), cast(rec_w_in[j]), rec_conv_w[j], vec(rec_conv_b[j]), cast(rec_w_a[j]),
                        vec(rec_b_a[j]), cast(rec_w_x[j]), vec(rec_b_x[j]), vec(rec_lambda[j]), cast(rec_w_out[j]),
                        vec(norm_mix_post[i]))
            hp, conv_new, h_new = _rec_prompt(hp.reshape(bsz, t, d), *rec(bf), tt=_tile(t, SCAN_TILE))
            outs["conv_p"].append(conv_new)
            outs["lru_p"].append(h_new.reshape(bsz, d))
            hs, conv_new, h_new = _rec_step(hs, jnp.swapaxes(state_conv[j], 0, 1), state_lru[j], *rec(keep))
            outs["conv_s"].append(jnp.swapaxes(conv_new, 0, 1))
            outs["lru_s"].append(h_new)
            hp = _ffn_layer(hp.reshape(mp, d), vec(norm_ffn_pre[i]), bf(ffn_w_gu[j]), bf(ffn_w_down[j]), pp, i, *tail,
                            bf(ple_w_proj[i]), bf(ple_w_gate[i]), tmp)
            hs = _ffn_layer(hs, vec(norm_ffn_pre[i]), ffn_w_gu[j], ffn_w_down[j], ps, i, *tail, ple_w_proj[i],
                            ple_w_gate[i], s)
        else:
            hd = d // N_HEADS
            kp, vp, qb, kb, vb = _qkv(hp, vec(norm_mix_pre[i]), bf(att_w_qkv[j]), _tile(t, ROW_TILE), seq=t)
            outs["k_p"].append(kp.reshape(bsz, N_HEADS, hd, t).transpose(0, 3, 1, 2))
            outs["v_p"].append(vp.reshape(bsz, N_HEADS, hd, t).transpose(0, 3, 1, 2))
            op = _sb_prompt(qb.reshape(bsz, t, d), kb.reshape(bsz, t, d), vb.reshape(bsz, t, d), att_sb_bias[j],
                            tq=_tile(t, QUERY_TILE), tk=_tile(t, KEY_TILE))
            ks, vs, qs, _, _ = _qkv(hs, vec(norm_mix_pre[i]), att_w_qkv[j], s, q_dtype=F32)
            outs["k_s"].append(ks.reshape(s, 1, N_HEADS, hd))
            outs["v_s"].append(vs.reshape(s, 1, N_HEADS, hd))
            n_pages = page_table.shape[1]
            os_ = _sb_decode(qs.reshape(s, N_HEADS, hd, 1), jnp.transpose(cache_k[j], (0, 2, 3, 1)),
                             jnp.transpose(cache_v[j], (0, 2, 3, 1)), page_table, att_sb_bias[j].reshape(N_HEADS, 1),
                             pg=PAGES_PER_STEP if n_pages % PAGES_PER_STEP == 0 else 1)
            wr = jnp.pad(moe_w_router[j], ((0, 0), (0, LANES - N_EXPERTS)))
            br = jnp.pad(moe_b_router[j], (0, LANES - N_EXPERTS), constant_values=NEG_BIG).reshape(1, LANES)
            norms = (vec(norm_mix_post[i]), vec(norm_ffn_pre[i]))
            hp, zp, cols, rows = _attn_out(hp, op.reshape(mp, d), bf(att_w_o[j]), *norms, bf(wr), br, tmp, route=True)
            hs, zs, cs = _attn_out(hs, os_.reshape(s, d), att_w_o[j], *norms, wr, br, s)
            fs, wg, wu, wd = _moe(zs, cs, moe_w_gu[j], moe_w_down[j])
            tme = tmp
            n_tiles = (TOP_K * mp + N_EXPERTS * (tme - 1)) // tme
            slots, source, tile_expert, n_used = _route_plan(rows[0].astype(jnp.int32), rows[1].astype(jnp.int32),
                                                             N_EXPERTS, tme, n_tiles)
            ys = _experts(zp, source, tile_expert, n_used, wg, wu, wd)
            hp = _finish_routed(hp, ys, slots, cols, pp, i, *tail, bf(ple_w_proj[i]), bf(ple_w_gate[i]), tmp)
            hs = _finish(hs, fs, ps, i, *tail, ple_w_proj[i], ple_w_gate[i], s)
    st = lambda k: jnp.stack(outs[k])
    return (hp.reshape(bsz, t, d), hs.reshape(s, 1, d), st("conv_p"), st("lru_p"), st("k_p"), st("v_p"),
            st("conv_s"), st("lru_s"), st("k_s"), st("v_s"))
```

```python
import functools
import math

import jax
import jax.numpy as jnp
from jax import lax
from jax.experimental import pallas as pl
from jax.experimental.pallas import tpu as pltpu

F32 = jnp.float32
BF16 = jnp.bfloat16

RMS_EPS = 1e-6
LRU_C = 8.0
N_HEADS = 16
N_LRU_BLOCKS = 8
CONV_W = 4
N_EXPERTS = 8
TOP_K = 2
LANES = 128
SUBLANES = 8
LOG2E = 1.4426950408889634
NEG_BIG = -1e30
VMEM_LIMIT = 56 * 1024 * 1024

ROW_TILE = 512
SCAN_TILE = 256
KEY_TILE = 256
QUERY_TILE = 2 * KEY_TILE
HEADS_PER_STEP = 4
PAGES_PER_STEP = 16
FF_CHUNK = 512
EXPERT_FF_CHUNK = 1792


def _cparams(*sem):
    return pltpu.CompilerParams(dimension_semantics=sem, vmem_limit_bytes=VMEM_LIMIT)


def _resident(shape):
    return pl.BlockSpec(shape, lambda *_: (0,) * len(shape), pipeline_mode=pl.Buffered(1))


def _dot(a, b):
    return jnp.dot(a, b, preferred_element_type=F32)


def _mm(x, w):
    if w.dtype == F32:
        return jnp.dot(x.astype(F32), w, precision=lax.Precision.HIGHEST, preferred_element_type=F32)
    return jnp.dot(x.astype(w.dtype), w, preferred_element_type=F32)


def _dot_nt(a, b):
    return lax.dot_general(a, b, (((1,), (1,)), ((), ())), preferred_element_type=F32)


def _rms(x, g):
    return x * lax.rsqrt(jnp.mean(x * x, axis=-1, keepdims=True) + RMS_EPS) * g


def _softplus(x):
    return jnp.maximum(x, 0.0) + jnp.log1p(jnp.exp(-jnp.abs(x)))


def _gelu(x):
    c = math.sqrt(2.0 / math.pi)
    return 0.5 * x * (1.0 + jnp.tanh(c * (x + 0.044715 * (x * x * x))))


def _silu(x):
    return x * jax.nn.sigmoid(x)


def _lru_gates(xc, wa_ref, ba, wx_ref, bx, lam):
    blk = xc.shape[1] // N_LRU_BLOCKS
    ra, rx = [], []
    for n in range(N_LRU_BLOCKS):
        xb = xc[:, n * blk:(n + 1) * blk]
        ra.append(_mm(xb, wa_ref[n]))
        rx.append(_mm(xb, wx_ref[n]))
    r = jax.nn.sigmoid(jnp.concatenate(ra, axis=1) + ba)
    ig = jax.nn.sigmoid(jnp.concatenate(rx, axis=1) + bx)
    log_a = -LRU_C * r * _softplus(-lam)
    a = jnp.exp(log_a)
    gx = jnp.sqrt(-jnp.tanh(log_a) * (a * a + 1.0)) * ig * xc
    return a, gx


def _rec_prompt_kernel(x_ref, gpre_ref, win_ref, cw_ref, cb_ref, wa_ref, ba_ref, wx_ref, bx_ref, lam_ref,
                       wout_ref, gpost_ref, h_ref, conv_ref, hlast_ref, tail_ref, hc_ref):
    tt, d = x_ref.shape

    @pl.when(pl.program_id(1) == 0)
    def _():
        tail_ref[...] = jnp.zeros_like(tail_ref)
        hc_ref[...] = jnp.zeros_like(hc_ref)

    x = x_ref[...]
    xn = _rms(x, gpre_ref[...]).astype(win_ref.dtype)
    gate = _gelu(_mm(xn, win_ref[:, :d]))
    xr = _mm(xn, win_ref[:, d:])
    xpad = jnp.concatenate([tail_ref[...], xr], axis=0)
    xc = cb_ref[...] + cw_ref[3:4, :] * xr
    for back in range(1, CONV_W):
        xc = xc + cw_ref[CONV_W - 1 - back:CONV_W - back, :] * xpad[8 - back:8 - back + tt]
    tail_ref[...] = xr[tt - 8:]
    a, b = _lru_gates(xc, wa_ref, ba_ref[...], wx_ref, bx_ref[...], lam_ref[...])
    row = lax.rem(lax.broadcasted_iota(jnp.int32, (tt, 1), 0), SUBLANES)
    s = 1
    while s < SUBLANES:
        keep = row >= s
        a_sh = jnp.where(keep, pltpu.roll(a, s, 0), 1.0)
        b_sh = jnp.where(keep, pltpu.roll(b, s, 0), 0.0)
        b = a * b_sh + b
        a = a * a_sh
        s *= 2
    last = hc_ref[...]
    groups = []
    for g in range(tt // SUBLANES):
        rows = slice(g * SUBLANES, (g + 1) * SUBLANES)
        hg = b[rows] + a[rows] * last
        groups.append(hg)
        last = hg[SUBLANES - 1:]
    h = jnp.concatenate(groups, axis=0)
    hc_ref[...] = last
    y = _mm(h * gate, wout_ref[...])
    h_ref[...] = x + _rms(y, gpost_ref[...])
    conv_ref[...] = xr[tt - (CONV_W - 1):]
    hlast_ref[...] = h[tt - 1:]


def _rec_prompt(x, gpre, win, cw, cb, wa, ba, wx, bx, lam, wout, gpost, tt):
    bsz, t, d = x.shape
    vec = _resident((1, d))
    return pl.pallas_call(
        _rec_prompt_kernel,
        grid=(bsz, t // tt),
        in_specs=[pl.BlockSpec((None, tt, d), lambda b, i: (b, i, 0)), vec, _resident((d, 2 * d)),
                  _resident((CONV_W, d)), vec, _resident(wa.shape), vec, _resident(wx.shape), vec, vec,
                  _resident((d, d)), vec],
        out_specs=[pl.BlockSpec((None, tt, d), lambda b, i: (b, i, 0)),
                   pl.BlockSpec((None, CONV_W - 1, d), lambda b, i: (b, 0, 0)),
                   pl.BlockSpec((None, 1, d), lambda b, i: (b, 0, 0))],
        out_shape=[jax.ShapeDtypeStruct((bsz, t, d), F32), jax.ShapeDtypeStruct((bsz, CONV_W - 1, d), F32),
                   jax.ShapeDtypeStruct((bsz, 1, d), F32)],
        scratch_shapes=[pltpu.VMEM((8, d), F32), pltpu.VMEM((1, d), F32)],
        compiler_params=_cparams("arbitrary", "arbitrary"),
        name="rec_prompt",
    )(x, gpre, win, cw, cb, wa, ba, wx, bx, lam, wout, gpost)


def _rec_step_kernel(x_ref, sc_ref, h0_ref, gpre_ref, win_ref, cw_ref, cb_ref, wa_ref, ba_ref, wx_ref, bx_ref,
                     lam_ref, wout_ref, gpost_ref, h_ref, conv_ref, hnew_ref):
    d = x_ref.shape[1]
    x = x_ref[...]
    xn = _rms(x, gpre_ref[...]).astype(win_ref.dtype)
    gate = _gelu(_mm(xn, win_ref[:, :d]))
    xr = _mm(xn, win_ref[:, d:])
    xc = cb_ref[...] + cw_ref[CONV_W - 1:CONV_W, :] * xr
    for k in range(CONV_W - 1):
        xc = xc + cw_ref[k:k + 1, :] * sc_ref[k]
    a, b = _lru_gates(xc, wa_ref, ba_ref[...], wx_ref, bx_ref[...], lam_ref[...])
    h = a * h0_ref[...] + b
    y = _mm(h * gate, wout_ref[...])
    h_ref[...] = x + _rms(y, gpost_ref[...])
    for k in range(CONV_W - 2):
        conv_ref[k] = sc_ref[k + 1]
    conv_ref[CONV_W - 2] = xr
    hnew_ref[...] = h


def _rec_step(x, sc, h0, gpre, win, cw, cb, wa, ba, wx, bx, lam, wout, gpost):
    rows, d = x.shape
    return pl.pallas_call(
        _rec_step_kernel,
        out_shape=[jax.ShapeDtypeStruct((rows, d), F32), jax.ShapeDtypeStruct((CONV_W - 1, rows, d), F32),
                   jax.ShapeDtypeStruct((rows, d), F32)],
        compiler_params=pltpu.CompilerParams(vmem_limit_bytes=VMEM_LIMIT),
        name="rec_step",
    )(x, sc, h0, gpre, win, cw, cb, wa, ba, wx, bx, lam, wout, gpost)


def _layer_tail(h, f, p_ref, gpost_ref, gple_ref, wproj_ref, wgate_ref):
    h2 = h + _rms(f, gpost_ref[...])
    e = _mm(p_ref[...], wproj_ref[...])
    g = jax.nn.sigmoid(_mm(_rms(h2, gple_ref[...]), wgate_ref[...]))
    return h2 + g * e


def _ffn_layer_kernel(h_ref, g_ref, wgu_ref, wd_ref, p_ref, gpost_ref, gple_ref, wproj_ref, wgate_ref, o_ref, *, ck):
    ff = wd_ref.shape[0]
    h = h_ref[...]
    z = _rms(h, g_ref[...]).astype(wgu_ref.dtype)
    acc = jnp.zeros(h.shape, F32)
    for c in range(ff // ck):
        g = _mm(z, wgu_ref[:, c * ck:(c + 1) * ck])
        u = _mm(z, wgu_ref[:, ff + c * ck:ff + (c + 1) * ck])
        acc = acc + _mm(_silu(g) * u, wd_ref[c * ck:(c + 1) * ck, :])
    o_ref[...] = _layer_tail(h, acc, p_ref, gpost_ref, gple_ref, wproj_ref, wgate_ref)


def _p_spec(p, layer, tm):
    return pl.BlockSpec((None, tm, p.shape[2]), lambda i: (layer, i, 0))


def _ffn_layer(h, g, wgu, wd, p, layer, gpost, gple, wproj, wgate, tm, ck=FF_CHUNK):
    m, d = h.shape
    row = pl.BlockSpec((tm, d), lambda i: (i, 0))
    vec = _resident((1, d))
    return pl.pallas_call(
        functools.partial(_ffn_layer_kernel, ck=ck),
        grid=(m // tm,),
        in_specs=[row, vec, _resident(wgu.shape), _resident(wd.shape), _p_spec(p, layer, tm), vec, vec,
                  _resident(wproj.shape), _resident(wgate.shape)],
        out_specs=row,
        out_shape=jax.ShapeDtypeStruct((m, d), F32),
        compiler_params=_cparams("arbitrary"),
        name="ffn_layer",
    )(h, g, wgu, wd, p, gpost, gple, wproj, wgate)


def _finish_kernel(h_ref, f_ref, p_ref, gpost_ref, gple_ref, wproj_ref, wgate_ref, o_ref):
    o_ref[...] = _layer_tail(h_ref[...], f_ref[...], p_ref, gpost_ref, gple_ref, wproj_ref, wgate_ref)


def _finish(h, f, p, layer, gpost, gple, wproj, wgate, tm):
    m, d = h.shape
    row = pl.BlockSpec((tm, d), lambda i: (i, 0))
    return pl.pallas_call(
        _finish_kernel,
        grid=(m // tm,),
        in_specs=[row, row, _p_spec(p, layer, tm), _resident((1, d)), _resident((1, d)),
                  _resident(wproj.shape), _resident(wgate.shape)],
        out_specs=row,
        out_shape=jax.ShapeDtypeStruct((m, d), F32),
        compiler_params=_cparams("arbitrary"),
        name="finish",
    )(h, f, p, gpost, gple, wproj, wgate)


def _qkv_kernel(h_ref, g_ref, w_ref, k_ref, v_ref, qb_ref, kb_ref, vb_ref):
    d = h_ref.shape[1]
    xn = _rms(h_ref[...], g_ref[...]).astype(w_ref.dtype)
    scale = (d // N_HEADS) ** -0.5
    qb_ref[...] = (_mm(xn, w_ref[:, :d]) * scale).astype(qb_ref.dtype)
    k = _mm(xn, w_ref[:, d:2 * d])
    v = _mm(xn, w_ref[:, 2 * d:])
    if k_ref.shape == k.shape:
        k_ref[...] = k
        v_ref[...] = v
    else:
        k_ref[...] = k.T
        v_ref[...] = v.T
    kb_ref[...] = k.astype(BF16)
    vb_ref[...] = v.astype(BF16)


def _qkv(h, g, w, tm, q_dtype=BF16, seq=None):
    m, d = h.shape
    row = pl.BlockSpec((tm, d), lambda i: (i, 0))
    if seq is None:
        kv_spec, kv_shape = row, jax.ShapeDtypeStruct((m, d), F32)
    else:
        per = seq // tm
        kv_spec = pl.BlockSpec((None, d, tm), lambda i: (i // per, 0, i % per))
        kv_shape = jax.ShapeDtypeStruct((m // seq, d, seq), F32)
    return pl.pallas_call(
        _qkv_kernel,
        grid=(m // tm,),
        in_specs=[row, _resident((1, d)), _resident(w.shape)],
        out_specs=[kv_spec, kv_spec, row, row, row],
        out_shape=[kv_shape, kv_shape, jax.ShapeDtypeStruct((m, d), q_dtype)]
        + [jax.ShapeDtypeStruct((m, d), BF16)] * 2,
        compiler_params=_cparams("arbitrary"),
        name="qkv",
    )(h, g, w)


def _sb_weights(z, sp, ls, tri, carry):
    hi = ls.astype(BF16)
    lo = (ls - hi.astype(F32)).astype(BF16)
    later = _dot(hi, tri) + _dot(lo, tri)
    return jnp.exp((z - sp) + later + carry)


def _sb_prompt_kernel(bias_ref, q_ref, k_ref, v_ref, o_ref, qs_ref, w_ref, acc_ref, carry_ref, *, hps, tk):
    tq, width = q_ref.shape
    hd = width // hps
    rows = hps * tq
    per = tq // tk
    grp = pl.program_id(1)
    i = pl.program_id(2)
    lane = lax.broadcasted_iota(jnp.int32, (tq, width), 1)
    q = q_ref[...]
    for h in range(hps):
        qs_ref[h * tq:(h + 1) * tq, :] = jnp.where((lane >= h * hd) & (lane < (h + 1) * hd), q, jnp.zeros_like(q))
    acc_ref[...] = jnp.zeros_like(acc_ref)
    carry_ref[...] = jnp.zeros_like(carry_ref)
    r2 = lax.broadcasted_iota(jnp.int32, (tk, tk), 0)
    c2 = lax.broadcasted_iota(jnp.int32, (tk, tk), 1)
    neg_tri = jnp.where(r2 > c2, -1.0, 0.0).astype(BF16)
    ahead = (lax.broadcasted_iota(jnp.int32, (rows, tk), 1)
             - lax.rem(lax.broadcasted_iota(jnp.int32, (rows, tk), 0), tq))
    bias = jnp.concatenate([jnp.full((tq, 1), bias_ref[hps * grp + h], F32) for h in range(hps)], axis=0)

    def weights(j, diag, lo=0):
        n = tq - lo

        def part(x):
            return x[...] if lo == 0 else jnp.concatenate([x[h * tq + lo:(h + 1) * tq] for h in range(hps)], axis=0)

        z = _dot_nt(part(qs_ref), k_ref[pl.ds(pl.multiple_of(j * tk, tk), tk), :]) + part(bias)
        sp = jnp.maximum(z, 0.0) + jnp.log(1.0 + jnp.exp2(jnp.abs(z) * -LOG2E))
        if diag:
            causal = part(ahead) < i * tq - j * tk
            sp = jnp.where(causal, sp, 0.0)
        later = _dot(sp.astype(BF16), neg_tri)
        w = jnp.exp((z - sp) + later + part(carry_ref))
        if diag:
            w = jnp.where(causal, w, 0.0)
        w = w.astype(BF16)
        total = jnp.sum(sp, axis=-1, keepdims=True)
        if lo == 0:
            w_ref[...] = w
            carry_ref[...] -= total
        else:
            for h in range(hps):
                w_ref[h * tq:h * tq + lo, :] = jnp.zeros((lo, tk), BF16)
                w_ref[h * tq + lo:(h + 1) * tq, :] = w[h * n:(h + 1) * n]
                carry_ref[h * tq + lo:(h + 1) * tq, :] -= total[h * n:(h + 1) * n]

    def values(j):
        acc_ref[...] += _dot(w_ref[...], v_ref[pl.ds(pl.multiple_of(j * tk, tk), tk), :])

    top = (i + 1) * per - 1
    weights(top, True, lo=(per - 1) * tk)
    for back in range(1, per):
        values(top - back + 1)
        weights(top - back, True, lo=(per - 1 - back) * tk)

    def body(jj, c):
        j = i * per - 1 - jj
        values(j + 1)
        weights(j, False)
        return c

    lax.fori_loop(0, i * per, body, 0)
    values(0)
    out = acc_ref[0:tq, :]
    for h in range(1, hps):
        out = jnp.where(lane >= h * hd, acc_ref[h * tq:(h + 1) * tq, :], out)
    o_ref[...] = out.astype(o_ref.dtype)


def _sb_prompt(qb, kb, vb, bias, tq, tk, hps=HEADS_PER_STEP):
    bsz, t, d = qb.shape
    width = hps * (d // N_HEADS)
    qspec = pl.BlockSpec((None, tq, width), lambda b, g, i: (b, i, g))
    kvspec = pl.BlockSpec((None, t, width), lambda b, g, i: (b, 0, g))
    return pl.pallas_call(
        functools.partial(_sb_prompt_kernel, hps=hps, tk=tk),
        grid=(bsz, d // width, t // tq),
        in_specs=[pl.BlockSpec(memory_space=pltpu.SMEM), qspec, kvspec, kvspec],
        out_specs=qspec,
        out_shape=jax.ShapeDtypeStruct((bsz, t, d), BF16),
        scratch_shapes=[pltpu.VMEM((hps * tq, width), BF16), pltpu.VMEM((hps * tq, tk), BF16),
                        pltpu.VMEM((hps * tq, width), F32),
                        pltpu.VMEM((hps * tq, 1), F32)],
        compiler_params=_cparams("arbitrary", "arbitrary", "arbitrary"),
        name="sb_prompt",
    )(bias, qb, kb, vb)


def _sb_decode_kernel(pt_ref, q_ref, bias_ref, *refs, pg):
    del pt_ref
    k_refs, v_refs = refs[:pg], refs[pg:2 * pg]
    o_ref, qb_ref, acc_ref, carry_ref = refs[2 * pg:]
    nh, _, page = k_refs[0].shape
    g = pl.program_id(1)

    @pl.when(g == 0)
    def _():
        acc_ref[...] = jnp.zeros_like(acc_ref)
        carry_ref[...] = jnp.zeros_like(carry_ref)
        qb_ref[...] = jnp.broadcast_to(q_ref[...], qb_ref.shape)

    row = lax.broadcasted_iota(jnp.int32, (page, page), 0)
    col = lax.broadcasted_iota(jnp.int32, (page, page), 1)
    tri = jnp.where(row > col, 1.0, 0.0).astype(BF16)
    z = jnp.concatenate([jnp.sum(qb_ref[h] * k_refs[r][h], axis=0, keepdims=True)
                         for r in range(pg) for h in range(nh)], axis=0)
    z = z + jnp.concatenate([bias_ref[...]] * pg, axis=0)
    sp = _softplus(z)
    ls = -sp
    total = jnp.sum(ls, axis=-1, keepdims=True)
    carries = [carry_ref[...]]
    for r in range(pg):
        carries.append(carries[-1] + total[r * nh:(r + 1) * nh])
    carry_ref[...] = carries[pg]
    w = _sb_weights(z, sp, ls, tri, jnp.concatenate(carries[:pg], axis=0))
    for h in range(nh):
        part = w[h:h + 1, :] * v_refs[0][h]
        for r in range(1, pg):
            part = part + w[r * nh + h:r * nh + h + 1, :] * v_refs[r][h]
        acc_ref[h] += part

    @pl.when(g == pl.num_programs(1) - 1)
    def _():
        o_ref[...] = jnp.sum(acc_ref[...], axis=-1)


def _sb_decode(q, cache_k, cache_v, page_table, bias, pg):
    s, nh, hd, _ = q.shape
    n_pages = page_table.shape[1]
    page = cache_k.shape[3]

    def page_spec(r):
        return pl.BlockSpec((None, nh, hd, page), lambda b, g, pt: (pt[b, n_pages - 1 - (g * pg + r)], 0, 0, 0))

    grid_spec = pltpu.PrefetchScalarGridSpec(
        num_scalar_prefetch=1,
        grid=(s, n_pages // pg),
        in_specs=[pl.BlockSpec((None, nh, hd, 1), lambda b, g, pt: (b, 0, 0, 0)),
                  pl.BlockSpec((nh, 1), lambda b, g, pt: (0, 0))] + [page_spec(r) for r in range(pg)] * 2,
        out_specs=pl.BlockSpec((None, nh, hd), lambda b, g, pt: (b, 0, 0)),
        scratch_shapes=[pltpu.VMEM((nh, hd, page), F32), pltpu.VMEM((nh, hd, page), F32), pltpu.VMEM((nh, 1), F32)],
    )
    return pl.pallas_call(
        functools.partial(_sb_decode_kernel, pg=pg),
        grid_spec=grid_spec,
        out_shape=jax.ShapeDtypeStruct((s, nh, hd), F32),
        compiler_params=_cparams("arbitrary", "arbitrary"),
        name="sb_decode",
    )(page_table, q, bias, *([cache_k] * pg), *([cache_v] * pg))


def _mix_residual(h_ref, o_ref, wo_ref, gpost_ref, gffn_ref):
    h1 = h_ref[...] + _rms(_mm(o_ref[...], wo_ref[...]), gpost_ref[...])
    return h1, _rms(h1, gffn_ref[...])


def _top2(logits):
    idx = lax.broadcasted_iota(jnp.int32, logits.shape, 1)
    m1 = jnp.max(logits, axis=-1, keepdims=True)
    i1 = jnp.min(jnp.where(logits == m1, idx, LANES), axis=-1, keepdims=True)
    rest = jnp.where(idx == i1, NEG_BIG, logits)
    m2 = jnp.max(rest, axis=-1, keepdims=True)
    i2 = jnp.min(jnp.where(rest == m2, idx, LANES), axis=-1, keepdims=True)
    e = jnp.exp(m2 - m1)
    w1 = 1.0 / (1.0 + e)
    return idx, i1, i2, w1, e * w1


def _attn_out_kernel(h_ref, o_ref, wo_ref, gpost_ref, gffn_ref, wr_ref, br_ref, h1_ref, z_ref, comb_ref):
    h1, z = _mix_residual(h_ref, o_ref, wo_ref, gpost_ref, gffn_ref)
    h1_ref[...] = h1
    z_ref[...] = z.astype(z_ref.dtype)
    idx, i1, i2, w1, w2 = _top2(_mm(z, wr_ref[...]) + br_ref[...])
    comb_ref[...] = jnp.where(idx == i1, w1, 0.0) + jnp.where(idx == i2, w2, 0.0)


def _attn_out_route_kernel(h_ref, o_ref, wo_ref, gpost_ref, gffn_ref, wr_ref, br_ref, h1_ref, z_ref, cols_ref,
                           rows_ref):
    h1, z = _mix_residual(h_ref, o_ref, wo_ref, gpost_ref, gffn_ref)
    h1_ref[...] = h1
    z_ref[...] = z
    idx, i1, i2, w1, w2 = _top2(_mm(z, wr_ref[...]) + br_ref[...])
    cols = jnp.where(idx == 0, i1.astype(F32), jnp.where(idx == 1, i2.astype(F32),
                                                         jnp.where(idx == 2, w1, jnp.where(idx == 3, w2, 0.0))))
    cols_ref[...] = cols
    pick = (lax.broadcasted_iota(jnp.int32, (8, LANES), 0) == lax.broadcasted_iota(jnp.int32, (8, LANES), 1))
    rows_ref[...] = lax.dot_general(pick.astype(F32), cols, (((1,), (1,)), ((), ())),
                                    precision=lax.Precision.HIGHEST, preferred_element_type=F32)


def _attn_out(h, o, wo, gpost, gffn, wr, br, tm, route=False):
    m, d = h.shape
    row = pl.BlockSpec((tm, d), lambda i: (i, 0))
    comb = pl.BlockSpec((tm, LANES), lambda i: (i, 0))
    in_specs = [row, row, _resident(wo.shape), _resident((1, d)), _resident((1, d)), _resident(wr.shape),
                _resident(br.shape)]
    if route:
        return pl.pallas_call(
            _attn_out_route_kernel,
            grid=(m // tm,),
            in_specs=in_specs,
            out_specs=[row, row, comb, pl.BlockSpec((8, tm), lambda i: (0, i))],
            out_shape=[jax.ShapeDtypeStruct((m, d), F32), jax.ShapeDtypeStruct((m, d), F32),
                       jax.ShapeDtypeStruct((m, LANES), F32), jax.ShapeDtypeStruct((8, m), F32)],
            compiler_params=_cparams("arbitrary"),
            name="attn_out_route",
        )(h, o, wo, gpost, gffn, wr, br)
    return pl.pallas_call(
        _attn_out_kernel,
        grid=(m // tm,),
        in_specs=in_specs,
        out_specs=[row, row, comb],
        out_shape=[jax.ShapeDtypeStruct((m, d), F32), jax.ShapeDtypeStruct((m, d), BF16),
                   jax.ShapeDtypeStruct((m, LANES), F32)],
        compiler_params=_cparams("arbitrary"),
        name="attn_out",
    )(h, o, wo, gpost, gffn, wr, br)


def _moe_kernel(z_ref, comb_ref, wg_ref, wu_ref, wd_ref, f_ref, wgb_ref, wub_ref, wdb_ref):
    e = pl.program_id(0)
    c = pl.program_id(1)

    @pl.when((e == 0) & (c == 0))
    def _():
        f_ref[...] = jnp.zeros_like(f_ref)

    wg = wg_ref[...].astype(BF16)
    wu = wu_ref[...].astype(BF16)
    wd = wd_ref[...].astype(BF16)
    wgb_ref[...] = wg
    wub_ref[...] = wu
    wdb_ref[...] = wd
    z = z_ref[...]
    act = (_silu(_dot(z, wg)) * _dot(z, wu)).astype(BF16)
    lane = lax.broadcasted_iota(jnp.int32, comb_ref.shape, 1)
    ce = jnp.sum(jnp.where(lane == e, comb_ref[...], 0.0), axis=-1, keepdims=True)
    f_ref[...] += ce * _dot(act, wd)


def _moe(z, comb, wgu, wd, ck=FF_CHUNK):
    m, d = z.shape
    n_e, ff, _ = wd.shape
    fc = ff // ck
    rows = pl.BlockSpec((m, d), lambda e, c: (0, 0))
    up_down = [pl.BlockSpec((None, d, ck), lambda e, c: (e, 0, c)), pl.BlockSpec((None, ck, d), lambda e, c: (e, c, 0))]
    return pl.pallas_call(
        _moe_kernel,
        grid=(n_e, fc),
        in_specs=[rows, pl.BlockSpec((m, LANES), lambda e, c: (0, 0)), up_down[0],
                  pl.BlockSpec((None, d, ck), lambda e, c: (e, 0, fc + c)), up_down[1]],
        out_specs=[rows, up_down[0], up_down[0], up_down[1]],
        out_shape=[jax.ShapeDtypeStruct((m, d), F32), jax.ShapeDtypeStruct((n_e, d, ff), BF16),
                   jax.ShapeDtypeStruct((n_e, d, ff), BF16), jax.ShapeDtypeStruct((n_e, ff, d), BF16)],
        compiler_params=_cparams("arbitrary", "arbitrary"),
        name="moe",
    )(z, comb, wgu, wgu, wd)


def _route_plan(i1, i2, n_e, tm, n_tiles):
    m = i1.shape[0]
    experts = jnp.arange(n_e, dtype=jnp.int32)[None, :]
    hit = (experts == i1[:, None]).astype(jnp.int32) + (experts == i2[:, None]).astype(jnp.int32)
    upto = jnp.cumsum(hit, axis=0)
    before = upto - hit
    padded = ((upto[-1] + tm - 1) // tm) * tm
    ends = jnp.cumsum(padded)
    starts = ends - padded
    slot1 = starts[i1] + jnp.take_along_axis(before, i1[:, None], axis=1)[:, 0]
    slot2 = starts[i2] + jnp.take_along_axis(before, i2[:, None], axis=1)[:, 0]
    slots = jnp.stack([slot1, slot2]).astype(jnp.int32)
    token = jnp.tile(jnp.arange(m, dtype=jnp.int32), 2)
    source = jnp.zeros((n_tiles * tm,), jnp.int32).at[slots.reshape(-1)].set(token).reshape(n_tiles, 1, tm)
    tile_start = jnp.arange(n_tiles, dtype=jnp.int32) * tm
    tile_expert = jnp.minimum(jnp.sum(tile_start[:, None] >= ends[None, :], axis=1), n_e - 1).astype(jnp.int32)
    return slots, source, tile_expert, (ends[-1:] // tm).astype(jnp.int32)


GATHER_UNROLL = 8


def _start_row_gather(src_ref, row_of, dst_ref, sem):
    rows = dst_ref.shape[0]

    def body(r, carry):
        pltpu.make_async_copy(src_ref.at[pl.ds(row_of(r), 1), :], dst_ref.at[pl.ds(r, 1), :], sem).start()
        return carry

    lax.fori_loop(0, rows, body, 0, unroll=GATHER_UNROLL)


def _wait_row_gather(src_ref, dst_ref, sem):
    pltpu.make_async_copy(src_ref.at[pl.ds(0, dst_ref.shape[0]), :], dst_ref, sem).wait()


def _experts_kernel(te_ref, nu_ref, src_ref, nxt_ref, z_ref, wg_ref, wu_ref, wd_ref, y_ref, xg_ref, xb_ref, sem):
    del te_ref
    i = pl.program_id(0)
    c = pl.program_id(1)
    n_used = nu_ref[0]
    used = i < n_used

    @pl.when(used & (c == 0))
    def _():
        buf = i % 2

        @pl.when(i == 0)
        def _():
            _start_row_gather(z_ref, lambda r: src_ref[0, r], xg_ref.at[0], sem.at[0])

        _wait_row_gather(z_ref, xg_ref.at[buf], sem.at[buf])
        xb_ref[...] = xg_ref[buf].astype(BF16)

        @pl.when(i + 1 < n_used)
        def _():
            _start_row_gather(z_ref, lambda r: nxt_ref[0, r], xg_ref.at[1 - buf], sem.at[1 - buf])

    @pl.when(used)
    def _():
        xb = xb_ref[...]
        act = (_silu(_dot(xb, wg_ref[...])) * _dot(xb, wu_ref[...])).astype(BF16)
        y = _dot(act, wd_ref[...])

        @pl.when(c == 0)
        def _():
            y_ref[...] = y

        @pl.when(c > 0)
        def _():
            y_ref[...] += y

    @pl.when(jnp.logical_not(used) & (c == 0))
    def _():
        y_ref[...] = jnp.zeros_like(y_ref)


def _experts(z, source, tile_expert, n_used, wg, wu, wd, ck=EXPERT_FF_CHUNK):
    n_tiles, _, tm = source.shape
    d = z.shape[1]
    ff = wd.shape[1]
    assert ff % ck == 0, (ff, ck)
    fc = ff // ck
    grid_spec = pltpu.PrefetchScalarGridSpec(
        num_scalar_prefetch=2,
        grid=(n_tiles, fc),
        in_specs=[pl.BlockSpec((None, 1, tm), lambda i, c, te, nu: (i, 0, 0), memory_space=pltpu.SMEM),
                  pl.BlockSpec((None, 1, tm), lambda i, c, te, nu: (jnp.minimum(i + 1, n_tiles - 1), 0, 0),
                               memory_space=pltpu.SMEM),
                  pl.BlockSpec(memory_space=pl.ANY),
                  pl.BlockSpec((None, d, ck), lambda i, c, te, nu: (te[i], 0, c)),
                  pl.BlockSpec((None, d, ck), lambda i, c, te, nu: (te[i], 0, c)),
                  pl.BlockSpec((None, ck, d), lambda i, c, te, nu: (te[i], c, 0))],
        out_specs=pl.BlockSpec((tm, d), lambda i, c, te, nu: (i, 0)),
        scratch_shapes=[pltpu.VMEM((2, tm, d), z.dtype), pltpu.VMEM((tm, d), BF16), pltpu.SemaphoreType.DMA((2,))],
    )
    return pl.pallas_call(
        _experts_kernel,
        grid_spec=grid_spec,
        out_shape=jax.ShapeDtypeStruct((n_tiles * tm, d), F32),
        compiler_params=_cparams("arbitrary", "arbitrary"),
        name="moe_experts",
    )(tile_expert, n_used, source, source, z, wg, wu, wd)


def _finish_routed_kernel(slots_ref, nxt_ref, h_ref, cols_ref, p_ref, gpost_ref, gple_ref, wproj_ref, wgate_ref,
                          ys_ref, o_ref, y_ref, sem):
    i = pl.program_id(0)
    buf = i % 2

    def start(rows_ref, b):
        for k in range(TOP_K):
            _start_row_gather(ys_ref, lambda r, k=k: rows_ref[k, r], y_ref.at[b, k], sem.at[b])

    @pl.when(i == 0)
    def _():
        start(slots_ref, 0)

    for k in range(TOP_K):
        _wait_row_gather(ys_ref, y_ref.at[buf, k], sem.at[buf])

    @pl.when(i + 1 < pl.num_programs(0))
    def _():
        start(nxt_ref, 1 - buf)

    cols = cols_ref[...]
    f = cols[:, 2:3] * y_ref[buf, 0] + cols[:, 3:4] * y_ref[buf, 1]
    o_ref[...] = _layer_tail(h_ref[...], f, p_ref, gpost_ref, gple_ref, wproj_ref, wgate_ref)


def _finish_routed(h, ys, slots, cols, p, layer, gpost, gple, wproj, wgate, tm):
    m, d = h.shape
    n = m // tm
    row = pl.BlockSpec((tm, d), lambda i: (i, 0))
    return pl.pallas_call(
        _finish_routed_kernel,
        grid=(n,),
        in_specs=[pl.BlockSpec((TOP_K, tm), lambda i: (0, i), memory_space=pltpu.SMEM),
                  pl.BlockSpec((TOP_K, tm), lambda i: (0, jnp.minimum(i + 1, n - 1)), memory_space=pltpu.SMEM), row,
                  pl.BlockSpec((tm, LANES), lambda i: (i, 0)), _p_spec(p, layer, tm),
                  _resident((1, d)), _resident((1, d)), _resident(wproj.shape), _resident(wgate.shape),
                  pl.BlockSpec(memory_space=pl.ANY)],
        out_specs=row,
        out_shape=jax.ShapeDtypeStruct((m, d), F32),
        scratch_shapes=[pltpu.VMEM((2, TOP_K, tm, d), F32), pltpu.SemaphoreType.DMA((2,))],
        compiler_params=_cparams("arbitrary"),
        name="finish_routed",
    )(slots, slots, h, cols, p, gpost, gple, wproj, wgate, ys)


def _tile(m, pref):
    return pref if m % pref == 0 else m


def kernel(x_prompt, x_sample, state_conv, state_lru, cache_k, cache_v, page_table, p_prompt, p_sample, norm_mix_pre, norm_mix_post, norm_ffn_pre, norm_ffn_post, norm_ple, rec_w_in, rec_conv_w, rec_conv_b, rec_w_a, rec_b_a, rec_w_x, rec_b_x, rec_lambda, rec_w_out, att_w_qkv, att_w_o, att_sb_bias, ffn_w_gu, ffn_w_down, moe_w_router, moe_b_router, moe_w_gu, moe_w_down, ple_w_proj, ple_w_gate):
    bsz, t, d = x_prompt.shape
    s = x_sample.shape[0]
    depth = norm_mix_pre.shape[0]
    mp = bsz * t
    tmp = _tile(mp, ROW_TILE)
    vec = lambda a: a.reshape(1, -1)
    bf = lambda a: a.astype(BF16)
    hp = x_prompt.reshape(mp, d)
    hs = x_sample.reshape(s, d)
    keep = lambda a: a
    outs = {k: [] for k in ("conv_p", "lru_p", "k_p", "v_p", "conv_s", "lru_s", "k_s", "v_s")}
    pp = p_prompt.reshape(depth, mp, -1)
    ps = p_sample.reshape(depth, s, -1)
    for i in range(depth):
        j = i // 2
        tail = (vec(norm_ffn_post[i]), vec(norm_ple[i]))
        if i % 2 == 0:
            def rec(cast):
                return (vec(norm_mix_pre[i]), cast(rec_w_in[j]), rec_conv_w[j], vec(rec_conv_b[j]), cast(rec_w_a[j]),
                        vec(rec_b_a[j]), cast(rec_w_x[j]), vec(rec_b_x[j]), vec(rec_lambda[j]), cast(rec_w_out[j]),
                        vec(norm_mix_post[i]))
            hp, conv_new, h_new = _rec_prompt(hp.reshape(bsz, t, d), *rec(bf), tt=_tile(t, SCAN_TILE))
            outs["conv_p"].append(conv_new)
            outs["lru_p"].append(h_new.reshape(bsz, d))
            hs, conv_new, h_new = _rec_step(hs, jnp.swapaxes(state_conv[j], 0, 1), state_lru[j], *rec(keep))
            outs["conv_s"].append(jnp.swapaxes(conv_new, 0, 1))
            outs["lru_s"].append(h_new)
            hp = _ffn_layer(hp.reshape(mp, d), vec(norm_ffn_pre[i]), bf(ffn_w_gu[j]), bf(ffn_w_down[j]), pp, i, *tail,
                            bf(ple_w_proj[i]), bf(ple_w_gate[i]), tmp)
            hs = _ffn_layer(hs, vec(norm_ffn_pre[i]), ffn_w_gu[j], ffn_w_down[j], ps, i, *tail, ple_w_proj[i],
                            ple_w_gate[i], s)
        else:
            hd = d // N_HEADS
            kp, vp, qb, kb, vb = _qkv(hp, vec(norm_mix_pre[i]), bf(att_w_qkv[j]), _tile(t, ROW_TILE), seq=t)
            outs["k_p"].append(kp.reshape(bsz, N_HEADS, hd, t).transpose(0, 3, 1, 2))
            outs["v_p"].append(vp.reshape(bsz, N_HEADS, hd, t).transpose(0, 3, 1, 2))
            op = _sb_prompt(qb.reshape(bsz, t, d), kb.reshape(bsz, t, d), vb.reshape(bsz, t, d), att_sb_bias[j],
                            tq=_tile(t, QUERY_TILE), tk=_tile(t, KEY_TILE))
            ks, vs, qs, _, _ = _qkv(hs, vec(norm_mix_pre[i]), att_w_qkv[j], s, q_dtype=F32)
            outs["k_s"].append(ks.reshape(s, 1, N_HEADS, hd))
            outs["v_s"].append(vs.reshape(s, 1, N_HEADS, hd))
            n_pages = page_table.shape[1]
            os_ = _sb_decode(qs.reshape(s, N_HEADS, hd, 1), jnp.transpose(cache_k[j], (0, 2, 3, 1)),
                             jnp.transpose(cache_v[j], (0, 2, 3, 1)), page_table, att_sb_bias[j].reshape(N_HEADS, 1),
                             pg=PAGES_PER_STEP if n_pages % PAGES_PER_STEP == 0 else 1)
            wr = jnp.pad(moe_w_router[j], ((0, 0), (0, LANES - N_EXPERTS)))
            br = jnp.pad(moe_b_router[j], (0, LANES - N_EXPERTS), constant_values=NEG_BIG).reshape(1, LANES)
            norms = (vec(norm_mix_post[i]), vec(norm_ffn_pre[i]))
            hp, zp, cols, rows = _attn_out(hp, op.reshape(mp, d), bf(att_w_o[j]), *norms, bf(wr), br, tmp, route=True)
            hs, zs, cs = _attn_out(hs, os_.reshape(s, d), att_w_o[j], *norms, wr, br, s)
            fs, wg, wu, wd = _moe(zs, cs, moe_w_gu[j], moe_w_down[j])
            tme = tmp
            n_tiles = (TOP_K * mp + N_EXPERTS * (tme - 1)) // tme
            slots, source, tile_expert, n_used = _route_plan(rows[0].astype(jnp.int32), rows[1].astype(jnp.int32),
                                                             N_EXPERTS, tme, n_tiles)
            ys = _experts(zp, source, tile_expert, n_used, wg, wu, wd)
            hp = _finish_routed(hp, ys, slots, cols, pp, i, *tail, bf(ple_w_proj[i]), bf(ple_w_gate[i]), tmp)
            hs = _finish(hs, fs, ps, i, *tail, ple_w_proj[i], ple_w_gate[i], s)
    st = lambda k: jnp.stack(outs[k])
    return (hp.reshape(bsz, t, d), hs.reshape(s, 1, d), st("conv_p"), st("lru_p"), st("k_p"), st("v_p"),
            st("conv_s"), st("lru_s"), st("k_s"), st("v_s"))
```

```python
import functools
import math

import jax
import jax.numpy as jnp
from jax import lax
from jax.experimental import pallas as pl
from jax.experimental.pallas import tpu as pltpu

F32 = jnp.float32
BF16 = jnp.bfloat16

RMS_EPS = 1e-6
LRU_C = 8.0
N_HEADS = 16
N_LRU_BLOCKS = 8
CONV_W = 4
N_EXPERTS = 8
TOP_K = 2
LANES = 128
SUBLANES = 8
LOG2E = 1.4426950408889634
NEG_BIG = -1e30
VMEM_LIMIT = 56 * 1024 * 1024

ROW_TILE = 512
SCAN_TILE = 256
KEY_TILE = 256
QUERY_TILE = 4 * KEY_TILE
HEADS_PER_STEP = 4
PAGES_PER_STEP = 16
FF_CHUNK = 512
EXPERT_FF_CHUNK = 1792


def _cparams(*sem):
    return pltpu.CompilerParams(dimension_semantics=sem, vmem_limit_bytes=VMEM_LIMIT)


def _resident(shape):
    return pl.BlockSpec(shape, lambda *_: (0,) * len(shape), pipeline_mode=pl.Buffered(1))


def _dot(a, b):
    return jnp.dot(a, b, preferred_element_type=F32)


def _mm(x, w):
    if w.dtype == F32:
        return jnp.dot(x.astype(F32), w, precision=lax.Precision.HIGHEST, preferred_element_type=F32)
    return jnp.dot(x.astype(w.dtype), w, preferred_element_type=F32)


def _dot_nt(a, b):
    return lax.dot_general(a, b, (((1,), (1,)), ((), ())), preferred_element_type=F32)


def _rms(x, g):
    return x * lax.rsqrt(jnp.mean(x * x, axis=-1, keepdims=True) + RMS_EPS) * g


def _softplus(x):
    return jnp.maximum(x, 0.0) + jnp.log1p(jnp.exp(-jnp.abs(x)))


def _gelu(x):
    c = math.sqrt(2.0 / math.pi)
    return 0.5 * x * (1.0 + jnp.tanh(c * (x + 0.044715 * (x * x * x))))


def _silu(x):
    return x * jax.nn.sigmoid(x)


def _lru_gates(xc, wa_ref, ba, wx_ref, bx, lam):
    blk = xc.shape[1] // N_LRU_BLOCKS
    ra, rx = [], []
    for n in range(N_LRU_BLOCKS):
        xb = xc[:, n * blk:(n + 1) * blk]
        ra.append(_mm(xb, wa_ref[n]))
        rx.append(_mm(xb, wx_ref[n]))
    r = jax.nn.sigmoid(jnp.concatenate(ra, axis=1) + ba)
    ig = jax.nn.sigmoid(jnp.concatenate(rx, axis=1) + bx)
    log_a = -LRU_C * r * _softplus(-lam)
    a = jnp.exp(log_a)
    gx = jnp.sqrt(-jnp.tanh(log_a) * (a * a + 1.0)) * ig * xc
    return a, gx


def _rec_prompt_kernel(x_ref, gpre_ref, win_ref, cw_ref, cb_ref, wa_ref, ba_ref, wx_ref, bx_ref, lam_ref,
                       wout_ref, gpost_ref, h_ref, conv_ref, hlast_ref, tail_ref, hc_ref):
    tt, d = x_ref.shape

    @pl.when(pl.program_id(1) == 0)
    def _():
        tail_ref[...] = jnp.zeros_like(tail_ref)
        hc_ref[...] = jnp.zeros_like(hc_ref)

    x = x_ref[...]
    xn = _rms(x, gpre_ref[...]).astype(win_ref.dtype)
    gate = _gelu(_mm(xn, win_ref[:, :d]))
    xr = _mm(xn, win_ref[:, d:])
    xpad = jnp.concatenate([tail_ref[...], xr], axis=0)
    xc = cb_ref[...] + cw_ref[3:4, :] * xr
    for back in range(1, CONV_W):
        xc = xc + cw_ref[CONV_W - 1 - back:CONV_W - back, :] * xpad[8 - back:8 - back + tt]
    tail_ref[...] = xr[tt - 8:]
    a, b = _lru_gates(xc, wa_ref, ba_ref[...], wx_ref, bx_ref[...], lam_ref[...])
    row = lax.rem(lax.broadcasted_iota(jnp.int32, (tt, 1), 0), SUBLANES)
    s = 1
    while s < SUBLANES:
        keep = row >= s
        a_sh = jnp.where(keep, pltpu.roll(a, s, 0), 1.0)
        b_sh = jnp.where(keep, pltpu.roll(b, s, 0), 0.0)
        b = a * b_sh + b
        a = a * a_sh
        s *= 2
    last = hc_ref[...]
    groups = []
    for g in range(tt // SUBLANES):
        rows = slice(g * SUBLANES, (g + 1) * SUBLANES)
        hg = b[rows] + a[rows] * last
        groups.append(hg)
        last = hg[SUBLANES - 1:]
    h = jnp.concatenate(groups, axis=0)
    hc_ref[...] = last
    y = _mm(h * gate, wout_ref[...])
    h_ref[...] = x + _rms(y, gpost_ref[...])
    conv_ref[...] = xr[tt - (CONV_W - 1):]
    hlast_ref[...] = h[tt - 1:]


def _rec_prompt(x, gpre, win, cw, cb, wa, ba, wx, bx, lam, wout, gpost, tt):
    bsz, t, d = x.shape
    vec = _resident((1, d))
    return pl.pallas_call(
        _rec_prompt_kernel,
        grid=(bsz, t // tt),
        in_specs=[pl.BlockSpec((None, tt, d), lambda b, i: (b, i, 0)), vec, _resident((d, 2 * d)),
                  _resident((CONV_W, d)), vec, _resident(wa.shape), vec, _resident(wx.shape), vec, vec,
                  _resident((d, d)), vec],
        out_specs=[pl.BlockSpec((None, tt, d), lambda b, i: (b, i, 0)),
                   pl.BlockSpec((None, CONV_W - 1, d), lambda b, i: (b, 0, 0)),
                   pl.BlockSpec((None, 1, d), lambda b, i: (b, 0, 0))],
        out_shape=[jax.ShapeDtypeStruct((bsz, t, d), F32), jax.ShapeDtypeStruct((bsz, CONV_W - 1, d), F32),
                   jax.ShapeDtypeStruct((bsz, 1, d), F32)],
        scratch_shapes=[pltpu.VMEM((8, d), F32), pltpu.VMEM((1, d), F32)],
        compiler_params=_cparams("arbitrary", "arbitrary"),
        name="rec_prompt",
    )(x, gpre, win, cw, cb, wa, ba, wx, bx, lam, wout, gpost)


def _rec_step_kernel(x_ref, sc_ref, h0_ref, gpre_ref, win_ref, cw_ref, cb_ref, wa_ref, ba_ref, wx_ref, bx_ref,
                     lam_ref, wout_ref, gpost_ref, h_ref, conv_ref, hnew_ref):
    d = x_ref.shape[1]
    x = x_ref[...]
    xn = _rms(x, gpre_ref[...]).astype(win_ref.dtype)
    gate = _gelu(_mm(xn, win_ref[:, :d]))
    xr = _mm(xn, win_ref[:, d:])
    xc = cb_ref[...] + cw_ref[CONV_W - 1:CONV_W, :] * xr
    for k in range(CONV_W - 1):
        xc = xc + cw_ref[k:k + 1, :] * sc_ref[k]
    a, b = _lru_gates(xc, wa_ref, ba_ref[...], wx_ref, bx_ref[...], lam_ref[...])
    h = a * h0_ref[...] + b
    y = _mm(h * gate, wout_ref[...])
    h_ref[...] = x + _rms(y, gpost_ref[...])
    for k in range(CONV_W - 2):
        conv_ref[k] = sc_ref[k + 1]
    conv_ref[CONV_W - 2] = xr
    hnew_ref[...] = h


def _rec_step(x, sc, h0, gpre, win, cw, cb, wa, ba, wx, bx, lam, wout, gpost):
    rows, d = x.shape
    return pl.pallas_call(
        _rec_step_kernel,
        out_shape=[jax.ShapeDtypeStruct((rows, d), F32), jax.ShapeDtypeStruct((CONV_W - 1, rows, d), F32),
                   jax.ShapeDtypeStruct((rows, d), F32)],
        compiler_params=pltpu.CompilerParams(vmem_limit_bytes=VMEM_LIMIT),
        name="rec_step",
    )(x, sc, h0, gpre, win, cw, cb, wa, ba, wx, bx, lam, wout, gpost)


def _layer_tail(h, f, p_ref, gpost_ref, gple_ref, wproj_ref, wgate_ref):
    h2 = h + _rms(f, gpost_ref[...])
    e = _mm(p_ref[...], wproj_ref[...])
    g = jax.nn.sigmoid(_mm(_rms(h2, gple_ref[...]), wgate_ref[...]))
    return h2 + g * e


def _ffn_layer_kernel(h_ref, g_ref, wgu_ref, wd_ref, p_ref, gpost_ref, gple_ref, wproj_ref, wgate_ref, o_ref, *, ck):
    ff = wd_ref.shape[0]
    h = h_ref[...]
    z = _rms(h, g_ref[...]).astype(wgu_ref.dtype)
    acc = jnp.zeros(h.shape, F32)
    for c in range(ff // ck):
        g = _mm(z, wgu_ref[:, c * ck:(c + 1) * ck])
        u = _mm(z, wgu_ref[:, ff + c * ck:ff + (c + 1) * ck])
        acc = acc + _mm(_silu(g) * u, wd_ref[c * ck:(c + 1) * ck, :])
    o_ref[...] = _layer_tail(h, acc, p_ref, gpost_ref, gple_ref, wproj_ref, wgate_ref)


def _p_spec(p, layer, tm):
    return pl.BlockSpec((None, tm, p.shape[2]), lambda i: (layer, i, 0))


def _ffn_layer(h, g, wgu, wd, p, layer, gpost, gple, wproj, wgate, tm, ck=FF_CHUNK):
    m, d = h.shape
    row = pl.BlockSpec((tm, d), lambda i: (i, 0))
    vec = _resident((1, d))
    return pl.pallas_call(
        functools.partial(_ffn_layer_kernel, ck=ck),
        grid=(m // tm,),
        in_specs=[row, vec, _resident(wgu.shape), _resident(wd.shape), _p_spec(p, layer, tm), vec, vec,
                  _resident(wproj.shape), _resident(wgate.shape)],
        out_specs=row,
        out_shape=jax.ShapeDtypeStruct((m, d), F32),
        compiler_params=_cparams("arbitrary"),
        name="ffn_layer",
    )(h, g, wgu, wd, p, gpost, gple, wproj, wgate)


def _finish_kernel(h_ref, f_ref, p_ref, gpost_ref, gple_ref, wproj_ref, wgate_ref, o_ref):
    o_ref[...] = _layer_tail(h_ref[...], f_ref[...], p_ref, gpost_ref, gple_ref, wproj_ref, wgate_ref)


def _finish(h, f, p, layer, gpost, gple, wproj, wgate, tm):
    m, d = h.shape
    row = pl.BlockSpec((tm, d), lambda i: (i, 0))
    return pl.pallas_call(
        _finish_kernel,
        grid=(m // tm,),
        in_specs=[row, row, _p_spec(p, layer, tm), _resident((1, d)), _resident((1, d)),
                  _resident(wproj.shape), _resident(wgate.shape)],
        out_specs=row,
        out_shape=jax.ShapeDtypeStruct((m, d), F32),
        compiler_params=_cparams("arbitrary"),
        name="finish",
    )(h, f, p, gpost, gple, wproj, wgate)


def _qkv_kernel(h_ref, g_ref, w_ref, k_ref, v_ref, qb_ref, kb_ref, vb_ref):
    d = h_ref.shape[1]
    xn = _rms(h_ref[...], g_ref[...]).astype(w_ref.dtype)
    scale = (d // N_HEADS) ** -0.5
    qb_ref[...] = (_mm(xn, w_ref[:, :d]) * scale).astype(qb_ref.dtype)
    k = _mm(xn, w_ref[:, d:2 * d])
    v = _mm(xn, w_ref[:, 2 * d:])
    if k_ref.shape == k.shape:
        k_ref[...] = k
        v_ref[...] = v
    else:
        k_ref[...] = k.T
        v_ref[...] = v.T
    kb_ref[...] = k.astype(BF16)
    vb_ref[...] = v.astype(BF16)


def _qkv(h, g, w, tm, q_dtype=BF16, seq=None):
    m, d = h.shape
    row = pl.BlockSpec((tm, d), lambda i: (i, 0))
    if seq is None:
        kv_spec, kv_shape = row, jax.ShapeDtypeStruct((m, d), F32)
    else:
        per = seq // tm
        kv_spec = pl.BlockSpec((None, d, tm), lambda i: (i // per, 0, i % per))
        kv_shape = jax.ShapeDtypeStruct((m // seq, d, seq), F32)
    return pl.pallas_call(
        _qkv_kernel,
        grid=(m // tm,),
        in_specs=[row, _resident((1, d)), _resident(w.shape)],
        out_specs=[kv_spec, kv_spec, row, row, row],
        out_shape=[kv_shape, kv_shape, jax.ShapeDtypeStruct((m, d), q_dtype)]
        + [jax.ShapeDtypeStruct((m, d), BF16)] * 2,
        compiler_params=_cparams("arbitrary"),
        name="qkv",
    )(h, g, w)


def _sb_weights(z, sp, ls, tri, carry):
    hi = ls.astype(BF16)
    lo = (ls - hi.astype(F32)).astype(BF16)
    later = _dot(hi, tri) + _dot(lo, tri)
    return jnp.exp((z - sp) + later + carry)


def _sb_prompt_kernel(bias_ref, q_ref, k_ref, v_ref, o_ref, qs_ref, w_ref, acc_ref, carry_ref, *, hps, tk):
    tq, width = q_ref.shape
    hd = width // hps
    rows = hps * tq
    per = tq // tk
    grp = pl.program_id(1)
    i = pl.program_id(2)
    lane = lax.broadcasted_iota(jnp.int32, (tq, width), 1)
    q = q_ref[...]
    for h in range(hps):
        qs_ref[h * tq:(h + 1) * tq, :] = jnp.where((lane >= h * hd) & (lane < (h + 1) * hd), q, jnp.zeros_like(q))
    acc_ref[...] = jnp.zeros_like(acc_ref)
    carry_ref[...] = jnp.zeros_like(carry_ref)
    r2 = lax.broadcasted_iota(jnp.int32, (tk, tk), 0)
    c2 = lax.broadcasted_iota(jnp.int32, (tk, tk), 1)
    neg_tri = jnp.where(r2 > c2, -1.0, 0.0).astype(BF16)
    ahead = (lax.broadcasted_iota(jnp.int32, (rows, tk), 1)
             - lax.rem(lax.broadcasted_iota(jnp.int32, (rows, tk), 0), tq))
    bias = jnp.concatenate([jnp.full((tq, 1), bias_ref[hps * grp + h], F32) for h in range(hps)], axis=0)

    def weights(j, diag, lo=0):
        n = tq - lo

        def part(x):
            return x[...] if lo == 0 else jnp.concatenate([x[h * tq + lo:(h + 1) * tq] for h in range(hps)], axis=0)

        z = _dot_nt(part(qs_ref), k_ref[pl.ds(pl.multiple_of(j * tk, tk), tk), :]) + part(bias)
        sp = jnp.maximum(z, 0.0) + jnp.log(1.0 + jnp.exp2(jnp.abs(z) * -LOG2E))
        if diag:
            causal = part(ahead) < i * tq - j * tk
            sp = jnp.where(causal, sp, 0.0)
        later = _dot(sp.astype(BF16), neg_tri)
        w = jnp.exp((z - sp) + later + part(carry_ref))
        if diag:
            w = jnp.where(causal, w, 0.0)
        w = w.astype(BF16)
        total = jnp.sum(sp, axis=-1, keepdims=True)
        if lo == 0:
            w_ref[...] = w
            carry_ref[...] -= total
        else:
            for h in range(hps):
                w_ref[h * tq:h * tq + lo, :] = jnp.zeros((lo, tk), BF16)
                w_ref[h * tq + lo:(h + 1) * tq, :] = w[h * n:(h + 1) * n]
                carry_ref[h * tq + lo:(h + 1) * tq, :] -= total[h * n:(h + 1) * n]

    def values(j):
        acc_ref[...] += _dot(w_ref[...], v_ref[pl.ds(pl.multiple_of(j * tk, tk), tk), :])

    top = (i + 1) * per - 1
    weights(top, True, lo=(per - 1) * tk)
    for back in range(1, per):
        values(top - back + 1)
        weights(top - back, True, lo=(per - 1 - back) * tk)

    def body(jj, c):
        j = i * per - 1 - jj
        values(j + 1)
        weights(j, False)
        return c

    lax.fori_loop(0, i * per, body, 0)
    values(0)
    out = acc_ref[0:tq, :]
    for h in range(1, hps):
        out = jnp.where(lane >= h * hd, acc_ref[h * tq:(h + 1) * tq, :], out)
    o_ref[...] = out.astype(o_ref.dtype)


def _sb_prompt(qb, kb, vb, bias, tq, tk, hps=HEADS_PER_STEP):
    bsz, t, d = qb.shape
    width = hps * (d // N_HEADS)
    qspec = pl.BlockSpec((None, tq, width), lambda b, g, i: (b, i, g))
    kvspec = pl.BlockSpec((None, t, width), lambda b, g, i: (b, 0, g))
    return pl.pallas_call(
        functools.partial(_sb_prompt_kernel, hps=hps, tk=tk),
        grid=(bsz, d // width, t // tq),
        in_specs=[pl.BlockSpec(memory_space=pltpu.SMEM), qspec, kvspec, kvspec],
        out_specs=qspec,
        out_shape=jax.ShapeDtypeStruct((bsz, t, d), BF16),
        scratch_shapes=[pltpu.VMEM((hps * tq, width), BF16), pltpu.VMEM((hps * tq, tk), BF16),
                        pltpu.VMEM((hps * tq, width), F32),
                        pltpu.VMEM((hps * tq, 1), F32)],
        compiler_params=_cparams("arbitrary", "arbitrary", "arbitrary"),
        name="sb_prompt",
    )(bias, qb, kb, vb)


def _sb_decode_kernel(pt_ref, q_ref, bias_ref, *refs, pg):
    del pt_ref
    k_refs, v_refs = refs[:pg], refs[pg:2 * pg]
    o_ref, qb_ref, acc_ref, carry_ref = refs[2 * pg:]
    nh, _, page = k_refs[0].shape
    g = pl.program_id(1)

    @pl.when(g == 0)
    def _():
        acc_ref[...] = jnp.zeros_like(acc_ref)
        carry_ref[...] = jnp.zeros_like(carry_ref)
        qb_ref[...] = jnp.broadcast_to(q_ref[...], qb_ref.shape)

    row = lax.broadcasted_iota(jnp.int32, (page, page), 0)
    col = lax.broadcasted_iota(jnp.int32, (page, page), 1)
    tri = jnp.where(row > col, 1.0, 0.0).astype(BF16)
    z = jnp.concatenate([jnp.sum(qb_ref[h] * k_refs[r][h], axis=0, keepdims=True)
                         for r in range(pg) for h in range(nh)], axis=0)
    z = z + jnp.concatenate([bias_ref[...]] * pg, axis=0)
    sp = _softplus(z)
    ls = -sp
    total = jnp.sum(ls, axis=-1, keepdims=True)
    carries = [carry_ref[...]]
    for r in range(pg):
        carries.append(carries[-1] + total[r * nh:(r + 1) * nh])
    carry_ref[...] = carries[pg]
    w = _sb_weights(z, sp, ls, tri, jnp.concatenate(carries[:pg], axis=0))
    for h in range(nh):
        part = w[h:h + 1, :] * v_refs[0][h]
        for r in range(1, pg):
            part = part + w[r * nh + h:r * nh + h + 1, :] * v_refs[r][h]
        acc_ref[h] += part

    @pl.when(g == pl.num_programs(1) - 1)
    def _():
        o_ref[...] = jnp.sum(acc_ref[...], axis=-1)


def _sb_decode(q, cache_k, cache_v, page_table, bias, pg):
    s, nh, hd, _ = q.shape
    n_pages = page_table.shape[1]
    page = cache_k.shape[3]

    def page_spec(r):
        return pl.BlockSpec((None, nh, hd, page), lambda b, g, pt: (pt[b, n_pages - 1 - (g * pg + r)], 0, 0, 0))

    grid_spec = pltpu.PrefetchScalarGridSpec(
        num_scalar_prefetch=1,
        grid=(s, n_pages // pg),
        in_specs=[pl.BlockSpec((None, nh, hd, 1), lambda b, g, pt: (b, 0, 0, 0)),
                  pl.BlockSpec((nh, 1), lambda b, g, pt: (0, 0))] + [page_spec(r) for r in range(pg)] * 2,
        out_specs=pl.BlockSpec((None, nh, hd), lambda b, g, pt: (b, 0, 0)),
        scratch_shapes=[pltpu.VMEM((nh, hd, page), F32), pltpu.VMEM((nh, hd, page), F32), pltpu.VMEM((nh, 1), F32)],
    )
    return pl.pallas_call(
        functools.partial(_sb_decode_kernel, pg=pg),
        grid_spec=grid_spec,
        out_shape=jax.ShapeDtypeStruct((s, nh, hd), F32),
        compiler_params=_cparams("arbitrary", "arbitrary"),
        name="sb_decode",
    )(page_table, q, bias, *([cache_k] * pg), *([cache_v] * pg))


def _mix_residual(h_ref, o_ref, wo_ref, gpost_ref, gffn_ref):
    h1 = h_ref[...] + _rms(_mm(o_ref[...], wo_ref[...]), gpost_ref[...])
    return h1, _rms(h1, gffn_ref[...])


def _top2(logits):
    idx = lax.broadcasted_iota(jnp.int32, logits.shape, 1)
    m1 = jnp.max(logits, axis=-1, keepdims=True)
    i1 = jnp.min(jnp.where(logits == m1, idx, LANES), axis=-1, keepdims=True)
    rest = jnp.where(idx == i1, NEG_BIG, logits)
    m2 = jnp.max(rest, axis=-1, keepdims=True)
    i2 = jnp.min(jnp.where(rest == m2, idx, LANES), axis=-1, keepdims=True)
    e = jnp.exp(m2 - m1)
    w1 = 1.0 / (1.0 + e)
    return idx, i1, i2, w1, e * w1


def _attn_out_kernel(h_ref, o_ref, wo_ref, gpost_ref, gffn_ref, wr_ref, br_ref, h1_ref, z_ref, comb_ref):
    h1, z = _mix_residual(h_ref, o_ref, wo_ref, gpost_ref, gffn_ref)
    h1_ref[...] = h1
    z_ref[...] = z.astype(z_ref.dtype)
    idx, i1, i2, w1, w2 = _top2(_mm(z, wr_ref[...]) + br_ref[...])
    comb_ref[...] = jnp.where(idx == i1, w1, 0.0) + jnp.where(idx == i2, w2, 0.0)


def _attn_out_route_kernel(h_ref, o_ref, wo_ref, gpost_ref, gffn_ref, wr_ref, br_ref, h1_ref, z_ref, cols_ref,
                           rows_ref):
    h1, z = _mix_residual(h_ref, o_ref, wo_ref, gpost_ref, gffn_ref)
    h1_ref[...] = h1
    z_ref[...] = z
    idx, i1, i2, w1, w2 = _top2(_mm(z, wr_ref[...]) + br_ref[...])
    cols = jnp.where(idx == 0, i1.astype(F32), jnp.where(idx == 1, i2.astype(F32),
                                                         jnp.where(idx == 2, w1, jnp.where(idx == 3, w2, 0.0))))
    cols_ref[...] = cols
    pick = (lax.broadcasted_iota(jnp.int32, (8, LANES), 0) == lax.broadcasted_iota(jnp.int32, (8, LANES), 1))
    rows_ref[...] = lax.dot_general(pick.astype(F32), cols, (((1,), (1,)), ((), ())),
                                    precision=lax.Precision.HIGHEST, preferred_element_type=F32)


def _attn_out(h, o, wo, gpost, gffn, wr, br, tm, route=False):
    m, d = h.shape
    row = pl.BlockSpec((tm, d), lambda i: (i, 0))
    comb = pl.BlockSpec((tm, LANES), lambda i: (i, 0))
    in_specs = [row, row, _resident(wo.shape), _resident((1, d)), _resident((1, d)), _resident(wr.shape),
                _resident(br.shape)]
    if route:
        return pl.pallas_call(
            _attn_out_route_kernel,
            grid=(m // tm,),
            in_specs=in_specs,
            out_specs=[row, row, comb, pl.BlockSpec((8, tm), lambda i: (0, i))],
            out_shape=[jax.ShapeDtypeStruct((m, d), F32), jax.ShapeDtypeStruct((m, d), F32),
                       jax.ShapeDtypeStruct((m, LANES), F32), jax.ShapeDtypeStruct((8, m), F32)],
            compiler_params=_cparams("arbitrary"),
            name="attn_out_route",
        )(h, o, wo, gpost, gffn, wr, br)
    return pl.pallas_call(
        _attn_out_kernel,
        grid=(m // tm,),
        in_specs=in_specs,
        out_specs=[row, row, comb],
        out_shape=[jax.ShapeDtypeStruct((m, d), F32), jax.ShapeDtypeStruct((m, d), BF16),
                   jax.ShapeDtypeStruct((m, LANES), F32)],
        compiler_params=_cparams("arbitrary"),
        name="attn_out",
    )(h, o, wo, gpost, gffn, wr, br)


def _moe_kernel(z_ref, comb_ref, wg_ref, wu_ref, wd_ref, f_ref, wgb_ref, wub_ref, wdb_ref):
    e = pl.program_id(0)
    c = pl.program_id(1)

    @pl.when((e == 0) & (c == 0))
    def _():
        f_ref[...] = jnp.zeros_like(f_ref)

    wg = wg_ref[...].astype(BF16)
    wu = wu_ref[...].astype(BF16)
    wd = wd_ref[...].astype(BF16)
    wgb_ref[...] = wg
    wub_ref[...] = wu
    wdb_ref[...] = wd
    z = z_ref[...]
    act = (_silu(_dot(z, wg)) * _dot(z, wu)).astype(BF16)
    lane = lax.broadcasted_iota(jnp.int32, comb_ref.shape, 1)
    ce = jnp.sum(jnp.where(lane == e, comb_ref[...], 0.0), axis=-1, keepdims=True)
    f_ref[...] += ce * _dot(act, wd)


def _moe(z, comb, wgu, wd, ck=FF_CHUNK):
    m, d = z.shape
    n_e, ff, _ = wd.shape
    fc = ff // ck
    rows = pl.BlockSpec((m, d), lambda e, c: (0, 0))
    up_down = [pl.BlockSpec((None, d, ck), lambda e, c: (e, 0, c)), pl.BlockSpec((None, ck, d), lambda e, c: (e, c, 0))]
    return pl.pallas_call(
        _moe_kernel,
        grid=(n_e, fc),
        in_specs=[rows, pl.BlockSpec((m, LANES), lambda e, c: (0, 0)), up_down[0],
                  pl.BlockSpec((None, d, ck), lambda e, c: (e, 0, fc + c)), up_down[1]],
        out_specs=[rows, up_down[0], up_down[0], up_down[1]],
        out_shape=[jax.ShapeDtypeStruct((m, d), F32), jax.ShapeDtypeStruct((n_e, d, ff), BF16),
                   jax.ShapeDtypeStruct((n_e, d, ff), BF16), jax.ShapeDtypeStruct((n_e, ff, d), BF16)],
        compiler_params=_cparams("arbitrary", "arbitrary"),
        name="moe",
    )(z, comb, wgu, wgu, wd)


def _route_plan(i1, i2, n_e, tm, n_tiles):
    m = i1.shape[0]
    experts = jnp.arange(n_e, dtype=jnp.int32)[None, :]
    hit = (experts == i1[:, None]).astype(jnp.int32) + (experts == i2[:, None]).astype(jnp.int32)
    upto = jnp.cumsum(hit, axis=0)
    before = upto - hit
    padded = ((upto[-1] + tm - 1) // tm) * tm
    ends = jnp.cumsum(padded)
    starts = ends - padded
    slot1 = starts[i1] + jnp.take_along_axis(before, i1[:, None], axis=1)[:, 0]
    slot2 = starts[i2] + jnp.take_along_axis(before, i2[:, None], axis=1)[:, 0]
    slots = jnp.stack([slot1, slot2]).astype(jnp.int32)
    token = jnp.tile(jnp.arange(m, dtype=jnp.int32), 2)
    source = jnp.zeros((n_tiles * tm,), jnp.int32).at[slots.reshape(-1)].set(token).reshape(n_tiles, 1, tm)
    tile_start = jnp.arange(n_tiles, dtype=jnp.int32) * tm
    tile_expert = jnp.minimum(jnp.sum(tile_start[:, None] >= ends[None, :], axis=1), n_e - 1).astype(jnp.int32)
    return slots, source, tile_expert, (ends[-1:] // tm).astype(jnp.int32)


GATHER_UNROLL = 8


def _start_row_gather(src_ref, row_of, dst_ref, sem):
    rows = dst_ref.shape[0]

    def body(r, carry):
        pltpu.make_async_copy(src_ref.at[pl.ds(row_of(r), 1), :], dst_ref.at[pl.ds(r, 1), :], sem).start()
        return carry

    lax.fori_loop(0, rows, body, 0, unroll=GATHER_UNROLL)


def _wait_row_gather(src_ref, dst_ref, sem):
    pltpu.make_async_copy(src_ref.at[pl.ds(0, dst_ref.shape[0]), :], dst_ref, sem).wait()


def _experts_kernel(te_ref, nu_ref, src_ref, nxt_ref, z_ref, wg_ref, wu_ref, wd_ref, y_ref, xg_ref, xb_ref, sem):
    del te_ref
    i = pl.program_id(0)
    c = pl.program_id(1)
    n_used = nu_ref[0]
    used = i < n_used

    @pl.when(used & (c == 0))
    def _():
        buf = i % 2

        @pl.when(i == 0)
        def _():
            _start_row_gather(z_ref, lambda r: src_ref[0, r], xg_ref.at[0], sem.at[0])

        _wait_row_gather(z_ref, xg_ref.at[buf], sem.at[buf])
        xb_ref[...] = xg_ref[buf].astype(BF16)

        @pl.when(i + 1 < n_used)
        def _():
            _start_row_gather(z_ref, lambda r: nxt_ref[0, r], xg_ref.at[1 - buf], sem.at[1 - buf])

    @pl.when(used)
    def _():
        xb = xb_ref[...]
        act = (_silu(_dot(xb, wg_ref[...])) * _dot(xb, wu_ref[...])).astype(BF16)
        y = _dot(act, wd_ref[...])

        @pl.when(c == 0)
        def _():
            y_ref[...] = y

        @pl.when(c > 0)
        def _():
            y_ref[...] += y

    @pl.when(jnp.logical_not(used) & (c == 0))
    def _():
        y_ref[...] = jnp.zeros_like(y_ref)


def _experts(z, source, tile_expert, n_used, wg, wu, wd, ck=EXPERT_FF_CHUNK):
    n_tiles, _, tm = source.shape
    d = z.shape[1]
    ff = wd.shape[1]
    assert ff % ck == 0, (ff, ck)
    fc = ff // ck
    grid_spec = pltpu.PrefetchScalarGridSpec(
        num_scalar_prefetch=2,
        grid=(n_tiles, fc),
        in_specs=[pl.BlockSpec((None, 1, tm), lambda i, c, te, nu: (i, 0, 0), memory_space=pltpu.SMEM),
                  pl.BlockSpec((None, 1, tm), lambda i, c, te, nu: (jnp.minimum(i + 1, n_tiles - 1), 0, 0),
                               memory_space=pltpu.SMEM),
                  pl.BlockSpec(memory_space=pl.ANY),
                  pl.BlockSpec((None, d, ck), lambda i, c, te, nu: (te[i], 0, c)),
                  pl.BlockSpec((None, d, ck), lambda i, c, te, nu: (te[i], 0, c)),
                  pl.BlockSpec((None, ck, d), lambda i, c, te, nu: (te[i], c, 0))],
        out_specs=pl.BlockSpec((tm, d), lambda i, c, te, nu: (i, 0)),
        scratch_shapes=[pltpu.VMEM((2, tm, d), z.dtype), pltpu.VMEM((tm, d), BF16), pltpu.SemaphoreType.DMA((2,))],
    )
    return pl.pallas_call(
        _experts_kernel,
        grid_spec=grid_spec,
        out_shape=jax.ShapeDtypeStruct((n_tiles * tm, d), F32),
        compiler_params=_cparams("arbitrary", "arbitrary"),
        name="moe_experts",
    )(tile_expert, n_used, source, source, z, wg, wu, wd)


def _finish_routed_kernel(slots_ref, nxt_ref, h_ref, cols_ref, p_ref, gpost_ref, gple_ref, wproj_ref, wgate_ref,
                          ys_ref, o_ref, y_ref, sem):
    i = pl.program_id(0)
    buf = i % 2

    def start(rows_ref, b):
        for k in range(TOP_K):
            _start_row_gather(ys_ref, lambda r, k=k: rows_ref[k, r], y_ref.at[b, k], sem.at[b])

    @pl.when(i == 0)
    def _():
        start(slots_ref, 0)

    for k in range(TOP_K):
        _wait_row_gather(ys_ref, y_ref.at[buf, k], sem.at[buf])

    @pl.when(i + 1 < pl.num_programs(0))
    def _():
        start(nxt_ref, 1 - buf)

    cols = cols_ref[...]
    f = cols[:, 2:3] * y_ref[buf, 0] + cols[:, 3:4] * y_ref[buf, 1]
    o_ref[...] = _layer_tail(h_ref[...], f, p_ref, gpost_ref, gple_ref, wproj_ref, wgate_ref)


def _finish_routed(h, ys, slots, cols, p, layer, gpost, gple, wproj, wgate, tm):
    m, d = h.shape
    n = m // tm
    row = pl.BlockSpec((tm, d), lambda i: (i, 0))
    return pl.pallas_call(
        _finish_routed_kernel,
        grid=(n,),
        in_specs=[pl.BlockSpec((TOP_K, tm), lambda i: (0, i), memory_space=pltpu.SMEM),
                  pl.BlockSpec((TOP_K, tm), lambda i: (0, jnp.minimum(i + 1, n - 1)), memory_space=pltpu.SMEM), row,
                  pl.BlockSpec((tm, LANES), lambda i: (i, 0)), _p_spec(p, layer, tm),
                  _resident((1, d)), _resident((1, d)), _resident(wproj.shape), _resident(wgate.shape),
                  pl.BlockSpec(memory_space=pl.ANY)],
        out_specs=row,
        out_shape=jax.ShapeDtypeStruct((m, d), F32),
        scratch_shapes=[pltpu.VMEM((2, TOP_K, tm, d), F32), pltpu.SemaphoreType.DMA((2,))],
        compiler_params=_cparams("arbitrary"),
        name="finish_routed",
    )(slots, slots, h, cols, p, gpost, gple, wproj, wgate, ys)


def _tile(m, pref):
    return pref if m % pref == 0 else m


def kernel(x_prompt, x_sample, state_conv, state_lru, cache_k, cache_v, page_table, p_prompt, p_sample, norm_mix_pre, norm_mix_post, norm_ffn_pre, norm_ffn_post, norm_ple, rec_w_in, rec_conv_w, rec_conv_b, rec_w_a, rec_b_a, rec_w_x, rec_b_x, rec_lambda, rec_w_out, att_w_qkv, att_w_o, att_sb_bias, ffn_w_gu, ffn_w_down, moe_w_router, moe_b_router, moe_w_gu, moe_w_down, ple_w_proj, ple_w_gate):
    bsz, t, d = x_prompt.shape
    s = x_sample.shape[0]
    depth = norm_mix_pre.shape[0]
    mp = bsz * t
    tmp = _tile(mp, ROW_TILE)
    vec = lambda a: a.reshape(1, -1)
    bf = lambda a: a.astype(BF16)
    hp = x_prompt.reshape(mp, d)
    hs = x_sample.reshape(s, d)
    keep = lambda a: a
    outs = {k: [] for k in ("conv_p", "lru_p", "k_p", "v_p", "conv_s", "lru_s", "k_s", "v_s")}
    pp = p_prompt.reshape(depth, mp, -1)
    ps = p_sample.reshape(depth, s, -1)
    for i in range(depth):
        j = i // 2
        tail = (vec(norm_ffn_post[i]), vec(norm_ple[i]))
        if i % 2 == 0:
            def rec(cast):
                return (vec(norm_mix_pre[i]), cast(rec_w_in[j]), rec_conv_w[j], vec(rec_conv_b[j]), cast(rec_w_a[j]),
                        vec(rec_b_a[j]), cast(rec_w_x[j]), vec(rec_b_x[j]), vec(rec_lambda[j]), cast(rec_w_out[j]),
                        vec(norm_mix_post[i]))
            hp, conv_new, h_new = _rec_prompt(hp.reshape(bsz, t, d), *rec(bf), tt=_tile(t, SCAN_TILE))
            outs["conv_p"].append(conv_new)
            outs["lru_p"].append(h_new.reshape(bsz, d))
            hs, conv_new, h_new = _rec_step(hs, jnp.swapaxes(state_conv[j], 0, 1), state_lru[j], *rec(keep))
            outs["conv_s"].append(jnp.swapaxes(conv_new, 0, 1))
            outs["lru_s"].append(h_new)
            hp = _ffn_layer(hp.reshape(mp, d), vec(norm_ffn_pre[i]), bf(ffn_w_gu[j]), bf(ffn_w_down[j]), pp, i, *tail,
                            bf(ple_w_proj[i]), bf(ple_w_gate[i]), tmp)
            hs = _ffn_layer(hs, vec(norm_ffn_pre[i]), ffn_w_gu[j], ffn_w_down[j], ps, i, *tail, ple_w_proj[i],
                            ple_w_gate[i], s)
        else:
            hd = d // N_HEADS
            kp, vp, qb, kb, vb = _qkv(hp, vec(norm_mix_pre[i]), bf(att_w_qkv[j]), _tile(t, ROW_TILE), seq=t)
            outs["k_p"].append(kp.reshape(bsz, N_HEADS, hd, t).transpose(0, 3, 1, 2))
            outs["v_p"].append(vp.reshape(bsz, N_HEADS, hd, t).transpose(0, 3, 1, 2))
            op = _sb_prompt(qb.reshape(bsz, t, d), kb.reshape(bsz, t, d), vb.reshape(bsz, t, d), att_sb_bias[j],
                            tq=_tile(t, QUERY_TILE), tk=_tile(t, KEY_TILE))
            ks, vs, qs, _, _ = _qkv(hs, vec(norm_mix_pre[i]), att_w_qkv[j], s, q_dtype=F32)
            outs["k_s"].append(ks.reshape(s, 1, N_HEADS, hd))
            outs["v_s"].append(vs.reshape(s, 1, N_HEADS, hd))
            n_pages = page_table.shape[1]
            os_ = _sb_decode(qs.reshape(s, N_HEADS, hd, 1), jnp.transpose(cache_k[j], (0, 2, 3, 1)),
                             jnp.transpose(cache_v[j], (0, 2, 3, 1)), page_table, att_sb_bias[j].reshape(N_HEADS, 1),
                             pg=PAGES_PER_STEP if n_pages % PAGES_PER_STEP == 0 else 1)
            wr = jnp.pad(moe_w_router[j], ((0, 0), (0, LANES - N_EXPERTS)))
            br = jnp.pad(moe_b_router[j], (0, LANES - N_EXPERTS), constant_values=NEG_BIG).reshape(1, LANES)
            norms = (vec(norm_mix_post[i]), vec(norm_ffn_pre[i]))
            hp, zp, cols, rows = _attn_out(hp, op.reshape(mp, d), bf(att_w_o[j]), *norms, bf(wr), br, tmp, route=True)
            hs, zs, cs = _attn_out(hs, os_.reshape(s, d), att_w_o[j], *norms, wr, br, s)
            fs, wg, wu, wd = _moe(zs, cs, moe_w_gu[j], moe_w_down[j])
            tme = tmp
            n_tiles = (TOP_K * mp + N_EXPERTS * (tme - 1)) // tme
            slots, source, tile_expert, n_used = _route_plan(rows[0].astype(jnp.int32), rows[1].astype(jnp.int32),
                                                             N_EXPERTS, tme, n_tiles)
            ys = _experts(zp, source, tile_expert, n_used, wg, wu, wd)
            hp = _finish_routed(hp, ys, slots, cols, pp, i, *tail, bf(ple_w_proj[i]), bf(ple_w_gate[i]), tmp)
            hs = _finish(hs, fs, ps, i, *tail, ple_w_proj[i], ple_w_gate[i], s)
    st = lambda k: jnp.stack(outs[k])
    return (hp.reshape(bsz, t, d), hs.reshape(s, 1, d), st("conv_p"), st("lru_p"), st("k_p"), st("v_p"),
            st("conv_s"), st("lru_s"), st("k_s"), st("v_s"))
```
